```python
import jax
import jax.numpy as jnp
from jax import lax
import numpy as np

D_MODEL = 1024
BATCH = 8
SEQ = 4096
DEPTH = 4

CTX_LEN = 256
GRID_W = 64
HEAD_DIM = 64
EPS = 1e-6

GLA_HEADS = 4
GLA_DK = 64
GLA_DV = 128
GLA_RANK = 16
GLA_TAU = 16.0
GLA_CHUNK = 64

NA_HEADS = 8
NA_KH = 8
NA_KW = 16
NA_QCOLS = 16
NA_KCOLS = NA_QCOLS + NA_KW

SW_HEADS = 16
SW_KV_HEADS = 2
SW_WINDOW = 128
SW_BLOCK = 128
ROPE_BASE = 10000.0

N_EXPERTS = 32
TOP_K = 4
D_EXPERT = 1024
SWIGLU_LIMIT = 7.0
SWIGLU_ALPHA = 1.702
MOE_BLOCK = 512

N_EVEN = (DEPTH + 1) // 2
N_ODD = DEPTH // 2
GLA_QK = GLA_HEADS * GLA_DK
GLA_V = GLA_HEADS * GLA_DV
NA_W = NA_HEADS * HEAD_DIM
EVEN_IN = 2 * GLA_QK + 2 * GLA_V + 2 * GLA_RANK + 3 * NA_W
EVEN_MIX = GLA_V + NA_W
SW_Q = SW_HEADS * HEAD_DIM
SW_KV = SW_KV_HEADS * HEAD_DIM
ODD_IN = SW_Q + 2 * SW_KV
ODD_MIX = SW_Q

kernel_name = 'hybrid_gla_natten_swa_moe_dit'


def rmsnorm(x, g):
    xf = x.astype(jnp.float32)
    y = xf * lax.rsqrt(jnp.mean(xf * xf, axis=-1, keepdims=True) + EPS)
    return (y * g.astype(jnp.float32)).astype(x.dtype)


def split_heads(x, h):
    b, t, _ = x.shape
    return x.reshape(b, t, h, -1).transpose(0, 2, 1, 3)


def merge_heads(x):
    b, h, t, d = x.shape
    return x.transpose(0, 2, 1, 3).reshape(b, t, h * d)


def axial_rope(x):
    t = jnp.arange(x.shape[2])
    pos = jnp.stack([t // GRID_W, t % GRID_W], axis=-1).astype(jnp.float32)
    quarter = HEAD_DIM // 4
    inv_freq = ROPE_BASE ** (-jnp.arange(quarter, dtype=jnp.float32) / quarter)
    ang = pos[:, :, None] * inv_freq
    cos, sin = jnp.cos(ang), jnp.sin(ang)
    xa = x.astype(jnp.float32).reshape(*x.shape[:-1], 2, 2, quarter)
    x1, x2 = xa[..., 0, :], xa[..., 1, :]
    out = jnp.stack([x1 * cos - x2 * sin, x1 * sin + x2 * cos], axis=-2)
    return out.reshape(x.shape).astype(x.dtype)


def gla_chunked(q, k, v, log_a, s0):
    b, h, t, dk = q.shape
    dv = v.shape[-1]
    n = t // GLA_CHUNK

    def to_chunks(a):
        return a.reshape(b, h, n, GLA_CHUNK, a.shape[-1]).transpose(2, 0, 1, 3, 4)

    causal = jnp.tril(jnp.ones((GLA_CHUNK, GLA_CHUNK), dtype=bool))

    def step(s, inp):
        qc, kc, vc, gc = inp
        cum = jnp.cumsum(gc, axis=-2)
        o_inter = jnp.einsum('bhcd,bhde->bhce', qc * jnp.exp(cum), s)
        diff = cum[:, :, :, None, :] - cum[:, :, None, :, :]
        decay = jnp.exp(jnp.where(causal[:, :, None], diff, -jnp.inf))
        attn = jnp.einsum('bhid,bhjd,bhijd->bhij', qc, kc, decay)
        o = o_inter + jnp.einsum('bhij,bhje->bhie', attn, vc)
        last = cum[:, :, -1:, :]
        s_new = jnp.exp(last[:, :, 0, :, None]) * s + jnp.einsum('bhcd,bhce->bhde', kc * jnp.exp(last - cum), vc)
        return s_new, o

    s_fin, o = lax.scan(step, s0, (to_chunks(q), to_chunks(k), to_chunks(v), to_chunks(log_a)))
    return o.transpose(1, 2, 0, 3, 4).reshape(b, h, t, dv), s_fin


def bidirectional_gla(lat, ctx):
    ql, kl, vl, (al_f, al_b) = lat
    qc, kc, vc, (ac_f, ac_b) = ctx
    b, h, _, dk = ql.shape
    s0 = jnp.zeros((b, h, dk, vl.shape[-1]), ql.dtype)
    rev = lambda a: jnp.flip(a, axis=2)
    oc_f, sc_f = gla_chunked(qc, kc, vc, ac_f, s0)
    oc_b, sc_b = gla_chunked(rev(qc), rev(kc), rev(vc), rev(ac_b), s0)
    ol_f, _ = gla_chunked(ql, kl, vl, al_f, sc_f)
    ol_b, _ = gla_chunked(rev(ql), rev(kl), rev(vl), rev(al_b), sc_b)
    return ol_f + rev(ol_b), oc_f + rev(oc_b)


def gla_readout(o, gate, g_norm):
    return merge_heads(rmsnorm(o, g_norm)) * jax.nn.silu(gate)


def context_attention(q, k, v, sinks=None):
    b, hq, l, d = q.shape
    hkv = k.shape[1]
    g = hq // hkv
    s = jnp.einsum('bkgqd,bkld->bkgql', q.reshape(b, hkv, g, l, d), k).astype(jnp.float32) * d ** -0.5
    if sinks is not None:
        sink = jnp.broadcast_to(sinks.astype(jnp.float32).reshape(1, hkv, g, 1, 1), s.shape[:-1] + (1,))
        s = jnp.concatenate([s, sink], axis=-1)
    p = jax.nn.softmax(s, axis=-1)[..., :l].astype(v.dtype)
    return jnp.einsum('bkgql,bkld->bkgqd', p, v).reshape(b, hq, l, d)


def neighbourhood_attention(q, k, v, kc, vc, rpb):
    b, h, s_len, d = q.shape
    rows = s_len // GRID_W
    kh = min(NA_KH, rows)
    r = np.arange(rows)
    row_idx = np.clip(r - kh // 2, 0, rows - kh)[:, None] + np.arange(kh)[None, :]
    d_row = row_idx - r[:, None]
    qg, kg, vg = (a.reshape(b, h, rows, GRID_W, d) for a in (q, k, v))
    scale = d ** -0.5
    s_ctx = jnp.einsum('bhrcd,bhld->bhrcl', qg, kc).astype(jnp.float32) * scale
    outs = []
    for cb in range(GRID_W // NA_QCOLS):
        c0 = cb * NA_QCOLS
        q_cols = c0 + np.arange(NA_QCOLS)
        k0 = int(np.clip(c0 - NA_KW // 2, 0, GRID_W - NA_KCOLS))
        k_cols = k0 + np.arange(NA_KCOLS)
        win0 = np.clip(q_cols - NA_KW // 2, 0, GRID_W - NA_KW)
        valid = (k_cols[None, :] >= win0[:, None]) & (k_cols[None, :] < win0[:, None] + NA_KW)
        d_col = np.clip(k_cols[None, :] - q_cols[:, None], -(NA_KW - 1), NA_KW - 1)
        kb = kg[:, :, :, k0:k0 + NA_KCOLS][:, :, row_idx]
        vb = vg[:, :, :, k0:k0 + NA_KCOLS][:, :, row_idx]
        qb = qg[:, :, :, c0:c0 + NA_QCOLS]
        s_nb = jnp.einsum('bhrqd,bhrjkd->bhrqjk', qb, kb).astype(jnp.float32) * scale
        bias = rpb[:, (d_row + NA_KH - 1)[:, None, :, None], (d_col + NA_KW - 1)[None, :, None, :]]
        s_nb = jnp.where(valid[:, None, :], s_nb + bias.astype(jnp.float32), -jnp.inf)
        s_nb = s_nb.reshape(b, h, rows, NA_QCOLS, kh * NA_KCOLS)
        p = jax.nn.softmax(jnp.concatenate([s_nb, s_ctx[:, :, :, c0:c0 + NA_QCOLS]], axis=-1), axis=-1).astype(v.dtype)
        p_nb = p[..., :kh * NA_KCOLS].reshape(b, h, rows, NA_QCOLS, kh, NA_KCOLS)
        o = jnp.einsum('bhrqjk,bhrjkd->bhrqd', p_nb, vb) + jnp.einsum('bhrql,bhld->bhrqd', p[..., kh * NA_KCOLS:], vc)
        outs.append(o)
    return jnp.concatenate(outs, axis=3).reshape(b, h, s_len, d)


def window_attention(q, k, v, kc, vc, sinks):
    b, hq, s_len, d = q.shape
    g = hq // SW_KV_HEADS
    nb = s_len // SW_BLOCK
    l = kc.shape[2]
    qb = q.reshape(b, SW_KV_HEADS, g, nb, SW_BLOCK, d)

    def bands(a):
        ap = jnp.pad(a, ((0, 0), (0, 0), (SW_BLOCK, SW_BLOCK), (0, 0)))
        return jnp.concatenate([ap[:, :, i * SW_BLOCK:i * SW_BLOCK + s_len].reshape(b, SW_KV_HEADS, nb, SW_BLOCK, d)
                                for i in range(3)], axis=3)

    kb, vb = bands(k), bands(v)
    qpos = jnp.arange(s_len).reshape(nb, SW_BLOCK)
    kpos = (jnp.arange(nb)[:, None] - 1) * SW_BLOCK + jnp.arange(3 * SW_BLOCK)[None, :]
    mask = ((jnp.abs(qpos[:, :, None] - kpos[:, None, :]) <= SW_WINDOW)
            & (kpos[:, None, :] >= 0) & (kpos[:, None, :] < s_len))
    scale = d ** -0.5
    s_loc = jnp.einsum('bkgnqd,bknjd->bkgnqj', qb, kb).astype(jnp.float32) * scale
    s_loc = jnp.where(mask, s_loc, -jnp.inf)
    s_ctx = jnp.einsum('bkgnqd,bkld->bkgnql', qb, kc).astype(jnp.float32) * scale
    sink = jnp.broadcast_to(sinks.astype(jnp.float32).reshape(1, SW_KV_HEADS, g, 1, 1, 1), s_loc.shape[:-1] + (1,))
    p = jax.nn.softmax(jnp.concatenate([s_loc, s_ctx, sink], axis=-1), axis=-1).astype(v.dtype)
    nk = 3 * SW_BLOCK
    o = (jnp.einsum('bkgnqj,bknjd->bkgnqd', p[..., :nk], vb)
         + jnp.einsum('bkgnql,bkld->bkgnqd', p[..., nk:nk + l], vc))
    return o.reshape(b, hq, s_len, d)


def even_mixer(h_lat, h_ctx, w_in, w_out, gla_w_a2, gla_b_a2, gla_norm_g, na_q_g, na_k_g, na_rpb, with_ctx):
    cuts = [int(v) for v in np.cumsum([GLA_QK, GLA_QK, GLA_V, GLA_V, GLA_RANK, GLA_RANK, NA_W, NA_W])]

    def project(h):
        q, k, v, gate, a_f, a_b, nq, nk, nv = jnp.split(h @ w_in, cuts, axis=-1)
        log_a = tuple(split_heads(jax.nn.log_sigmoid(a @ gla_w_a2[i] + gla_b_a2[i]) / GLA_TAU, GLA_HEADS)
                      for i, a in enumerate((a_f, a_b)))
        gla = (split_heads(q, GLA_HEADS) * GLA_DK ** -0.5, split_heads(k, GLA_HEADS), split_heads(v, GLA_HEADS), log_a)
        na = (rmsnorm(split_heads(nq, NA_HEADS), na_q_g), rmsnorm(split_heads(nk, NA_HEADS), na_k_g),
              split_heads(nv, NA_HEADS))
        return gla, gate, na

    gla_l, gate_l, (nq_l, nk_l, nv_l) = project(h_lat)
    gla_c, gate_c, (nq_c, nk_c, nv_c) = project(h_ctx)
    o_gla_l, o_gla_c = bidirectional_gla(gla_l, gla_c)
    o_na_l = neighbourhood_attention(nq_l, nk_l, nv_l, nk_c, nv_c, na_rpb)
    out_l = jnp.concatenate([gla_readout(o_gla_l, gate_l, gla_norm_g), merge_heads(o_na_l)], axis=-1) @ w_out
    if not with_ctx:
        return out_l, None
    o_na_c = context_attention(nq_c, nk_c, nv_c)
    out_c = jnp.concatenate([gla_readout(o_gla_c, gate_c, gla_norm_g), merge_heads(o_na_c)], axis=-1) @ w_out
    return out_l, out_c


def odd_mixer(h_lat, h_ctx, w_in, w_out, q_g, k_g, sinks, with_ctx):
    def project(h):
        q, k, v = jnp.split(h @ w_in, [SW_Q, SW_Q + SW_KV], axis=-1)
        return (rmsnorm(split_heads(q, SW_HEADS), q_g), rmsnorm(split_heads(k, SW_KV_HEADS), k_g),
                split_heads(v, SW_KV_HEADS))

    q_l, k_l, v_l = project(h_lat)
    q_c, k_c, v_c = project(h_ctx)
    o_l = window_attention(axial_rope(q_l), axial_rope(k_l), v_l, k_c, v_c, sinks)
    out_l = merge_heads(o_l) @ w_out
    if not with_ctx:
        return out_l, None
    out_c = merge_heads(context_attention(q_c, k_c, v_c, sinks)) @ w_out
    return out_l, out_c


def moe(h, w_router, b_router, w_gate_up, b_gate_up, w_down, b_down):
    t, d = h.shape
    logits = (h @ w_router + b_router).astype(jnp.float32)
    top_vals, top_idx = lax.top_k(logits, TOP_K)
    weights = jax.nn.softmax(top_vals, axis=-1).astype(h.dtype)
    flat_e = top_idx.reshape(-1)
    flat_tok = jnp.repeat(jnp.arange(t, dtype=jnp.int32), TOP_K)
    flat_w = weights.reshape(-1)
    order = jnp.argsort(flat_e)
    e_sorted = flat_e[order]
    counts = jnp.bincount(flat_e, length=N_EXPERTS)
    padded = (counts + MOE_BLOCK - 1) // MOE_BLOCK * MOE_BLOCK
    start = jnp.cumsum(counts) - counts
    pend = jnp.cumsum(padded)
    pstart = pend - padded
    slot = pstart[e_sorted] + (jnp.arange(t * TOP_K) - start[e_sorted])
    n_blocks = (t * TOP_K + MOE_BLOCK - 1) // MOE_BLOCK + N_EXPERTS
    n_slots = n_blocks * MOE_BLOCK
    slot_tok = jnp.full((n_slots,), t, jnp.int32).at[slot].set(flat_tok[order])
    slot_w = jnp.zeros((n_slots,), h.dtype).at[slot].set(flat_w[order])
    h_pad = jnp.concatenate([h, jnp.zeros((1, d), h.dtype)], axis=0)
    xs = h_pad[slot_tok].reshape(n_blocks, MOE_BLOCK, d)
    block_e = jnp.minimum(jnp.searchsorted(pend, jnp.arange(n_blocks) * MOE_BLOCK, side='right'), N_EXPERTS - 1)

    def expert_block(args):
        xb, e = args
        gu = xb @ w_gate_up[e] + b_gate_up[e]
        glu, lin = jnp.split(gu, 2, axis=-1)
        glu = jnp.minimum(glu, SWIGLU_LIMIT)
        lin = jnp.clip(lin, -SWIGLU_LIMIT, SWIGLU_LIMIT)
        act = glu * jax.nn.sigmoid(SWIGLU_ALPHA * glu) * (lin + 1.0)
        return act @ w_down[e] + b_down[e]

    ys = lax.map(expert_block, (xs, block_e)).reshape(n_slots, d)
    out = jnp.zeros((t + 1, d), h.dtype).at[slot_tok].add(ys * slot_w[:, None])
    return out[:t]


def setup_inputs(seed: int = 0) -> dict:
    key = jax.random.key(seed)
    keys = iter(jax.random.split(key, 40))

    def nrm(shape, scale):
        return scale * jax.random.normal(next(keys), shape, jnp.float32)

    d = D_MODEL
    return {
        'x': nrm((BATCH, SEQ, d), 1.0),
        'c': nrm((BATCH, d), 1.0),
        'ctx': nrm((BATCH, CTX_LEN, d), 1.0),
        'c_ctx': nrm((d,), 1.0),
        'mod_w': nrm((DEPTH, d, 6 * d), 0.5 * d ** -0.5),
        'mod_b': nrm((DEPTH, 6 * d), 0.02),
        'norm_mix_g': 1.0 + nrm((DEPTH, d), 0.05),
        'norm_ffn_g': 1.0 + nrm((DEPTH, d), 0.05),
        'ev_w_in': nrm((N_EVEN, d, EVEN_IN), d ** -0.5),
        'ev_w_out': nrm((N_EVEN, EVEN_MIX, d), EVEN_MIX ** -0.5),
        'ev_gla_w_a2': nrm((N_EVEN, 2, GLA_RANK, GLA_QK), GLA_RANK ** -0.5),
        'ev_gla_b_a2': nrm((N_EVEN, 2, GLA_QK), 0.5),
        'ev_gla_norm_g': 1.0 + nrm((N_EVEN, GLA_DV), 0.05),
        'ev_na_q_g': 1.0 + nrm((N_EVEN, HEAD_DIM), 0.05),
        'ev_na_k_g': 1.0 + nrm((N_EVEN, HEAD_DIM), 0.05),
        'ev_na_rpb': nrm((N_EVEN, NA_HEADS, 2 * NA_KH - 1, 2 * NA_KW - 1), 0.1),
        'od_w_in': nrm((N_ODD, d, ODD_IN), d ** -0.5),
        'od_w_out': nrm((N_ODD, ODD_MIX, d), ODD_MIX ** -0.5),
        'od_q_g': 1.0 + nrm((N_ODD, HEAD_DIM), 0.05),
        'od_k_g': 1.0 + nrm((N_ODD, HEAD_DIM), 0.05),
        'od_sinks': nrm((N_ODD, SW_HEADS), 0.5),
        'moe_w_router': nrm((DEPTH, d, N_EXPERTS), d ** -0.5),
        'moe_b_router': nrm((DEPTH, N_EXPERTS), 0.01),
        'moe_w_gate_up': nrm((DEPTH, N_EXPERTS, d, 2 * D_EXPERT), d ** -0.5),
        'moe_b_gate_up': nrm((DEPTH, N_EXPERTS, 2 * D_EXPERT), 0.02),
        'moe_w_down': nrm((DEPTH, N_EXPERTS, D_EXPERT, d), D_EXPERT ** -0.5),
        'moe_b_down': nrm((DEPTH, N_EXPERTS, d), 0.02),
    }


def reference(x, c, ctx, c_ctx, mod_w, mod_b, norm_mix_g, norm_ffn_g,
              ev_w_in, ev_w_out, ev_gla_w_a2, ev_gla_b_a2, ev_gla_norm_g, ev_na_q_g, ev_na_k_g, ev_na_rpb,
              od_w_in, od_w_out, od_q_g, od_k_g, od_sinks,
              moe_w_router, moe_b_router, moe_w_gate_up, moe_b_gate_up, moe_w_down, moe_b_down):
    b, s_len, d = x.shape
    l = ctx.shape[1]
    silu_c = jax.nn.silu(c)
    silu_cc = jax.nn.silu(c_ctx)
    for layer in range(DEPTH):
        with_ctx = layer < DEPTH - 1
        idx = layer // 2
        m_lat = (silu_c @ mod_w[layer] + mod_b[layer])[:, None, :]
        m_ctx = silu_cc @ mod_w[layer] + mod_b[layer]
        sh1, sc1, g1, sh2, sc2, g2 = jnp.split(m_lat, 6, axis=-1)
        csh1, csc1, cg1, csh2, csc2, cg2 = jnp.split(m_ctx, 6, axis=-1)
        h_lat = rmsnorm(x, norm_mix_g[layer]) * (1.0 + sc1) + sh1
        h_ctx = rmsnorm(ctx, norm_mix_g[layer]) * (1.0 + csc1) + csh1
        if layer % 2 == 0:
            o_lat, o_ctx = even_mixer(h_lat, h_ctx, ev_w_in[idx], ev_w_out[idx], ev_gla_w_a2[idx], ev_gla_b_a2[idx],
                                      ev_gla_norm_g[idx], ev_na_q_g[idx], ev_na_k_g[idx], ev_na_rpb[idx], with_ctx)
        else:
            o_lat, o_ctx = odd_mixer(h_lat, h_ctx, od_w_in[idx], od_w_out[idx], od_q_g[idx], od_k_g[idx],
                                     od_sinks[idx], with_ctx)
        x = x + g1 * o_lat
        f_lat = (rmsnorm(x, norm_ffn_g[layer]) * (1.0 + sc2) + sh2).reshape(b * s_len, d)
        moe_args = (moe_w_router[layer], moe_b_router[layer], moe_w_gate_up[layer], moe_b_gate_up[layer],
                    moe_w_down[layer], moe_b_down[layer])
        if with_ctx:
            ctx = ctx + cg1 * o_ctx
            f_ctx = (rmsnorm(ctx, norm_ffn_g[layer]) * (1.0 + csc2) + csh2).reshape(b * l, d)
            y = moe(jnp.concatenate([f_lat, f_ctx], axis=0), *moe_args)
            x = x + g2 * y[:b * s_len].reshape(b, s_len, d)
            ctx = ctx + cg2 * y[b * s_len:].reshape(b, l, d)
        else:
            x = x + g2 * moe(f_lat, *moe_args).reshape(b, s_len, d)
    return x
```

```python
import functools

import jax
import jax.numpy as jnp
import numpy as np
from jax import lax
from jax.experimental import pallas as pl
from jax.experimental.pallas import tpu as pltpu

F32 = jnp.float32
BF16 = jnp.bfloat16
I32 = jnp.int32
HIGHEST = lax.Precision.HIGHEST

GRID_W = 64
HEAD_DIM = 64
EPS = 1e-6
GLA_HEADS = 4
GLA_DK = 64
GLA_DV = 128
GLA_RANK = 16
GLA_TAU = 16.0
GLA_CHUNK = 64
GLA_SUB = 16
NA_HEADS = 8
NA_KH = 8
NA_KW = 16
SW_HEADS = 16
SW_KV_HEADS = 2
SW_WINDOW = 128
SW_BLOCK = 128
ROPE_BASE = 10000.0
N_EXPERTS = 32
TOP_K = 4
SWIGLU_LIMIT = 7.0
SWIGLU_ALPHA = 1.702

GLA_QK = GLA_HEADS * GLA_DK
GLA_V = GLA_HEADS * GLA_DV
NA_W = NA_HEADS * HEAD_DIM
SW_Q = SW_HEADS * HEAD_DIM
SW_KV = SW_KV_HEADS * HEAD_DIM

LANES = 128
VMEM_LIMIT_BYTES = 56 * 1024 * 1024
NEG = -1e30

TOKEN_TILE = 256
EXPERT_BLOCK = 512
MOD_ROWS = 16


def _dot(a, b, **kw):
    return jnp.dot(a, b, preferred_element_type=F32, **kw)


def _dot_nt(a, b):
    return lax.dot_general(a, b, (((1,), (1,)), ((), ())), preferred_element_type=F32)


def _dot_tn(a, b):
    return lax.dot_general(a, b, (((0,), (0,)), ((), ())), preferred_element_type=F32)


def _split_bf16(x):
    hi = x.astype(BF16)
    lo = (x - hi.astype(F32)).astype(BF16)
    return hi, lo


def _params(sem, vmem=VMEM_LIMIT_BYTES):
    return pltpu.CompilerParams(dimension_semantics=sem, vmem_limit_bytes=vmem)


def _mod_kernel(cv_ref, w_ref, b_ref, o_ref):
    cv = cv_ref[...]
    s = cv * jax.nn.sigmoid(cv)
    o_ref[...] = _dot(s, w_ref[...], precision=HIGHEST) + b_ref[...]


def _modulation(cv, mod_w, mod_b):
    depth, d, n = mod_w.shape
    tn = 1536
    return pl.pallas_call(
        _mod_kernel,
        grid=(depth, n // tn),
        in_specs=[
            pl.BlockSpec((MOD_ROWS, d), lambda l, j: (0, 0)),
            pl.BlockSpec((None, d, tn), lambda l, j: (l, 0, j)),
            pl.BlockSpec((None, 1, tn), lambda l, j: (l, 0, j)),
        ],
        out_specs=pl.BlockSpec((None, MOD_ROWS, tn), lambda l, j: (l, 0, j)),
        out_shape=jax.ShapeDtypeStruct((depth, MOD_ROWS, n), F32),
        compiler_params=_params(("arbitrary", "arbitrary")),
        name="modulation",
    )(cv, mod_w, mod_b.reshape(depth, 1, n))


def _norm_mod(x, g_row, m_ref, shift_row, scale_row):
    ms = jnp.mean(x * x, axis=-1, keepdims=True)
    h = x * lax.rsqrt(ms + EPS) * g_row
    return h * (1.0 + m_ref[scale_row:scale_row + 1, :]) + m_ref[shift_row:shift_row + 1, :]


def _head_pair_ones():
    r = lax.broadcasted_iota(I32, (LANES, LANES), 0) // HEAD_DIM
    c = lax.broadcasted_iota(I32, (LANES, LANES), 1) // HEAD_DIM
    return jnp.where(r == c, 1.0, 0.0).astype(BF16)


def _head_rms(y, g_row, ones_blk):
    outs = []
    for j in range(y.shape[1] // LANES):
        s = y[:, j * LANES:(j + 1) * LANES]
        hi, lo = _split_bf16(s * s)
        ms = (_dot(hi, ones_blk) + _dot(lo, ones_blk)) * (1.0 / HEAD_DIM)
        outs.append(s * lax.rsqrt(ms + EPS))
    return jnp.concatenate(outs, axis=1) * g_row


def _mod_row_map(n_ctx_tiles, n_batch):
    return lambda b, t: (jnp.where(t < n_ctx_tiles, n_batch, b), 0, 0)


def _proj_even_kernel(x_ref, m_ref, gn_ref, w_ref, a2_ref, ba2_ref, qg_ref, kg_ref,
                      gq_ref, gk_ref, gv_ref, gate_ref, la_ref, nq_ref, nk_ref, nv_ref):
    h = _norm_mod(x_ref[...], gn_ref[...], m_ref, 0, 1)
    y = _dot(h.astype(BF16), w_ref[...])
    c0 = 0
    gq_ref[...] = y[:, c0:c0 + GLA_QK] * (GLA_DK ** -0.5)
    c0 += GLA_QK
    gk_ref[...] = y[:, c0:c0 + GLA_QK]
    c0 += GLA_QK
    gv_ref[...] = y[:, c0:c0 + GLA_V].astype(BF16)
    c0 += GLA_V
    gate_ref[...] = y[:, c0:c0 + GLA_V].astype(BF16)
    c0 += GLA_V
    ones_blk = _head_pair_ones()
    nq_ref[...] = (_head_rms(y[:, c0:c0 + NA_W], qg_ref[...], ones_blk) * (HEAD_DIM ** -0.5)).astype(BF16)
    c0 += NA_W
    nk_ref[...] = _head_rms(y[:, c0:c0 + NA_W], kg_ref[...], ones_blk).astype(BF16)
    c0 += NA_W
    nv_ref[...] = y[:, c0:c0 + NA_W].astype(BF16)
    c0 += NA_W
    a = y[:, c0:c0 + LANES]
    z = _dot(a, a2_ref[...], precision=HIGHEST) + ba2_ref[...]
    la_ref[...] = (jnp.minimum(z, 0.0) - jnp.log1p(jnp.exp(-jnp.abs(z)))) * (1.0 / GLA_TAU)


def _proj_even(xa, mods, gn, w_bf, a2, ba2, qg, kg, n_ctx_tiles):
    b, tt, d = xa.shape
    tm = TOKEN_TILE
    n = w_bf.shape[1]
    tok = lambda w: pl.BlockSpec((None, tm, w), lambda bi, t: (bi, t, 0))
    full = lambda shape: pl.BlockSpec(shape, lambda bi, t: (0,) * len(shape))
    widths = [(GLA_QK, F32), (GLA_QK, F32), (GLA_V, BF16), (GLA_V, BF16), (2 * GLA_QK, F32),
              (NA_W, BF16), (NA_W, BF16), (NA_W, BF16)]
    return pl.pallas_call(
        _proj_even_kernel,
        grid=(b, tt // tm),
        in_specs=[
            tok(d),
            pl.BlockSpec((None, 6, d), _mod_row_map(n_ctx_tiles, b)),
            full((1, d)), full((d, n)), full((LANES, 2 * GLA_QK)), full((1, 2 * GLA_QK)),
            full((1, NA_W)), full((1, NA_W)),
        ],
        out_specs=[tok(w) for w, _ in widths],
        out_shape=[jax.ShapeDtypeStruct((b, tt, w), dt) for w, dt in widths],
        compiler_params=_params(("parallel", "parallel")),
        name="proj_even",
    )(xa, mods, gn, w_bf, a2, ba2, qg, kg)


def _rope(y, cos, sin_signed, lo_mask):
    w = y.shape[1]
    reps = w // LANES
    cos_t = jnp.concatenate([cos] * reps, axis=1) if reps > 1 else cos
    sin_t = jnp.concatenate([sin_signed] * reps, axis=1) if reps > 1 else sin_signed
    msk = jnp.concatenate([lo_mask] * reps, axis=1) if reps > 1 else lo_mask
    quarter = HEAD_DIM // 4
    swapped = jnp.where(msk, pltpu.roll(y, w - quarter, 1), pltpu.roll(y, quarter, 1))
    return y * cos_t + swapped * sin_t


def _proj_odd_kernel(x_ref, m_ref, gn_ref, w_ref, qg_ref, kg_ref, cos_ref, sin_ref,
                     q_ref, k_ref, v_ref):
    h = _norm_mod(x_ref[...], gn_ref[...], m_ref, 0, 1)
    y = _dot(h.astype(BF16), w_ref[...])
    ones_blk = _head_pair_ones()
    lane = lax.broadcasted_iota(I32, (1, LANES), 1)
    lo_mask = (lane % (HEAD_DIM // 2)) < (HEAD_DIM // 4)
    cos = cos_ref[...]
    sin = sin_ref[...]
    q = _head_rms(y[:, 0:SW_Q], qg_ref[...], ones_blk)
    q_ref[...] = (_rope(q, cos, sin, lo_mask) * (HEAD_DIM ** -0.5)).astype(BF16)
    k = _head_rms(y[:, SW_Q:SW_Q + SW_KV], kg_ref[...], ones_blk)
    k_ref[...] = _rope(k, cos, sin, lo_mask).astype(BF16)
    v_ref[...] = y[:, SW_Q + SW_KV:SW_Q + 2 * SW_KV].astype(BF16)


def _proj_odd(xa, mods, gn, w_bf, qg, kg, cos, sin, n_ctx_tiles):
    b, tt, d = xa.shape
    tm = TOKEN_TILE
    n = w_bf.shape[1]
    tok = lambda w: pl.BlockSpec((None, tm, w), lambda bi, t: (bi, t, 0))
    full = lambda shape: pl.BlockSpec(shape, lambda bi, t: (0,) * len(shape))
    pos = pl.BlockSpec((tm, LANES), lambda bi, t: (t, 0))
    widths = [SW_Q, SW_KV, SW_KV]
    return pl.pallas_call(
        _proj_odd_kernel,
        grid=(b, tt // tm),
        in_specs=[
            tok(d),
            pl.BlockSpec((None, 6, d), _mod_row_map(n_ctx_tiles, b)),
            full((1, d)), full((d, n)), full((1, SW_Q)), full((1, SW_KV)), pos, pos,
        ],
        out_specs=[tok(w) for w in widths],
        out_shape=[jax.ShapeDtypeStruct((b, tt, w), BF16) for w in widths],
        compiler_params=_params(("parallel", "parallel")),
        name="proj_odd",
    )(xa, mods, gn, w_bf, qg, kg, cos, sin)


def _gla_kernel(q_ref, k_ref, v_ref, g_ref, o_ref, st_ref):
    c = GLA_CHUNK
    direction = pl.program_id(0)

    @pl.when(pl.program_id(2) == 0)
    def _():
        st_ref[...] = jnp.zeros_like(st_ref)

    sgn = jnp.where(direction == 0, 1, -1)
    r = lax.broadcasted_iota(I32, (2 * c, c), 0)
    j = lax.broadcasted_iota(I32, (2 * c, c), 1)
    i = r % c
    incl = jnp.where(sgn * (i - j) >= 0, 1.0, 0.0)
    blk = jnp.where(sgn * (i // GLA_SUB - j // GLA_SUB) > 0, 1.0, 0.0)
    scan_mat = jnp.where(r >= c, blk, incl).astype(BF16)
    causal = scan_mat[0:c, :].astype(F32)

    g_hi, g_lo = _split_bf16(g_ref[...])
    cr = _dot(scan_mat, g_hi) + _dot(scan_mat, g_lo)
    cum = cr[0:c, :]
    ref_pt = cr[c:2 * c, :]
    tot = jnp.where(direction == 0, cum[c - 1:c, :], cum[0:1, :])

    q = q_ref[...]
    k = k_ref[...]
    q_state = (q * jnp.exp(cum)).astype(BF16)
    q_sub = (q * jnp.exp(cum - ref_pt)).astype(BF16)
    k_state = (k * jnp.exp(tot - cum)).astype(BF16)
    decay = jnp.exp(tot)
    row = lax.broadcasted_iota(I32, (c, 1), 0)

    attn_rows = [[] for _ in range(GLA_HEADS)]
    for a in range(c // GLA_SUB):
        lo = a * GLA_SUB
        edge = jnp.where(direction == 0, lo + GLA_SUB - 1, lo)
        reachable = sgn * (edge - row) >= 0
        ra = ref_pt[lo:lo + 1, :]
        k_sub = (k * jnp.exp(jnp.where(reachable, ra - cum, 0.0))).astype(BF16)
        for h in range(GLA_HEADS):
            hs = slice(h * GLA_DK, (h + 1) * GLA_DK)
            attn_rows[h].append(_dot_nt(q_sub[lo:lo + GLA_SUB, hs], k_sub[:, hs]))

    v = v_ref[...]
    for h in range(GLA_HEADS):
        hs = slice(h * GLA_DK, (h + 1) * GLA_DK)
        vs = slice(h * GLA_DV, (h + 1) * GLA_DV)
        attn = (jnp.concatenate(attn_rows[h], axis=0) * causal).astype(BF16)
        st = st_ref[h]
        o = _dot(attn, v[:, vs]) + _dot_nt(q_state[:, hs], st.astype(BF16))
        o_ref[:, vs] = o
        st_ref[h] = st * decay[:, hs] + _dot_tn(v[:, vs], k_state[:, hs])


def _gla(gq, gk, gv, la, n_ctx_chunks):
    b, tt, _ = gq.shape
    c = GLA_CHUNK
    nc = tt // c

    def chunk(d, s):
        bwd = jnp.where(s < n_ctx_chunks, n_ctx_chunks - 1 - s, nc - 1 + n_ctx_chunks - s)
        return jnp.where(d == 0, s, bwd)

    return pl.pallas_call(
        _gla_kernel,
        grid=(2, b, nc),
        in_specs=[
            pl.BlockSpec((None, c, GLA_QK), lambda d, bi, s: (bi, chunk(d, s), 0)),
            pl.BlockSpec((None, c, GLA_QK), lambda d, bi, s: (bi, chunk(d, s), 0)),
            pl.BlockSpec((None, c, GLA_V), lambda d, bi, s: (bi, chunk(d, s), 0)),
            pl.BlockSpec((None, c, GLA_QK), lambda d, bi, s: (bi, chunk(d, s), d)),
        ],
        out_specs=pl.BlockSpec((None, None, c, GLA_V), lambda d, bi, s: (d, bi, chunk(d, s), 0)),
        out_shape=jax.ShapeDtypeStruct((2, b, tt, GLA_V), F32),
        scratch_shapes=[pltpu.VMEM((GLA_HEADS, GLA_DV, GLA_DK), F32)],
        compiler_params=_params(("arbitrary", "arbitrary", "arbitrary")),
        name="gla_scan",
    )(gq, gk, gv, la)


def _na_bias_table(rpb, kh):
    cls = np.arange(kh)[:, None, None, None]
    qc = np.arange(GRID_W)[None, :, None, None]
    kr = np.arange(kh)[None, None, :, None]
    kc = np.arange(GRID_W)[None, None, None, :]
    win0 = np.clip(qc - NA_KW // 2, 0, GRID_W - NA_KW)
    valid = (kc >= win0) & (kc < win0 + NA_KW)
    ridx = np.broadcast_to(kr - cls + NA_KH - 1, (kh, GRID_W, kh, GRID_W))
    cidx = np.broadcast_to(np.clip(kc - qc, -(NA_KW - 1), NA_KW - 1) + NA_KW - 1, (kh, GRID_W, kh, GRID_W))
    valid = np.broadcast_to(valid, (kh, GRID_W, kh, GRID_W))
    tab = jnp.where(valid[None], rpb[:, ridx, cidx], NEG)
    return tab.reshape(rpb.shape[0], kh, GRID_W, kh * GRID_W).astype(F32)


def _na_kernel(q_ref, k_ref, v_ref, bias_ref, o_ref, *, n_ctx, n_ctx_blk, rows_per_blk, n_rows, kh):
    t = pl.program_id(2)
    lane = lax.broadcasted_iota(I32, (1, LANES), 1)
    first = lane < HEAD_DIM
    kc = k_ref[0:n_ctx, :]
    vc = v_ref[0:n_ctx, :]

    def head_q(q, hh):
        keep = first if hh == 0 else jnp.logical_not(first)
        return jnp.where(keep, q, jnp.zeros_like(q))

    @pl.when(t < n_ctx_blk)
    def _():
        q = q_ref[...]
        outs = []
        for hh in range(2):
            s = _dot_nt(head_q(q, hh), kc)
            m = jnp.max(s, axis=-1, keepdims=True)
            p = jnp.exp(s - m)
            den = jnp.sum(p, axis=-1, keepdims=True)
            outs.append(_dot(p.astype(BF16), vc) / den)
        o_ref[...] = jnp.where(first, outs[0], outs[1]).astype(o_ref.dtype)

    @pl.when(t >= n_ctx_blk)
    def _():
        for i in range(rows_per_blk):
            r = (t - n_ctx_blk) * rows_per_blk + i
            row0 = jnp.clip(r - kh // 2, 0, n_rows - kh)
            cls = r - row0
            start = pl.multiple_of(n_ctx + row0 * GRID_W, GRID_W)
            kb = k_ref[pl.ds(start, kh * GRID_W), :]
            vb = v_ref[pl.ds(start, kh * GRID_W), :]
            q = q_ref[i * GRID_W:(i + 1) * GRID_W, :]
            outs = []
            for hh in range(2):
                qh = head_q(q, hh)
                s = _dot_nt(qh, kb) + bias_ref[hh, cls]
                sc = _dot_nt(qh, kc)
                m = jnp.maximum(jnp.max(s, axis=-1, keepdims=True), jnp.max(sc, axis=-1, keepdims=True))
                p = jnp.exp(s - m)
                pc = jnp.exp(sc - m)
                den = jnp.sum(p, axis=-1, keepdims=True) + jnp.sum(pc, axis=-1, keepdims=True)
                outs.append((_dot(p.astype(BF16), vb) + _dot(pc.astype(BF16), vc)) / den)
            o_ref[i * GRID_W:(i + 1) * GRID_W, :] = jnp.where(first, outs[0], outs[1]).astype(o_ref.dtype)


def _na(nq, nk, nv, bias_tab, n_ctx):
    b, tt, _ = nq.shape
    tq = TOKEN_TILE
    n_rows = (tt - n_ctx) // GRID_W
    kh = bias_tab.shape[1]
    kern = functools.partial(_na_kernel, n_ctx=n_ctx, n_ctx_blk=n_ctx // tq, rows_per_blk=tq // GRID_W,
                             n_rows=n_rows, kh=kh)
    return pl.pallas_call(
        kern,
        grid=(b, NA_HEADS // 2, tt // tq),
        in_specs=[
            pl.BlockSpec((None, tq, LANES), lambda bi, p, t: (bi, t, p)),
            pl.BlockSpec((None, tt, LANES), lambda bi, p, t: (bi, 0, p)),
            pl.BlockSpec((None, tt, LANES), lambda bi, p, t: (bi, 0, p)),
            pl.BlockSpec((2, kh, GRID_W, kh * GRID_W), lambda bi, p, t: (p, 0, 0, 0)),
        ],
        out_specs=pl.BlockSpec((None, tq, LANES), lambda bi, p, t: (bi, t, p)),
        out_shape=jax.ShapeDtypeStruct((b, tt, NA_W), BF16),
        compiler_params=_params(("parallel", "parallel", "arbitrary")),
        name="na_attn",
    )(nq, nk, nv, bias_tab)


def _swa_kernel(sink_ref, q_ref, k_ref, v_ref, o_ref, *, n_ctx, n_lat_blk):
    blk = SW_BLOCK
    n = pl.program_id(1)
    n_ctx_blk = n_ctx // blk
    jb = n - n_ctx_blk
    sb = jnp.clip(jb - 1, 0, n_lat_blk - 3)
    start = pl.multiple_of(n_ctx + sb * blk, blk)
    kl = k_ref[pl.ds(start, 3 * blk), :]
    vl = v_ref[pl.ds(start, 3 * blk), :]
    kc = k_ref[0:n_ctx, :]
    vc = v_ref[0:n_ctx, :]
    iq = lax.broadcasted_iota(I32, (blk, 3 * blk), 0)
    ik = lax.broadcasted_iota(I32, (blk, 3 * blk), 1)
    offset = jnp.where(n < n_ctx_blk, 4 * SW_WINDOW, (jb - sb) * blk)
    visible = jnp.abs(iq - ik + offset) <= SW_WINDOW
    q = q_ref[...]
    group = SW_HEADS // SW_KV_HEADS
    outs = []
    for h in range(SW_HEADS):
        g = h // group
        gs = slice(g * HEAD_DIM, (g + 1) * HEAD_DIM)
        qh = q[:, h * HEAD_DIM:(h + 1) * HEAD_DIM]
        s = jnp.where(visible, _dot_nt(qh, kl[:, gs]), NEG)
        sc = _dot_nt(qh, kc[:, gs])
        sink = sink_ref[h]
        m = jnp.maximum(jnp.maximum(jnp.max(s, axis=-1, keepdims=True), jnp.max(sc, axis=-1, keepdims=True)), sink)
        p = jnp.exp(s - m)
        pc = jnp.exp(sc - m)
        den = jnp.sum(p, axis=-1, keepdims=True) + jnp.sum(pc, axis=-1, keepdims=True) + jnp.exp(sink - m)
        outs.append((_dot(p.astype(BF16), vl[:, gs]) + _dot(pc.astype(BF16), vc[:, gs])) / den)
    for pr in range(SW_HEADS // 2):
        o_ref[:, pr * LANES:(pr + 1) * LANES] = jnp.concatenate(outs[2 * pr:2 * pr + 2], axis=1).astype(o_ref.dtype)


def _swa(q, k, v, sinks, n_ctx):
    b, tt, _ = q.shape
    blk = SW_BLOCK
    kern = functools.partial(_swa_kernel, n_ctx=n_ctx, n_lat_blk=(tt - n_ctx) // blk)
    return pl.pallas_call(
        kern,
        grid=(b, tt // blk),
        in_specs=[
            pl.BlockSpec(memory_space=pltpu.SMEM),
            pl.BlockSpec((None, blk, SW_Q), lambda bi, n: (bi, n, 0)),
            pl.BlockSpec((None, tt, SW_KV), lambda bi, n: (bi, 0, 0)),
            pl.BlockSpec((None, tt, SW_KV), lambda bi, n: (bi, 0, 0)),
        ],
        out_specs=pl.BlockSpec((None, blk, SW_Q), lambda bi, n: (bi, n, 0)),
        out_shape=jax.ShapeDtypeStruct((b, tt, SW_Q), BF16),
        compiler_params=_params(("parallel", "arbitrary")),
        name="swa_attn",
    )(sinks, q, k, v)


def _route(f, wr_ref, br_ref, run_ref, idx_ref, wt_ref, rank_ref, cnt_ref):
    tm = f.shape[0]
    logits = _dot(f, wr_ref[...], precision=HIGHEST) + br_ref[...]
    e_iota = lax.broadcasted_iota(I32, (tm, N_EXPERTS), 1).astype(F32)
    lane8 = lax.broadcasted_iota(I32, (tm, 8), 1)
    vals, hots = [], []
    idx_out = jnp.zeros((tm, 8), I32)
    for k in range(TOP_K):
        mx = jnp.max(logits, axis=-1, keepdims=True)
        am = jnp.min(jnp.where(logits == mx, e_iota, float(N_EXPERTS)), axis=-1, keepdims=True)
        hot = e_iota == am
        vals.append(mx)
        hots.append(hot)
        idx_out = jnp.where(lane8 == k, am.astype(I32), idx_out)
        logits = jnp.where(hot, -jnp.inf, logits)
    exps = [jnp.exp(v - vals[0]) for v in vals]
    den = exps[0] + exps[1] + exps[2] + exps[3]
    multihot = jnp.zeros((tm, N_EXPERTS), F32)
    wt_out = jnp.zeros((tm, 8), F32)
    for k in range(TOP_K):
        multihot = multihot + jnp.where(hots[k], 1.0, 0.0)
        wt_out = jnp.where(lane8 == k, exps[k] / den, wt_out)
    ri = lax.broadcasted_iota(I32, (tm, tm), 0)
    ci = lax.broadcasted_iota(I32, (tm, tm), 1)
    before = jnp.where(ci < ri, 1.0, 0.0).astype(BF16)
    base = _dot(before, multihot.astype(BF16)) + run_ref[...]
    rank_out = jnp.zeros((tm, 8), I32)
    for k in range(TOP_K):
        rk = jnp.sum(jnp.where(hots[k], base, 0.0), axis=-1, keepdims=True)
        rank_out = jnp.where(lane8 == k, rk.astype(I32), rank_out)
    run_ref[...] = run_ref[...] + jnp.sum(multihot, axis=0, keepdims=True)
    idx_ref[...] = idx_out
    wt_ref[...] = wt_out
    rank_ref[...] = rank_out
    cnt_ref[...] = run_ref[...]


def _post_common(mix, x_ref, m_ref, fg_ref, wr_ref, br_ref, xn_ref, f_ref, idx_ref, wt_ref, rank_ref, cnt_ref,
                 run_ref):
    @pl.when((pl.program_id(0) == 0) & (pl.program_id(1) == 0))
    def _():
        run_ref[...] = jnp.zeros_like(run_ref)

    xn = x_ref[...] + m_ref[2:3, :] * mix
    xn_ref[...] = xn
    f = _norm_mod(xn, fg_ref[...], m_ref, 3, 4)
    f_ref[...] = f
    _route(f, wr_ref, br_ref, run_ref, idx_ref, wt_ref, rank_ref, cnt_ref)


def _post_even_kernel(of_ref, ob_ref, gate_ref, na_ref, gng_ref, w_ref, x_ref, m_ref, fg_ref, wr_ref, br_ref,
                      xn_ref, f_ref, idx_ref, wt_ref, rank_ref, cnt_ref, run_ref):
    o = of_ref[...] + ob_ref[...]
    gate = gate_ref[...].astype(F32)
    gn = gng_ref[...]
    parts = []
    for h in range(GLA_HEADS):
        oh = o[:, h * GLA_DV:(h + 1) * GLA_DV]
        ms = jnp.mean(oh * oh, axis=-1, keepdims=True)
        parts.append(oh * lax.rsqrt(ms + EPS) * gn)
    gla = jnp.concatenate(parts, axis=1) * (gate * jax.nn.sigmoid(gate))
    mix = _dot(gla.astype(BF16), w_ref[0:GLA_V, :]) + _dot(na_ref[...], w_ref[GLA_V:GLA_V + NA_W, :])
    _post_common(mix, x_ref, m_ref, fg_ref, wr_ref, br_ref, xn_ref, f_ref, idx_ref, wt_ref, rank_ref, cnt_ref,
                 run_ref)


def _post_odd_kernel(o_ref, w_ref, x_ref, m_ref, fg_ref, wr_ref, br_ref,
                     xn_ref, f_ref, idx_ref, wt_ref, rank_ref, cnt_ref, run_ref):
    mix = _dot(o_ref[...], w_ref[...])
    _post_common(mix, x_ref, m_ref, fg_ref, wr_ref, br_ref, xn_ref, f_ref, idx_ref, wt_ref, rank_ref, cnt_ref,
                 run_ref)


def _post(even, mixer_outs, extra, w_bf, xa, mods, fg, wr, br, n_ctx_tiles):
    b, tt, d = xa.shape
    tm = TOKEN_TILE
    tok = lambda w: pl.BlockSpec((None, tm, w), lambda bi, t: (bi, t, 0))
    full = lambda shape: pl.BlockSpec(shape, lambda bi, t: (0,) * len(shape))
    if even:
        o2, gate, na = mixer_outs
        lead_specs = [
            pl.BlockSpec((None, None, tm, GLA_V), lambda bi, t: (0, bi, t, 0)),
            pl.BlockSpec((None, None, tm, GLA_V), lambda bi, t: (1, bi, t, 0)),
            tok(GLA_V), tok(NA_W), full((1, GLA_DV)),
        ]
        lead_args = [o2, o2, gate, na, extra]
        kern = _post_even_kernel
    else:
        (o,) = mixer_outs
        lead_specs = [tok(SW_Q)]
        lead_args = [o]
        kern = _post_odd_kernel
    small = lambda: pl.BlockSpec((None, tm, 8), lambda bi, t: (bi, t, 0))
    return pl.pallas_call(
        kern,
        grid=(b, tt // tm),
        in_specs=lead_specs + [
            full(w_bf.shape), tok(d),
            pl.BlockSpec((None, 6, d), _mod_row_map(n_ctx_tiles, b)),
            full((1, d)), full((d, N_EXPERTS)), full((1, N_EXPERTS)),
        ],
        out_specs=[tok(d), tok(d), small(), small(), small(), full((1, N_EXPERTS))],
        out_shape=[
            jax.ShapeDtypeStruct((b, tt, d), F32), jax.ShapeDtypeStruct((b, tt, d), F32),
            jax.ShapeDtypeStruct((b, tt, 8), I32), jax.ShapeDtypeStruct((b, tt, 8), F32),
            jax.ShapeDtypeStruct((b, tt, 8), I32), jax.ShapeDtypeStruct((1, N_EXPERTS), F32),
        ],
        scratch_shapes=[pltpu.VMEM((1, N_EXPERTS), F32)],
        compiler_params=_params(("arbitrary", "arbitrary")),
        name="post_even" if even else "post_odd",
    )(*lead_args, w_bf, xa, mods, fg, wr, br)


def _dispatch_kernel(slot_ref, f_ref, xs_in_ref, xs_ref, sem):
    del xs_in_ref
    tm = f_ref.shape[0]

    def issue(i, carry):
        for k in range(TOP_K):
            s = slot_ref[i * TOP_K + k]
            pltpu.make_async_copy(f_ref.at[pl.ds(i, 1)], xs_ref.at[pl.ds(s, 1)], sem).start()
        return carry

    lax.fori_loop(0, tm, issue, 0)
    for _ in range(TOP_K):
        pltpu.make_async_copy(f_ref, xs_ref.at[pl.ds(0, tm)], sem).wait()


def _dispatch(slots, f2, n_slots):
    t, d = f2.shape
    tm = TOKEN_TILE
    xs0 = jnp.zeros((n_slots, d), F32)
    return pl.pallas_call(
        _dispatch_kernel,
        grid=(t // tm,),
        in_specs=[
            pl.BlockSpec((tm * TOP_K,), lambda i: (i,), memory_space=pltpu.SMEM),
            pl.BlockSpec((tm, d), lambda i: (i, 0)),
            pl.BlockSpec(memory_space=pl.ANY),
        ],
        out_specs=pl.BlockSpec(memory_space=pl.ANY),
        out_shape=jax.ShapeDtypeStruct((n_slots, d), F32),
        scratch_shapes=[pltpu.SemaphoreType.DMA(())],
        input_output_aliases={2: 0},
        compiler_params=_params(("arbitrary",)),
        name="moe_dispatch",
    )(slots, f2, xs0)


def _expert_kernel(be_ref, nu_ref, xs_ref, wgu_ref, bgu_ref, wd_ref, bd_ref, ys_ref, wgu_bf, wd_bf):
    i = pl.program_id(0)
    prev = be_ref[jnp.maximum(i - 1, 0)]
    fresh = (i == 0) | (be_ref[i] != prev)

    @pl.when(fresh & (i < nu_ref[0]))
    def _():
        wgu_bf[...] = wgu_ref[...].astype(BF16)
        wd_bf[...] = wd_ref[...].astype(BF16)

    @pl.when(i < nu_ref[0])
    def _():
        x = xs_ref[...].astype(BF16)
        de = wd_bf.shape[0]
        half = de // 2
        acc = None
        for c in range(2):
            glu = _dot(x, wgu_bf[:, c * half:(c + 1) * half]) + bgu_ref[:, c * half:(c + 1) * half]
            lin = _dot(x, wgu_bf[:, de + c * half:de + (c + 1) * half]) + bgu_ref[:, de + c * half:de + (c + 1) * half]
            glu = jnp.minimum(glu, SWIGLU_LIMIT)
            lin = jnp.clip(lin, -SWIGLU_LIMIT, SWIGLU_LIMIT)
            act = (glu * jax.nn.sigmoid(SWIGLU_ALPHA * glu) * (lin + 1.0)).astype(BF16)
            part = _dot(act, wd_bf[c * half:(c + 1) * half, :])
            acc = part if acc is None else acc + part
        ys_ref[...] = acc + bd_ref[...]

    @pl.when(i >= nu_ref[0])
    def _():
        ys_ref[...] = jnp.zeros_like(ys_ref)


def _experts(block_e, n_used, xs, wgu, bgu, wd, bd):
    n_slots, d = xs.shape
    bm = EXPERT_BLOCK
    ne, _, de2 = wgu.shape
    de = wd.shape[1]
    row_blk = lambda i, be, nu: (jnp.minimum(i, nu[0] - 1), 0)
    exp_blk = lambda i, be, nu: (be[i], 0, 0)
    return pl.pallas_call(
        _expert_kernel,
        grid_spec=pltpu.PrefetchScalarGridSpec(
            num_scalar_prefetch=2,
            grid=(n_slots // bm,),
            in_specs=[
                pl.BlockSpec((bm, d), row_blk),
                pl.BlockSpec((None, d, de2), exp_blk),
                pl.BlockSpec((None, 1, de2), exp_blk),
                pl.BlockSpec((None, de, d), exp_blk),
                pl.BlockSpec((None, 1, d), exp_blk),
            ],
            out_specs=pl.BlockSpec((bm, d), lambda i, be, nu: (i, 0)),
            scratch_shapes=[pltpu.VMEM((d, de2), BF16), pltpu.VMEM((de, d), BF16)],
        ),
        out_shape=jax.ShapeDtypeStruct((n_slots, d), F32),
        compiler_params=_params(("arbitrary",)),
        name="moe_experts",
    )(block_e, n_used, xs, wgu, bgu.reshape(ne, 1, de2), wd, bd.reshape(ne, 1, d))


def _combine_kernel(slot_ref, wt_ref, x_ref, m_ref, ys_ref, o_ref, buf, sem):
    tm = x_ref.shape[0]

    def issue(i, carry):
        for k in range(TOP_K):
            s = slot_ref[i * TOP_K + k]
            pltpu.make_async_copy(ys_ref.at[pl.ds(s, 1)], buf.at[k, pl.ds(i, 1)], sem).start()
        return carry

    lax.fori_loop(0, tm, issue, 0)
    for k in range(TOP_K):
        pltpu.make_async_copy(ys_ref.at[pl.ds(0, tm)], buf.at[k], sem).wait()
    wt = wt_ref[...]
    y = wt[:, 0:1] * buf[0]
    for k in range(1, TOP_K):
        y = y + wt[:, k:k + 1] * buf[k]
    o_ref[...] = x_ref[...] + m_ref[5:6, :] * y


def _combine(slots, wts, xn, mods, ys, n_ctx_tiles):
    b, tt, d = xn.shape
    tm = TOKEN_TILE
    nt = tt // tm
    return pl.pallas_call(
        _combine_kernel,
        grid=(b, nt),
        in_specs=[
            pl.BlockSpec((tm * TOP_K,), lambda bi, t: (bi * nt + t,), memory_space=pltpu.SMEM),
            pl.BlockSpec((None, tm, 8), lambda bi, t: (bi, t, 0)),
            pl.BlockSpec((None, tm, d), lambda bi, t: (bi, t, 0)),
            pl.BlockSpec((None, 6, d), _mod_row_map(n_ctx_tiles, b)),
            pl.BlockSpec(memory_space=pl.ANY),
        ],
        out_specs=pl.BlockSpec((None, tm, d), lambda bi, t: (bi, t, 0)),
        out_shape=jax.ShapeDtypeStruct((b, tt, d), F32),
        scratch_shapes=[pltpu.VMEM((TOP_K, tm, d), F32), pltpu.SemaphoreType.DMA(())],
        compiler_params=_params(("arbitrary", "arbitrary")),
        name="moe_combine",
    )(slots, wts, xn, mods, ys)


def _moe(xn, f, idx, wts, rank, cnt, mods, wgu, bgu, wd, bd, n_ctx_tiles):
    b, tt, d = xn.shape
    t = b * tt
    bm = EXPERT_BLOCK
    n_blocks = (t * TOP_K + bm - 1) // bm + N_EXPERTS
    counts = cnt[0].astype(I32)
    padded = (counts + bm - 1) // bm * bm
    pend = jnp.cumsum(padded)
    pstart = pend - padded
    slots = (pstart[idx[..., :TOP_K]] + rank[..., :TOP_K]).reshape(t * TOP_K)
    n_used = (pend[-1] // bm).reshape(1).astype(I32)
    block_e = jnp.minimum(jnp.searchsorted(pend, jnp.arange(n_blocks, dtype=I32) * bm, side="right"),
                          N_EXPERTS - 1).astype(I32)
    xs = _dispatch(slots, f.reshape(t, d), n_blocks * bm)
    ys = _experts(block_e, n_used, xs, wgu, bgu, wd, bd)
    return _combine(slots, wts, xn, mods, ys, n_ctx_tiles)


def _rope_tables(n_ctx, s_len):
    quarter = HEAD_DIM // 4
    t = np.arange(s_len)
    pos = np.stack([t // GRID_W, t % GRID_W], axis=0).astype(np.float32)
    inv_freq = ROPE_BASE ** (-jnp.arange(quarter, dtype=F32) / quarter)
    ang = jnp.asarray(pos)[:, :, None] * inv_freq
    cos, sin = jnp.cos(ang), jnp.sin(ang)
    cos_h = jnp.concatenate([cos[0], cos[0], cos[1], cos[1]], axis=-1)
    sin_h = jnp.concatenate([-sin[0], sin[0], -sin[1], sin[1]], axis=-1)
    reps = LANES // HEAD_DIM
    cos_l = jnp.tile(cos_h, (1, reps))
    sin_l = jnp.tile(sin_h, (1, reps))
    cos_all = jnp.concatenate([jnp.ones((n_ctx, LANES), F32), cos_l], axis=0)
    sin_all = jnp.concatenate([jnp.zeros((n_ctx, LANES), F32), sin_l], axis=0)
    return cos_all, sin_all


def kernel(x, c, ctx, c_ctx, mod_w, mod_b, norm_mix_g, norm_ffn_g, ev_w_in, ev_w_out, ev_gla_w_a2, ev_gla_b_a2,
           ev_gla_norm_g, ev_na_q_g, ev_na_k_g, ev_na_rpb, od_w_in, od_w_out, od_q_g, od_k_g, od_sinks,
           moe_w_router, moe_b_router, moe_w_gate_up, moe_b_gate_up, moe_w_down, moe_b_down):
    b, s_len, d = x.shape
    n_ctx = ctx.shape[1]
    depth = mod_w.shape[0]
    assert b < MOD_ROWS and n_ctx % TOKEN_TILE == 0 and s_len % TOKEN_TILE == 0
    n_ctx_tiles = n_ctx // TOKEN_TILE
    n_rows = s_len // GRID_W
    kh = min(NA_KH, n_rows)

    xa = jnp.concatenate([ctx, x], axis=1)
    cv = jnp.zeros((MOD_ROWS, d), F32).at[:b].set(c).at[b].set(c_ctx)
    mods_all = _modulation(cv, mod_w, mod_b).reshape(depth, MOD_ROWS, 6, d)
    cos, sin = _rope_tables(n_ctx, s_len)

    for layer in range(depth):
        i = layer // 2
        mods = mods_all[layer]
        gn = norm_mix_g[layer].reshape(1, d)
        if layer % 2 == 0:
            w = ev_w_in[i]
            cuts = np.cumsum([0, GLA_QK, GLA_QK, GLA_V, GLA_V, GLA_RANK, GLA_RANK, NA_W, NA_W, NA_W])
            seg = lambda j: w[:, cuts[j]:cuts[j + 1]]
            pad = jnp.zeros((d, LANES - 2 * GLA_RANK), F32)
            w_bf = jnp.concatenate([seg(0), seg(1), seg(2), seg(3), seg(6), seg(7), seg(8), seg(4), seg(5), pad],
                                   axis=1).astype(BF16)
            a2 = jnp.zeros((LANES, 2 * GLA_QK), F32)
            a2 = a2.at[0:GLA_RANK, 0:GLA_QK].set(ev_gla_w_a2[i, 0])
            a2 = a2.at[GLA_RANK:2 * GLA_RANK, GLA_QK:].set(ev_gla_w_a2[i, 1])
            ba2 = ev_gla_b_a2[i].reshape(1, 2 * GLA_QK)
            qg = jnp.tile(ev_na_q_g[i], NA_HEADS).reshape(1, NA_W)
            kg = jnp.tile(ev_na_k_g[i], NA_HEADS).reshape(1, NA_W)
            gq, gk, gv, gate, la, nq, nk, nv = _proj_even(xa, mods, gn, w_bf, a2, ba2, qg, kg, n_ctx_tiles)
            o2 = _gla(gq, gk, gv, la, n_ctx // GLA_CHUNK)
            o_na = _na(nq, nk, nv, _na_bias_table(ev_na_rpb[i], kh), n_ctx)
            mixer_outs = (o2, gate, o_na)
            extra = ev_gla_norm_g[i].reshape(1, GLA_DV)
            w_out = ev_w_out[i].astype(BF16)
        else:
            w_bf = od_w_in[i].astype(BF16)
            qg = jnp.tile(od_q_g[i], SW_HEADS).reshape(1, SW_Q)
            kg = jnp.tile(od_k_g[i], SW_KV_HEADS).reshape(1, SW_KV)
            q, k, v = _proj_odd(xa, mods, gn, w_bf, qg, kg, cos, sin, n_ctx_tiles)
            mixer_outs = (_swa(q, k, v, od_sinks[i], n_ctx),)
            extra = None
            w_out = od_w_out[i].astype(BF16)
        xn, f, idx, wts, rank, cnt = _post(
            layer % 2 == 0, mixer_outs, extra, w_out, xa, mods, norm_ffn_g[layer].reshape(1, d),
            moe_w_router[layer], moe_b_router[layer].reshape(1, N_EXPERTS), n_ctx_tiles)
        xa = _moe(xn, f, idx, wts, rank, cnt, mods, moe_w_gate_up[layer], moe_b_gate_up[layer],
                  moe_w_down[layer], moe_b_down[layer], n_ctx_tiles)
    return xa[:, n_ctx:, :]
```

```python
import functools

import jax
import jax.numpy as jnp
import numpy as np
from jax import lax
from jax.experimental import pallas as pl
from jax.experimental.pallas import tpu as pltpu

F32 = jnp.float32
BF16 = jnp.bfloat16
I32 = jnp.int32
HIGHEST = lax.Precision.HIGHEST

GRID_W = 64
HEAD_DIM = 64
EPS = 1e-6
GLA_HEADS = 4
GLA_DK = 64
GLA_DV = 128
GLA_RANK = 16
GLA_TAU = 16.0
GLA_CHUNK = 64
GLA_SUB = 16
NA_HEADS = 8
NA_KH = 8
NA_KW = 16
SW_HEADS = 16
SW_KV_HEADS = 2
SW_WINDOW = 128
SW_BLOCK = 128
ROPE_BASE = 10000.0
N_EXPERTS = 32
TOP_K = 4
SWIGLU_LIMIT = 7.0
SWIGLU_ALPHA = 1.702

GLA_QK = GLA_HEADS * GLA_DK
GLA_V = GLA_HEADS * GLA_DV
NA_W = NA_HEADS * HEAD_DIM
SW_Q = SW_HEADS * HEAD_DIM
SW_KV = SW_KV_HEADS * HEAD_DIM

LANES = 128
VMEM_LIMIT_BYTES = 56 * 1024 * 1024
NEG = -1e30

TOKEN_TILE = 256
EXPERT_BLOCK = 512
MOD_ROWS = 16


def _dot(a, b, **kw):
    return jnp.dot(a, b, preferred_element_type=F32, **kw)


def _dot_nt(a, b):
    return lax.dot_general(a, b, (((1,), (1,)), ((), ())), preferred_element_type=F32)


def _dot_tn(a, b):
    return lax.dot_general(a, b, (((0,), (0,)), ((), ())), preferred_element_type=F32)


def _split_bf16(x):
    hi = x.astype(BF16)
    lo = (x - hi.astype(F32)).astype(BF16)
    return hi, lo


def _params(sem, vmem=VMEM_LIMIT_BYTES):
    return pltpu.CompilerParams(dimension_semantics=sem, vmem_limit_bytes=vmem)


def _mod_kernel(cv_ref, w_ref, b_ref, o_ref):
    cv = cv_ref[...]
    s = cv * jax.nn.sigmoid(cv)
    o_ref[...] = _dot(s, w_ref[...], precision=HIGHEST) + b_ref[...]


def _modulation(cv, mod_w, mod_b):
    depth, d, n = mod_w.shape
    tn = 1536
    return pl.pallas_call(
        _mod_kernel,
        grid=(depth, n // tn),
        in_specs=[
            pl.BlockSpec((MOD_ROWS, d), lambda l, j: (0, 0)),
            pl.BlockSpec((None, d, tn), lambda l, j: (l, 0, j)),
            pl.BlockSpec((None, 1, tn), lambda l, j: (l, 0, j)),
        ],
        out_specs=pl.BlockSpec((None, MOD_ROWS, tn), lambda l, j: (l, 0, j)),
        out_shape=jax.ShapeDtypeStruct((depth, MOD_ROWS, n), F32),
        compiler_params=_params(("arbitrary", "arbitrary")),
        name="modulation",
    )(cv, mod_w, mod_b.reshape(depth, 1, n))


def _norm_mod(x, g_row, m_ref, shift_row, scale_row):
    ms = jnp.mean(x * x, axis=-1, keepdims=True)
    h = x * lax.rsqrt(ms + EPS) * g_row
    return h * (1.0 + m_ref[scale_row:scale_row + 1, :]) + m_ref[shift_row:shift_row + 1, :]


def _head_pair_ones():
    r = lax.broadcasted_iota(I32, (LANES, LANES), 0) // HEAD_DIM
    c = lax.broadcasted_iota(I32, (LANES, LANES), 1) // HEAD_DIM
    return jnp.where(r == c, 1.0, 0.0).astype(BF16)


def _head_rms(y, g_row, ones_blk):
    outs = []
    for j in range(y.shape[1] // LANES):
        s = y[:, j * LANES:(j + 1) * LANES]
        hi, lo = _split_bf16(s * s)
        ms = (_dot(hi, ones_blk) + _dot(lo, ones_blk)) * (1.0 / HEAD_DIM)
        outs.append(s * lax.rsqrt(ms + EPS))
    return jnp.concatenate(outs, axis=1) * g_row


def _mod_row_map(n_ctx_tiles, n_batch):
    return lambda b, t: (jnp.where(t < n_ctx_tiles, n_batch, b), 0, 0)


def _proj_even_kernel(x_ref, m_ref, gn_ref, w_ref, a2_ref, ba2_ref, qg_ref, kg_ref,
                      gq_ref, gk_ref, gv_ref, gate_ref, la_ref, nq_ref, nk_ref, nv_ref):
    h = _norm_mod(x_ref[...], gn_ref[...], m_ref, 0, 1)
    y = _dot(h.astype(BF16), w_ref[...])
    c0 = 0
    gq_ref[...] = y[:, c0:c0 + GLA_QK] * (GLA_DK ** -0.5)
    c0 += GLA_QK
    gk_ref[...] = y[:, c0:c0 + GLA_QK]
    c0 += GLA_QK
    gv_ref[...] = y[:, c0:c0 + GLA_V].astype(BF16)
    c0 += GLA_V
    gate_ref[...] = y[:, c0:c0 + GLA_V].astype(BF16)
    c0 += GLA_V
    ones_blk = _head_pair_ones()
    nq_ref[...] = (_head_rms(y[:, c0:c0 + NA_W], qg_ref[...], ones_blk) * (HEAD_DIM ** -0.5)).astype(BF16)
    c0 += NA_W
    nk_ref[...] = _head_rms(y[:, c0:c0 + NA_W], kg_ref[...], ones_blk).astype(BF16)
    c0 += NA_W
    nv_ref[...] = y[:, c0:c0 + NA_W].astype(BF16)
    c0 += NA_W
    a = y[:, c0:c0 + LANES]
    z = _dot(a, a2_ref[...], precision=HIGHEST) + ba2_ref[...]
    la_ref[...] = (jnp.minimum(z, 0.0) - jnp.log1p(jnp.exp(-jnp.abs(z)))) * (1.0 / GLA_TAU)


def _proj_even(xa, mods, gn, w_bf, a2, ba2, qg, kg, n_ctx_tiles):
    b, tt, d = xa.shape
    tm = TOKEN_TILE
    n = w_bf.shape[1]
    tok = lambda w: pl.BlockSpec((None, tm, w), lambda bi, t: (bi, t, 0))
    full = lambda shape: pl.BlockSpec(shape, lambda bi, t: (0,) * len(shape))
    widths = [(GLA_QK, F32), (GLA_QK, F32), (GLA_V, BF16), (GLA_V, BF16), (2 * GLA_QK, F32),
              (NA_W, BF16), (NA_W, BF16), (NA_W, BF16)]
    return pl.pallas_call(
        _proj_even_kernel,
        grid=(b, tt // tm),
        in_specs=[
            tok(d),
            pl.BlockSpec((None, 6, d), _mod_row_map(n_ctx_tiles, b)),
            full((1, d)), full((d, n)), full((LANES, 2 * GLA_QK)), full((1, 2 * GLA_QK)),
            full((1, NA_W)), full((1, NA_W)),
        ],
        out_specs=[tok(w) for w, _ in widths],
        out_shape=[jax.ShapeDtypeStruct((b, tt, w), dt) for w, dt in widths],
        compiler_params=_params(("parallel", "parallel")),
        name="proj_even",
    )(xa, mods, gn, w_bf, a2, ba2, qg, kg)


def _rope(y, cos, sin_signed, lo_mask):
    w = y.shape[1]
    reps = w // LANES
    cos_t = jnp.concatenate([cos] * reps, axis=1) if reps > 1 else cos
    sin_t = jnp.concatenate([sin_signed] * reps, axis=1) if reps > 1 else sin_signed
    msk = jnp.concatenate([lo_mask] * reps, axis=1) if reps > 1 else lo_mask
    quarter = HEAD_DIM // 4
    swapped = jnp.where(msk, pltpu.roll(y, w - quarter, 1), pltpu.roll(y, quarter, 1))
    return y * cos_t + swapped * sin_t


def _proj_odd_kernel(x_ref, m_ref, gn_ref, w_ref, qg_ref, kg_ref, cos_ref, sin_ref,
                     q_ref, k_ref, v_ref):
    h = _norm_mod(x_ref[...], gn_ref[...], m_ref, 0, 1)
    y = _dot(h.astype(BF16), w_ref[...])
    ones_blk = _head_pair_ones()
    lane = lax.broadcasted_iota(I32, (1, LANES), 1)
    lo_mask = (lane % (HEAD_DIM // 2)) < (HEAD_DIM // 4)
    cos = cos_ref[...]
    sin = sin_ref[...]
    q = _head_rms(y[:, 0:SW_Q], qg_ref[...], ones_blk)
    q_ref[...] = (_rope(q, cos, sin, lo_mask) * (HEAD_DIM ** -0.5)).astype(BF16)
    k = _head_rms(y[:, SW_Q:SW_Q + SW_KV], kg_ref[...], ones_blk)
    k_ref[...] = _rope(k, cos, sin, lo_mask).astype(BF16)
    v_ref[...] = y[:, SW_Q + SW_KV:SW_Q + 2 * SW_KV].astype(BF16)


def _proj_odd(xa, mods, gn, w_bf, qg, kg, cos, sin, n_ctx_tiles):
    b, tt, d = xa.shape
    tm = TOKEN_TILE
    n = w_bf.shape[1]
    tok = lambda w: pl.BlockSpec((None, tm, w), lambda bi, t: (bi, t, 0))
    full = lambda shape: pl.BlockSpec(shape, lambda bi, t: (0,) * len(shape))
    pos = pl.BlockSpec((tm, LANES), lambda bi, t: (t, 0))
    widths = [SW_Q, SW_KV, SW_KV]
    return pl.pallas_call(
        _proj_odd_kernel,
        grid=(b, tt // tm),
        in_specs=[
            tok(d),
            pl.BlockSpec((None, 6, d), _mod_row_map(n_ctx_tiles, b)),
            full((1, d)), full((d, n)), full((1, SW_Q)), full((1, SW_KV)), pos, pos,
        ],
        out_specs=[tok(w) for w in widths],
        out_shape=[jax.ShapeDtypeStruct((b, tt, w), BF16) for w in widths],
        compiler_params=_params(("parallel", "parallel")),
        name="proj_odd",
    )(xa, mods, gn, w_bf, qg, kg, cos, sin)


def _gla_kernel(q_ref, k_ref, v_ref, g_ref, o_ref, st_ref):
    c = GLA_CHUNK
    n_chunks = q_ref.shape[0] // c
    direction = pl.program_id(0)

    @pl.when(pl.program_id(2) == 0)
    def _():
        st_ref[...] = jnp.zeros_like(st_ref)

    sgn = jnp.where(direction == 0, 1, -1)
    r = lax.broadcasted_iota(I32, (2 * c, c), 0)
    j = lax.broadcasted_iota(I32, (2 * c, c), 1)
    i = r % c
    incl = jnp.where(sgn * (i - j) >= 0, 1.0, 0.0)
    blk = jnp.where(sgn * (i // GLA_SUB - j // GLA_SUB) > 0, 1.0, 0.0)
    scan_mat = jnp.where(r >= c, blk, incl).astype(BF16)
    causal = scan_mat[0:c, :].astype(F32)
    row = lax.broadcasted_iota(I32, (c, 1), 0)
    reachable = []
    for a in range(c // GLA_SUB):
        edge = jnp.where(direction == 0, a * GLA_SUB + GLA_SUB - 1, a * GLA_SUB)
        reachable.append(sgn * (edge - row) >= 0)

    for u in range(n_chunks):
        ci = jnp.where(direction == 0, u, n_chunks - 1 - u)
        rows = pl.ds(pl.multiple_of(ci * c, c), c)
        g_hi, g_lo = _split_bf16(g_ref[rows, :])
        cr = _dot(scan_mat, g_hi) + _dot(scan_mat, g_lo)
        cum = cr[0:c, :]
        ref_pt = cr[c:2 * c, :]
        tot = jnp.where(direction == 0, cum[c - 1:c, :], cum[0:1, :])

        q = q_ref[rows, :]
        k = k_ref[rows, :]
        q_state = (q * jnp.exp(cum)).astype(BF16)
        q_sub = (q * jnp.exp(cum - ref_pt)).astype(BF16)
        k_state = (k * jnp.exp(tot - cum)).astype(BF16)
        decay = jnp.exp(tot)

        attn_rows = [[] for _ in range(GLA_HEADS)]
        for a in range(c // GLA_SUB):
            lo = a * GLA_SUB
            ra = ref_pt[lo:lo + 1, :]
            k_sub = (k * jnp.exp(jnp.where(reachable[a], ra - cum, 0.0))).astype(BF16)
            for h in range(GLA_HEADS):
                hs = slice(h * GLA_DK, (h + 1) * GLA_DK)
                attn_rows[h].append(_dot_nt(q_sub[lo:lo + GLA_SUB, hs], k_sub[:, hs]))

        v = v_ref[rows, :]
        for h in range(GLA_HEADS):
            hs = slice(h * GLA_DK, (h + 1) * GLA_DK)
            vs = slice(h * GLA_DV, (h + 1) * GLA_DV)
            attn = (jnp.concatenate(attn_rows[h], axis=0) * causal).astype(BF16)
            st = st_ref[h]
            o_ref[rows, vs] = _dot(attn, v[:, vs]) + _dot_nt(q_state[:, hs], st.astype(BF16))
            st_ref[h] = st * decay[:, hs] + _dot_tn(v[:, vs], k_state[:, hs])


def _gla(gq, gk, gv, la, n_ctx):
    b, tt, _ = gq.shape
    rows = TOKEN_TILE
    nb = tt // rows
    n_ctx_blk = n_ctx // rows

    def blk(d, s):
        bwd = jnp.where(s < n_ctx_blk, n_ctx_blk - 1 - s, nb - 1 + n_ctx_blk - s)
        return jnp.where(d == 0, s, bwd)

    return pl.pallas_call(
        _gla_kernel,
        grid=(2, b, nb),
        in_specs=[
            pl.BlockSpec((None, rows, GLA_QK), lambda d, bi, s: (bi, blk(d, s), 0)),
            pl.BlockSpec((None, rows, GLA_QK), lambda d, bi, s: (bi, blk(d, s), 0)),
            pl.BlockSpec((None, rows, GLA_V), lambda d, bi, s: (bi, blk(d, s), 0)),
            pl.BlockSpec((None, rows, GLA_QK), lambda d, bi, s: (bi, blk(d, s), d)),
        ],
        out_specs=pl.BlockSpec((None, None, rows, GLA_V), lambda d, bi, s: (d, bi, blk(d, s), 0)),
        out_shape=jax.ShapeDtypeStruct((2, b, tt, GLA_V), F32),
        scratch_shapes=[pltpu.VMEM((GLA_HEADS, GLA_DV, GLA_DK), F32)],
        compiler_params=_params(("arbitrary", "arbitrary", "arbitrary")),
        name="gla_scan",
    )(gq, gk, gv, la)


def _na_bias_table(rpb, kh):
    n_heads = rpb.shape[0]
    qc = np.arange(GRID_W)[:, None]
    kc = np.arange(GRID_W)[None, :]
    win0 = np.clip(qc - NA_KW // 2, 0, GRID_W - NA_KW)
    valid = (kc >= win0) & (kc < win0 + NA_KW)
    side = GRID_W - NA_KW
    rp = jnp.pad(rpb, ((0, 0), (0, 0), (side, side)))
    base = jnp.stack([rp[:, :, GRID_W - 1 - c:2 * GRID_W - 1 - c] for c in range(GRID_W)], axis=2)
    base = jnp.where(valid[None, None], base, NEG)
    tab = jnp.stack([base[:, NA_KH - 1 - a:NA_KH - 1 - a + kh] for a in range(kh)], axis=1)
    tab = tab.reshape(n_heads // 2, 2, kh, kh, GRID_W, GRID_W).transpose(0, 2, 1, 4, 3, 5)
    return tab.reshape(n_heads // 2, kh, 2 * GRID_W, kh * GRID_W).astype(F32)


def _na_kernel(q_ref, k_ref, v_ref, bias_ref, o_ref, *, n_ctx, n_ctx_blk, rows_per_blk, n_rows, kh):
    t = pl.program_id(2)
    lane = lax.broadcasted_iota(I32, (1, LANES), 1)
    first = lane < HEAD_DIM
    kc = k_ref[0:n_ctx, :]
    vc = v_ref[0:n_ctx, :]

    def stack_heads(q):
        z = jnp.zeros_like(q)
        return jnp.concatenate([jnp.where(first, q, z), jnp.where(first, z, q)], axis=0)

    def unstack_heads(o):
        n = o.shape[0] // 2
        return jnp.where(first, o[0:n], o[n:2 * n])

    @pl.when(t < n_ctx_blk)
    def _():
        s = _dot_nt(stack_heads(q_ref[...]), kc)
        m = jnp.max(s, axis=-1, keepdims=True)
        p = jnp.exp(s - m)
        den = jnp.sum(p, axis=-1, keepdims=True)
        o_ref[...] = unstack_heads(_dot(p.astype(BF16), vc) / den).astype(o_ref.dtype)

    @pl.when(t >= n_ctx_blk)
    def _():
        for i in range(rows_per_blk):
            r = (t - n_ctx_blk) * rows_per_blk + i
            row0 = jnp.clip(r - kh // 2, 0, n_rows - kh)
            cls = r - row0
            start = pl.multiple_of(n_ctx + row0 * GRID_W, GRID_W)
            kb = k_ref[pl.ds(start, kh * GRID_W), :]
            vb = v_ref[pl.ds(start, kh * GRID_W), :]
            qs = stack_heads(q_ref[i * GRID_W:(i + 1) * GRID_W, :])
            s = _dot_nt(qs, kb) + bias_ref[cls]
            sc = _dot_nt(qs, kc)
            m = jnp.maximum(jnp.max(s, axis=-1, keepdims=True), jnp.max(sc, axis=-1, keepdims=True))
            p = jnp.exp(s - m)
            pc = jnp.exp(sc - m)
            den = jnp.sum(p, axis=-1, keepdims=True) + jnp.sum(pc, axis=-1, keepdims=True)
            o = (_dot(p.astype(BF16), vb) + _dot(pc.astype(BF16), vc)) / den
            o_ref[i * GRID_W:(i + 1) * GRID_W, :] = unstack_heads(o).astype(o_ref.dtype)


def _na(nq, nk, nv, bias_tab, n_ctx):
    b, tt, _ = nq.shape
    tq = TOKEN_TILE
    n_rows = (tt - n_ctx) // GRID_W
    kh = bias_tab.shape[1]
    kern = functools.partial(_na_kernel, n_ctx=n_ctx, n_ctx_blk=n_ctx // tq, rows_per_blk=tq // GRID_W,
                             n_rows=n_rows, kh=kh)
    return pl.pallas_call(
        kern,
        grid=(b, NA_HEADS // 2, tt // tq),
        in_specs=[
            pl.BlockSpec((None, tq, LANES), lambda bi, p, t: (bi, t, p)),
            pl.BlockSpec((None, tt, LANES), lambda bi, p, t: (bi, 0, p)),
            pl.BlockSpec((None, tt, LANES), lambda bi, p, t: (bi, 0, p)),
            pl.BlockSpec((None, kh, 2 * GRID_W, kh * GRID_W), lambda bi, p, t: (p, 0, 0, 0)),
        ],
        out_specs=pl.BlockSpec((None, tq, LANES), lambda bi, p, t: (bi, t, p)),
        out_shape=jax.ShapeDtypeStruct((b, tt, NA_W), BF16),
        compiler_params=_params(("parallel", "parallel", "arbitrary")),
        name="na_attn",
    )(nq, nk, nv, bias_tab)


def _swa_kernel(sink_ref, q_ref, k_ref, v_ref, o_ref, *, n_ctx, n_lat_blk):
    blk = SW_BLOCK
    n = pl.program_id(1)
    n_ctx_blk = n_ctx // blk
    jb = n - n_ctx_blk
    sb = jnp.clip(jb - 1, 0, n_lat_blk - 3)
    start = pl.multiple_of(n_ctx + sb * blk, blk)
    keys = jnp.concatenate([k_ref[pl.ds(start, 3 * blk), :], k_ref[0:n_ctx, :]], axis=0)
    vals = jnp.concatenate([v_ref[pl.ds(start, 3 * blk), :], v_ref[0:n_ctx, :]], axis=0)
    nk = 3 * blk + n_ctx
    iq = lax.broadcasted_iota(I32, (blk, nk), 0)
    ik = lax.broadcasted_iota(I32, (blk, nk), 1)
    offset = jnp.where(n < n_ctx_blk, 4 * SW_WINDOW, (jb - sb) * blk)
    dist = jnp.where(ik >= 3 * blk, 0, iq - ik + offset)
    bias = jnp.where(jnp.abs(dist) <= SW_WINDOW, 0.0, NEG)
    q = q_ref[...]
    group = SW_HEADS // SW_KV_HEADS
    bias_g = jnp.concatenate([bias] * group, axis=0)
    for g in range(SW_KV_HEADS):
        gs = slice(g * HEAD_DIM, (g + 1) * HEAD_DIM)
        heads = range(g * group, (g + 1) * group)
        qg = jnp.concatenate([q[:, h * HEAD_DIM:(h + 1) * HEAD_DIM] for h in heads], axis=0)
        sink = jnp.concatenate([jnp.full((blk, 1), sink_ref[h], F32) for h in heads], axis=0)
        s = _dot_nt(qg, keys[:, gs]) + bias_g
        m = jnp.maximum(jnp.max(s, axis=-1, keepdims=True), sink)
        p = jnp.exp(s - m)
        den = jnp.sum(p, axis=-1, keepdims=True) + jnp.exp(sink - m)
        o = _dot(p.astype(BF16), vals[:, gs]) / den
        for pr in range(group // 2):
            pair = [o[(2 * pr + e) * blk:(2 * pr + e + 1) * blk, :] for e in range(2)]
            col = (g * group // 2 + pr) * LANES
            o_ref[:, col:col + LANES] = jnp.concatenate(pair, axis=1).astype(o_ref.dtype)


def _swa(q, k, v, sinks, n_ctx):
    b, tt, _ = q.shape
    blk = SW_BLOCK
    kern = functools.partial(_swa_kernel, n_ctx=n_ctx, n_lat_blk=(tt - n_ctx) // blk)
    return pl.pallas_call(
        kern,
        grid=(b, tt // blk),
        in_specs=[
            pl.BlockSpec(memory_space=pltpu.SMEM),
            pl.BlockSpec((None, blk, SW_Q), lambda bi, n: (bi, n, 0)),
            pl.BlockSpec((None, tt, SW_KV), lambda bi, n: (bi, 0, 0)),
            pl.BlockSpec((None, tt, SW_KV), lambda bi, n: (bi, 0, 0)),
        ],
        out_specs=pl.BlockSpec((None, blk, SW_Q), lambda bi, n: (bi, n, 0)),
        out_shape=jax.ShapeDtypeStruct((b, tt, SW_Q), BF16),
        compiler_params=_params(("parallel", "arbitrary")),
        name="swa_attn",
    )(sinks, q, k, v)


def _route(f, wr_ref, br_ref, before_ref, run_ref, idx_ref, wt_ref, rank_ref, cnt_ref):
    tm = f.shape[0]
    f_hi, f_lo = _split_bf16(f)
    logits = _dot(f_hi, wr_ref[0]) + (_dot(f_lo, wr_ref[0]) + _dot(f_hi, wr_ref[1])) + br_ref[...]
    e_iota = lax.broadcasted_iota(I32, (tm, N_EXPERTS), 1).astype(F32)
    lane8 = lax.broadcasted_iota(I32, (tm, 8), 1)
    vals, hots = [], []
    idx_out = jnp.zeros((tm, 8), I32)
    for k in range(TOP_K):
        mx = jnp.max(logits, axis=-1, keepdims=True)
        am = jnp.min(jnp.where(logits == mx, e_iota, float(N_EXPERTS)), axis=-1, keepdims=True)
        hot = e_iota == am
        vals.append(mx)
        hots.append(hot)
        idx_out = jnp.where(lane8 == k, am.astype(I32), idx_out)
        logits = jnp.where(hot, -jnp.inf, logits)
    exps = [jnp.exp(v - vals[0]) for v in vals]
    den = exps[0] + exps[1] + exps[2] + exps[3]
    multihot = jnp.zeros((tm, N_EXPERTS), F32)
    wt_out = jnp.zeros((tm, 8), F32)
    for k in range(TOP_K):
        multihot = multihot + jnp.where(hots[k], 1.0, 0.0)
        wt_out = jnp.where(lane8 == k, exps[k] / den, wt_out)
    base = _dot(before_ref[...], multihot.astype(BF16)) + run_ref[...]
    rank_out = jnp.zeros((tm, 8), I32)
    for k in range(TOP_K):
        rk = jnp.sum(jnp.where(hots[k], base, 0.0), axis=-1, keepdims=True)
        rank_out = jnp.where(lane8 == k, rk.astype(I32), rank_out)
    run_ref[...] = run_ref[...] + jnp.sum(multihot, axis=0, keepdims=True)
    idx_ref[...] = idx_out
    wt_ref[...] = wt_out
    rank_ref[...] = rank_out
    cnt_ref[...] = run_ref[...]


def _post_common(mix, x_ref, m_ref, fg_ref, wr_ref, br_ref, before_ref, xn_ref, f_ref, idx_ref, wt_ref, rank_ref,
                 cnt_ref, run_ref):
    @pl.when((pl.program_id(0) == 0) & (pl.program_id(1) == 0))
    def _():
        run_ref[...] = jnp.zeros_like(run_ref)

    xn = x_ref[...] + m_ref[2:3, :] * mix
    xn_ref[...] = xn
    f = _norm_mod(xn, fg_ref[...], m_ref, 3, 4)
    f_ref[...] = f
    _route(f, wr_ref, br_ref, before_ref, run_ref, idx_ref, wt_ref, rank_ref, cnt_ref)


def _post_even_kernel(of_ref, ob_ref, gate_ref, na_ref, gng_ref, w_ref, x_ref, m_ref, fg_ref, wr_ref, br_ref,
                      before_ref, xn_ref, f_ref, idx_ref, wt_ref, rank_ref, cnt_ref, run_ref):
    o = of_ref[...] + ob_ref[...]
    gate = gate_ref[...].astype(F32)
    gn = gng_ref[...]
    parts = []
    for h in range(GLA_HEADS):
        oh = o[:, h * GLA_DV:(h + 1) * GLA_DV]
        ms = jnp.mean(oh * oh, axis=-1, keepdims=True)
        parts.append(oh * lax.rsqrt(ms + EPS) * gn)
    gla = jnp.concatenate(parts, axis=1) * (gate * jax.nn.sigmoid(gate))
    mix = _dot(gla.astype(BF16), w_ref[0:GLA_V, :]) + _dot(na_ref[...], w_ref[GLA_V:GLA_V + NA_W, :])
    _post_common(mix, x_ref, m_ref, fg_ref, wr_ref, br_ref, before_ref, xn_ref, f_ref, idx_ref, wt_ref, rank_ref,
                 cnt_ref, run_ref)


def _post_odd_kernel(o_ref, w_ref, x_ref, m_ref, fg_ref, wr_ref, br_ref, before_ref,
                     xn_ref, f_ref, idx_ref, wt_ref, rank_ref, cnt_ref, run_ref):
    mix = _dot(o_ref[...], w_ref[...])
    _post_common(mix, x_ref, m_ref, fg_ref, wr_ref, br_ref, before_ref, xn_ref, f_ref, idx_ref, wt_ref, rank_ref,
                 cnt_ref, run_ref)


def _post(even, mixer_outs, extra, w_bf, xa, mods, fg, wr, br, n_ctx_tiles):
    b, tt, d = xa.shape
    tm = TOKEN_TILE
    tok = lambda w: pl.BlockSpec((None, tm, w), lambda bi, t: (bi, t, 0))
    full = lambda shape: pl.BlockSpec(shape, lambda bi, t: (0,) * len(shape))
    if even:
        o2, gate, na = mixer_outs
        lead_specs = [
            pl.BlockSpec((None, None, tm, GLA_V), lambda bi, t: (0, bi, t, 0)),
            pl.BlockSpec((None, None, tm, GLA_V), lambda bi, t: (1, bi, t, 0)),
            tok(GLA_V), tok(NA_W), full((1, GLA_DV)),
        ]
        lead_args = [o2, o2, gate, na, extra]
        kern = _post_even_kernel
    else:
        (o,) = mixer_outs
        lead_specs = [tok(SW_Q)]
        lead_args = [o]
        kern = _post_odd_kernel
    small = lambda: pl.BlockSpec((None, tm, 8), lambda bi, t: (bi, t, 0))
    before = jnp.asarray(np.tril(np.ones((tm, tm), np.float32), -1), BF16)
    return pl.pallas_call(
        kern,
        grid=(b, tt // tm),
        in_specs=lead_specs + [
            full(w_bf.shape), tok(d),
            pl.BlockSpec((None, 6, d), _mod_row_map(n_ctx_tiles, b)),
            full((1, d)), full((2, d, N_EXPERTS)), full((1, N_EXPERTS)), full((tm, tm)),
        ],
        out_specs=[tok(d), tok(d), small(), small(), small(), full((1, N_EXPERTS))],
        out_shape=[
            jax.ShapeDtypeStruct((b, tt, d), F32), jax.ShapeDtypeStruct((b, tt, d), F32),
            jax.ShapeDtypeStruct((b, tt, 8), I32), jax.ShapeDtypeStruct((b, tt, 8), F32),
            jax.ShapeDtypeStruct((b, tt, 8), I32), jax.ShapeDtypeStruct((1, N_EXPERTS), F32),
        ],
        scratch_shapes=[pltpu.VMEM((1, N_EXPERTS), F32)],
        compiler_params=_params(("arbitrary", "arbitrary")),
        name="post_even" if even else "post_odd",
    )(*lead_args, w_bf, xa, mods, fg, wr, br, before)


def _dispatch_kernel(slot_ref, f_ref, xs_in_ref, xs_ref, sem):
    del xs_in_ref
    tm = f_ref.shape[0]

    def issue(i, carry):
        for k in range(TOP_K):
            s = slot_ref[i * TOP_K + k]
            pltpu.make_async_copy(f_ref.at[pl.ds(i, 1)], xs_ref.at[pl.ds(s, 1)], sem).start()
        return carry

    lax.fori_loop(0, tm, issue, 0)
    for _ in range(TOP_K):
        pltpu.make_async_copy(f_ref, xs_ref.at[pl.ds(0, tm)], sem).wait()


def _dispatch(slots, f2, n_slots):
    t, d = f2.shape
    tm = TOKEN_TILE
    xs0 = jnp.zeros((n_slots, d), F32)
    return pl.pallas_call(
        _dispatch_kernel,
        grid=(t // tm,),
        in_specs=[
            pl.BlockSpec((tm * TOP_K,), lambda i: (i,), memory_space=pltpu.SMEM),
            pl.BlockSpec((tm, d), lambda i: (i, 0)),
            pl.BlockSpec(memory_space=pl.ANY),
        ],
        out_specs=pl.BlockSpec(memory_space=pl.ANY),
        out_shape=jax.ShapeDtypeStruct((n_slots, d), F32),
        scratch_shapes=[pltpu.SemaphoreType.DMA(())],
        input_output_aliases={2: 0},
        compiler_params=_params(("arbitrary",)),
        name="moe_dispatch",
    )(slots, f2, xs0)


def _expert_kernel(be_ref, nu_ref, xs_ref, wgu_ref, bgu_ref, wd_ref, bd_ref, ys_ref, wgu_bf, wd_bf):
    i = pl.program_id(0)
    prev = be_ref[jnp.maximum(i - 1, 0)]
    fresh = (i == 0) | (be_ref[i] != prev)

    @pl.when(fresh & (i < nu_ref[0]))
    def _():
        wgu_bf[...] = wgu_ref[...].astype(BF16)
        wd_bf[...] = wd_ref[...].astype(BF16)

    @pl.when(i < nu_ref[0])
    def _():
        x = xs_ref[...].astype(BF16)
        de = wd_bf.shape[0]
        half = de // 2
        acc = None
        for c in range(2):
            glu = _dot(x, wgu_bf[:, c * half:(c + 1) * half]) + bgu_ref[:, c * half:(c + 1) * half]
            lin = _dot(x, wgu_bf[:, de + c * half:de + (c + 1) * half]) + bgu_ref[:, de + c * half:de + (c + 1) * half]
            glu = jnp.minimum(glu, SWIGLU_LIMIT)
            lin = jnp.clip(lin, -SWIGLU_LIMIT, SWIGLU_LIMIT)
            act = (glu * jax.nn.sigmoid(SWIGLU_ALPHA * glu) * (lin + 1.0)).astype(BF16)
            part = _dot(act, wd_bf[c * half:(c + 1) * half, :])
            acc = part if acc is None else acc + part
        ys_ref[...] = acc + bd_ref[...]

    @pl.when(i >= nu_ref[0])
    def _():
        ys_ref[...] = jnp.zeros_like(ys_ref)


def _experts(block_e, n_used, xs, layer, wgu, bgu, wd, bd):
    n_slots, d = xs.shape
    bm = EXPERT_BLOCK
    depth, ne, _, de2 = wgu.shape
    de = wd.shape[2]
    row_blk = lambda i, be, nu: (jnp.minimum(i, nu[0] - 1), 0)
    exp_blk = lambda i, be, nu: (layer, be[i], 0, 0)
    return pl.pallas_call(
        _expert_kernel,
        grid_spec=pltpu.PrefetchScalarGridSpec(
            num_scalar_prefetch=2,
            grid=(n_slots // bm,),
            in_specs=[
                pl.BlockSpec((bm, d), row_blk),
                pl.BlockSpec((None, None, d, de2), exp_blk),
                pl.BlockSpec((None, None, 1, de2), exp_blk),
                pl.BlockSpec((None, None, de, d), exp_blk),
                pl.BlockSpec((None, None, 1, d), exp_blk),
            ],
            out_specs=pl.BlockSpec((bm, d), lambda i, be, nu: (i, 0)),
            scratch_shapes=[pltpu.VMEM((d, de2), BF16), pltpu.VMEM((de, d), BF16)],
        ),
        out_shape=jax.ShapeDtypeStruct((n_slots, d), F32),
        compiler_params=_params(("arbitrary",)),
        name="moe_experts",
    )(block_e, n_used, xs, wgu, bgu.reshape(depth, ne, 1, de2), wd, bd.reshape(depth, ne, 1, d))


def _combine_kernel(slot_ref, wt_ref, x_ref, m_ref, ys_ref, o_ref, buf, sem):
    tm = x_ref.shape[0]

    def issue(i, carry):
        for k in range(TOP_K):
            s = slot_ref[i * TOP_K + k]
            pltpu.make_async_copy(ys_ref.at[pl.ds(s, 1)], buf.at[k, pl.ds(i, 1)], sem).start()
        return carry

    lax.fori_loop(0, tm, issue, 0)
    for k in range(TOP_K):
        pltpu.make_async_copy(ys_ref.at[pl.ds(0, tm)], buf.at[k], sem).wait()
    wt = wt_ref[...]
    y = wt[:, 0:1] * buf[0]
    for k in range(1, TOP_K):
        y = y + wt[:, k:k + 1] * buf[k]
    o_ref[...] = x_ref[...] + m_ref[5:6, :] * y


def _combine(slots, wts, xn, mods, ys, n_ctx_tiles):
    b, tt, d = xn.shape
    tm = TOKEN_TILE
    nt = tt // tm
    return pl.pallas_call(
        _combine_kernel,
        grid=(b, nt),
        in_specs=[
            pl.BlockSpec((tm * TOP_K,), lambda bi, t: (bi * nt + t,), memory_space=pltpu.SMEM),
            pl.BlockSpec((None, tm, 8), lambda bi, t: (bi, t, 0)),
            pl.BlockSpec((None, tm, d), lambda bi, t: (bi, t, 0)),
            pl.BlockSpec((None, 6, d), _mod_row_map(n_ctx_tiles, b)),
            pl.BlockSpec(memory_space=pl.ANY),
        ],
        out_specs=pl.BlockSpec((None, tm, d), lambda bi, t: (bi, t, 0)),
        out_shape=jax.ShapeDtypeStruct((b, tt, d), F32),
        scratch_shapes=[pltpu.VMEM((TOP_K, tm, d), F32), pltpu.SemaphoreType.DMA(())],
        compiler_params=_params(("arbitrary", "arbitrary")),
        name="moe_combine",
    )(slots, wts, xn, mods, ys)


def _moe(xn, f, idx, wts, rank, cnt, mods, layer, wgu, bgu, wd, bd, n_ctx_tiles):
    b, tt, d = xn.shape
    t = b * tt
    bm = EXPERT_BLOCK
    n_blocks = (t * TOP_K + bm - 1) // bm + N_EXPERTS
    counts = cnt[0].astype(I32)
    padded = (counts + bm - 1) // bm * bm
    e_ids = np.arange(N_EXPERTS)
    pend = jnp.sum(jnp.where(e_ids[None, :] <= e_ids[:, None], padded[None, :], 0), axis=1)
    pstart = pend - padded
    sel = idx[..., :TOP_K, None] == e_ids
    slots = (rank[..., :TOP_K] + jnp.sum(jnp.where(sel, pstart, 0), axis=-1)).reshape(t * TOP_K)
    n_used = (pend[N_EXPERTS - 1] // bm).reshape(1).astype(I32)
    starts = np.arange(n_blocks, dtype=np.int32)[:, None] * bm
    block_e = jnp.minimum(jnp.sum((pend[None, :] <= starts).astype(I32), axis=1), N_EXPERTS - 1)
    xs = _dispatch(slots, f.reshape(t, d), n_blocks * bm)
    ys = _experts(block_e, n_used, xs, layer, wgu, bgu, wd, bd)
    return _combine(slots, wts, xn, mods, ys, n_ctx_tiles)


def _rope_tables(n_ctx, s_len):
    quarter = HEAD_DIM // 4
    t = np.arange(s_len)
    pos = np.stack([t // GRID_W, t % GRID_W], axis=0).astype(np.float32)
    inv_freq = ROPE_BASE ** (-jnp.arange(quarter, dtype=F32) / quarter)
    ang = jnp.asarray(pos)[:, :, None] * inv_freq
    cos, sin = jnp.cos(ang), jnp.sin(ang)
    cos_h = jnp.concatenate([cos[0], cos[0], cos[1], cos[1]], axis=-1)
    sin_h = jnp.concatenate([-sin[0], sin[0], -sin[1], sin[1]], axis=-1)
    reps = LANES // HEAD_DIM
    cos_l = jnp.tile(cos_h, (1, reps))
    sin_l = jnp.tile(sin_h, (1, reps))
    cos_all = jnp.concatenate([jnp.ones((n_ctx, LANES), F32), cos_l], axis=0)
    sin_all = jnp.concatenate([jnp.zeros((n_ctx, LANES), F32), sin_l], axis=0)
    return cos_all, sin_all


def kernel(x, c, ctx, c_ctx, mod_w, mod_b, norm_mix_g, norm_ffn_g, ev_w_in, ev_w_out, ev_gla_w_a2, ev_gla_b_a2,
           ev_gla_norm_g, ev_na_q_g, ev_na_k_g, ev_na_rpb, od_w_in, od_w_out, od_q_g, od_k_g, od_sinks,
           moe_w_router, moe_b_router, moe_w_gate_up, moe_b_gate_up, moe_w_down, moe_b_down):
    b, s_len, d = x.shape
    n_ctx = ctx.shape[1]
    depth = mod_w.shape[0]
    assert b < MOD_ROWS and n_ctx % TOKEN_TILE == 0 and s_len % TOKEN_TILE == 0
    n_ctx_tiles = n_ctx // TOKEN_TILE
    n_rows = s_len // GRID_W
    kh = min(NA_KH, n_rows)

    xa = jnp.concatenate([ctx, x], axis=1)
    cv = jnp.zeros((MOD_ROWS, d), F32).at[:b].set(c).at[b].set(c_ctx)
    mods_all = _modulation(cv, mod_w, mod_b).reshape(depth, MOD_ROWS, 6, d)
    cos, sin = _rope_tables(n_ctx, s_len)

    for layer in range(depth):
        i = layer // 2
        mods = mods_all[layer]
        gn = norm_mix_g[layer].reshape(1, d)
        if layer % 2 == 0:
            w = ev_w_in[i]
            cuts = np.cumsum([0, GLA_QK, GLA_QK, GLA_V, GLA_V, GLA_RANK, GLA_RANK, NA_W, NA_W, NA_W])
            seg = lambda j: w[:, cuts[j]:cuts[j + 1]]
            pad = jnp.zeros((d, LANES - 2 * GLA_RANK), F32)
            w_bf = jnp.concatenate([seg(0), seg(1), seg(2), seg(3), seg(6), seg(7), seg(8), seg(4), seg(5), pad],
                                   axis=1).astype(BF16)
            a2 = jnp.zeros((LANES, 2 * GLA_QK), F32)
            a2 = a2.at[0:GLA_RANK, 0:GLA_QK].set(ev_gla_w_a2[i, 0])
            a2 = a2.at[GLA_RANK:2 * GLA_RANK, GLA_QK:].set(ev_gla_w_a2[i, 1])
            ba2 = ev_gla_b_a2[i].reshape(1, 2 * GLA_QK)
            qg = jnp.tile(ev_na_q_g[i], NA_HEADS).reshape(1, NA_W)
            kg = jnp.tile(ev_na_k_g[i], NA_HEADS).reshape(1, NA_W)
            gq, gk, gv, gate, la, nq, nk, nv = _proj_even(xa, mods, gn, w_bf, a2, ba2, qg, kg, n_ctx_tiles)
            o2 = _gla(gq, gk, gv, la, n_ctx)
            o_na = _na(nq, nk, nv, _na_bias_table(ev_na_rpb[i], kh), n_ctx)
            mixer_outs = (o2, gate, o_na)
            extra = ev_gla_norm_g[i].reshape(1, GLA_DV)
            w_out = ev_w_out[i].astype(BF16)
        else:
            w_bf = od_w_in[i].astype(BF16)
            qg = jnp.tile(od_q_g[i], SW_HEADS).reshape(1, SW_Q)
            kg = jnp.tile(od_k_g[i], SW_KV_HEADS).reshape(1, SW_KV)
            q, k, v = _proj_odd(xa, mods, gn, w_bf, qg, kg, cos, sin, n_ctx_tiles)
            mixer_outs = (_swa(q, k, v, od_sinks[i], n_ctx),)
            extra = None
            w_out = od_w_out[i].astype(BF16)
        xn, f, idx, wts, rank, cnt = _post(
            layer % 2 == 0, mixer_outs, extra, w_out, xa, mods, norm_ffn_g[layer].reshape(1, d),
            jnp.stack(_split_bf16(moe_w_router[layer])), moe_b_router[layer].reshape(1, N_EXPERTS), n_ctx_tiles)
        xa = _moe(xn, f, idx, wts, rank, cnt, mods, layer, moe_w_gate_up, moe_b_gate_up, moe_w_down, moe_b_down,
                  n_ctx_tiles)
    return xa[:, n_ctx:, :]
```

```python
import functools

import jax
import jax.numpy as jnp
import numpy as np
from jax import lax
from jax.experimental import pallas as pl
from jax.experimental.pallas import tpu as pltpu

F32 = jnp.float32
BF16 = jnp.bfloat16
I32 = jnp.int32
HIGHEST = lax.Precision.HIGHEST

GRID_W = 64
HEAD_DIM = 64
EPS = 1e-6
GLA_HEADS = 4
GLA_DK = 64
GLA_DV = 128
GLA_RANK = 16
GLA_TAU = 16.0
GLA_CHUNK = 64
GLA_SUB = 16
NA_HEADS = 8
NA_KH = 8
NA_KW = 16
SW_HEADS = 16
SW_KV_HEADS = 2
SW_WINDOW = 128
SW_BLOCK = 128
ROPE_BASE = 10000.0
N_EXPERTS = 32
TOP_K = 4
SWIGLU_LIMIT = 7.0
SWIGLU_ALPHA = 1.702

GLA_QK = GLA_HEADS * GLA_DK
GLA_V = GLA_HEADS * GLA_DV
NA_W = NA_HEADS * HEAD_DIM
SW_Q = SW_HEADS * HEAD_DIM
SW_KV = SW_KV_HEADS * HEAD_DIM

LANES = 128
SUBLANES = 8
VMEM_LIMIT_BYTES = 56 * 1024 * 1024
NEG = -1e30

TOKEN_TILE = 256
EXPERT_BLOCK = 512
MOE_CHUNK = 16
MOD_ROWS = 16


def _dot(a, b, **kw):
    return jnp.dot(a, b, preferred_element_type=F32, **kw)


def _dot_nt(a, b):
    return lax.dot_general(a, b, (((1,), (1,)), ((), ())), preferred_element_type=F32)


def _dot_tn(a, b):
    return lax.dot_general(a, b, (((0,), (0,)), ((), ())), preferred_element_type=F32)


def _split_bf16(x):
    hi = x.astype(BF16)
    lo = (x - hi.astype(F32)).astype(BF16)
    return hi, lo


def _params(sem, vmem=VMEM_LIMIT_BYTES):
    return pltpu.CompilerParams(dimension_semantics=sem, vmem_limit_bytes=vmem)


def _mod_kernel(cv_ref, w_ref, b_ref, o_ref):
    cv = cv_ref[...]
    s = cv * jax.nn.sigmoid(cv)
    o_ref[...] = _dot(s, w_ref[...], precision=HIGHEST) + b_ref[...]


def _modulation(cv, mod_w, mod_b):
    depth, d, n = mod_w.shape
    tn = 1536
    return pl.pallas_call(
        _mod_kernel,
        grid=(depth, n // tn),
        in_specs=[
            pl.BlockSpec((MOD_ROWS, d), lambda l, j: (0, 0)),
            pl.BlockSpec((None, d, tn), lambda l, j: (l, 0, j)),
            pl.BlockSpec((None, 1, tn), lambda l, j: (l, 0, j)),
        ],
        out_specs=pl.BlockSpec((None, MOD_ROWS, tn), lambda l, j: (l, 0, j)),
        out_shape=jax.ShapeDtypeStruct((depth, MOD_ROWS, n), F32),
        compiler_params=_params(("arbitrary", "arbitrary")),
        name="modulation",
    )(cv, mod_w, mod_b.reshape(depth, 1, n))


def _norm_mod(x, g_row, m_ref, shift_row, scale_row):
    ms = jnp.mean(x * x, axis=-1, keepdims=True)
    h = x * lax.rsqrt(ms + EPS) * g_row
    return h * (1.0 + m_ref[scale_row:scale_row + 1, :]) + m_ref[shift_row:shift_row + 1, :]


def _head_pair_ones():
    r = lax.broadcasted_iota(I32, (LANES, LANES), 0) // HEAD_DIM
    c = lax.broadcasted_iota(I32, (LANES, LANES), 1) // HEAD_DIM
    return jnp.where(r == c, 1.0, 0.0).astype(BF16)


def _head_rms(y, g_row, ones_blk):
    outs = []
    for j in range(y.shape[1] // LANES):
        s = y[:, j * LANES:(j + 1) * LANES]
        hi, lo = _split_bf16(s * s)
        ms = (_dot(hi, ones_blk) + _dot(lo, ones_blk)) * (1.0 / HEAD_DIM)
        outs.append(s * lax.rsqrt(ms + EPS))
    return jnp.concatenate(outs, axis=1) * g_row


def _mod_row_map(n_ctx_tiles, n_batch):
    return lambda b, t: (jnp.where(t < n_ctx_tiles, n_batch, b), 0, 0)


def _proj_even_kernel(x_ref, m_ref, gn_ref, w_ref, a2_ref, ba2_ref, qg_ref, kg_ref,
                      gq_ref, gk_ref, gv_ref, gate_ref, la_ref, nq_ref, nk_ref, nv_ref):
    h = _norm_mod(x_ref[...], gn_ref[...], m_ref, 0, 1)
    y = _dot(h.astype(BF16), w_ref[...])
    c0 = 0
    gq_ref[...] = y[:, c0:c0 + GLA_QK] * (GLA_DK ** -0.5)
    c0 += GLA_QK
    gk_ref[...] = y[:, c0:c0 + GLA_QK]
    c0 += GLA_QK
    gv_ref[...] = y[:, c0:c0 + GLA_V].astype(BF16)
    c0 += GLA_V
    gate_ref[...] = y[:, c0:c0 + GLA_V].astype(BF16)
    c0 += GLA_V
    ones_blk = _head_pair_ones()
    nq_ref[...] = (_head_rms(y[:, c0:c0 + NA_W], qg_ref[...], ones_blk) * (HEAD_DIM ** -0.5)).astype(BF16)
    c0 += NA_W
    nk_ref[...] = _head_rms(y[:, c0:c0 + NA_W], kg_ref[...], ones_blk).astype(BF16)
    c0 += NA_W
    nv_ref[...] = y[:, c0:c0 + NA_W].astype(BF16)
    c0 += NA_W
    a = y[:, c0:c0 + LANES]
    z = _dot(a, a2_ref[...], precision=HIGHEST) + ba2_ref[...]
    la_ref[...] = (jnp.minimum(z, 0.0) - jnp.log1p(jnp.exp(-jnp.abs(z)))) * (1.0 / GLA_TAU)


def _proj_even(xa, mods, gn, w_bf, a2, ba2, qg, kg, n_ctx_tiles):
    b, tt, d = xa.shape
    tm = TOKEN_TILE
    n = w_bf.shape[1]
    tok = lambda w: pl.BlockSpec((None, tm, w), lambda bi, t: (bi, t, 0))
    full = lambda shape: pl.BlockSpec(shape, lambda bi, t: (0,) * len(shape))
    widths = [(GLA_QK, F32), (GLA_QK, F32), (GLA_V, BF16), (GLA_V, BF16), (2 * GLA_QK, F32),
              (NA_W, BF16), (NA_W, BF16), (NA_W, BF16)]
    return pl.pallas_call(
        _proj_even_kernel,
        grid=(b, tt // tm),
        in_specs=[
            tok(d),
            pl.BlockSpec((None, 6, d), _mod_row_map(n_ctx_tiles, b)),
            full((1, d)), full((d, n)), full((LANES, 2 * GLA_QK)), full((1, 2 * GLA_QK)),
            full((1, NA_W)), full((1, NA_W)),
        ],
        out_specs=[tok(w) for w, _ in widths],
        out_shape=[jax.ShapeDtypeStruct((b, tt, w), dt) for w, dt in widths],
        compiler_params=_params(("parallel", "parallel")),
        name="proj_even",
    )(xa, mods, gn, w_bf, a2, ba2, qg, kg)


def _rope(y, cos, sin_signed, lo_mask):
    w = y.shape[1]
    reps = w // LANES
    cos_t = jnp.concatenate([cos] * reps, axis=1) if reps > 1 else cos
    sin_t = jnp.concatenate([sin_signed] * reps, axis=1) if reps > 1 else sin_signed
    msk = jnp.concatenate([lo_mask] * reps, axis=1) if reps > 1 else lo_mask
    quarter = HEAD_DIM // 4
    swapped = jnp.where(msk, pltpu.roll(y, w - quarter, 1), pltpu.roll(y, quarter, 1))
    return y * cos_t + swapped * sin_t


def _proj_odd_kernel(x_ref, m_ref, gn_ref, w_ref, qg_ref, kg_ref, cos_ref, sin_ref,
                     q_ref, k_ref, v_ref):
    h = _norm_mod(x_ref[...], gn_ref[...], m_ref, 0, 1)
    y = _dot(h.astype(BF16), w_ref[...])
    ones_blk = _head_pair_ones()
    lane = lax.broadcasted_iota(I32, (1, LANES), 1)
    lo_mask = (lane % (HEAD_DIM // 2)) < (HEAD_DIM // 4)
    cos = cos_ref[...]
    sin = sin_ref[...]
    q = _head_rms(y[:, 0:SW_Q], qg_ref[...], ones_blk)
    q_ref[...] = (_rope(q, cos, sin, lo_mask) * (HEAD_DIM ** -0.5)).astype(BF16)
    k = _head_rms(y[:, SW_Q:SW_Q + SW_KV], kg_ref[...], ones_blk)
    k_ref[...] = _rope(k, cos, sin, lo_mask).astype(BF16)
    v_ref[...] = y[:, SW_Q + SW_KV:SW_Q + 2 * SW_KV].astype(BF16)


def _proj_odd(xa, mods, gn, w_bf, qg, kg, cos, sin, n_ctx_tiles):
    b, tt, d = xa.shape
    tm = TOKEN_TILE
    n = w_bf.shape[1]
    tok = lambda w: pl.BlockSpec((None, tm, w), lambda bi, t: (bi, t, 0))
    full = lambda shape: pl.BlockSpec(shape, lambda bi, t: (0,) * len(shape))
    pos = pl.BlockSpec((tm, LANES), lambda bi, t: (t, 0))
    widths = [SW_Q, SW_KV, SW_KV]
    return pl.pallas_call(
        _proj_odd_kernel,
        grid=(b, tt // tm),
        in_specs=[
            tok(d),
            pl.BlockSpec((None, 6, d), _mod_row_map(n_ctx_tiles, b)),
            full((1, d)), full((d, n)), full((1, SW_Q)), full((1, SW_KV)), pos, pos,
        ],
        out_specs=[tok(w) for w in widths],
        out_shape=[jax.ShapeDtypeStruct((b, tt, w), BF16) for w in widths],
        compiler_params=_params(("parallel", "parallel")),
        name="proj_odd",
    )(xa, mods, gn, w_bf, qg, kg, cos, sin)


def _gla_kernel(q_ref, k_ref, v_ref, g_ref, o_ref, st_ref):
    c = GLA_CHUNK
    n_chunks = q_ref.shape[0] // c
    direction = pl.program_id(0)

    @pl.when(pl.program_id(2) == 0)
    def _():
        st_ref[...] = jnp.zeros_like(st_ref)

    sgn = jnp.where(direction == 0, 1, -1)
    r = lax.broadcasted_iota(I32, (2 * c, c), 0)
    j = lax.broadcasted_iota(I32, (2 * c, c), 1)
    i = r % c
    incl = jnp.where(sgn * (i - j) >= 0, 1.0, 0.0)
    blk = jnp.where(sgn * (i // GLA_SUB - j // GLA_SUB) > 0, 1.0, 0.0)
    scan_mat = jnp.where(r >= c, blk, incl).astype(BF16)
    causal = scan_mat[0:c, :].astype(F32)
    row = lax.broadcasted_iota(I32, (c, 1), 0)
    reachable = []
    for a in range(c // GLA_SUB):
        edge = jnp.where(direction == 0, a * GLA_SUB + GLA_SUB - 1, a * GLA_SUB)
        reachable.append(sgn * (edge - row) >= 0)

    for u in range(n_chunks):
        ci = jnp.where(direction == 0, u, n_chunks - 1 - u)
        rows = pl.ds(pl.multiple_of(ci * c, c), c)
        g_hi, g_lo = _split_bf16(g_ref[rows, :])
        cr = _dot(scan_mat, g_hi) + _dot(scan_mat, g_lo)
        cum = cr[0:c, :]
        ref_pt = cr[c:2 * c, :]
        tot = jnp.where(direction == 0, cum[c - 1:c, :], cum[0:1, :])

        q = q_ref[rows, :]
        k = k_ref[rows, :]
        q_state = (q * jnp.exp(cum)).astype(BF16)
        q_sub = (q * jnp.exp(cum - ref_pt)).astype(BF16)
        k_state = (k * jnp.exp(tot - cum)).astype(BF16)
        decay = jnp.exp(tot)

        attn_rows = [[] for _ in range(GLA_HEADS)]
        for a in range(c // GLA_SUB):
            lo = a * GLA_SUB
            ra = ref_pt[lo:lo + 1, :]
            k_sub = (k * jnp.exp(jnp.where(reachable[a], ra - cum, 0.0))).astype(BF16)
            for h in range(GLA_HEADS):
                hs = slice(h * GLA_DK, (h + 1) * GLA_DK)
                attn_rows[h].append(_dot_nt(q_sub[lo:lo + GLA_SUB, hs], k_sub[:, hs]))

        v = v_ref[rows, :]
        for h in range(GLA_HEADS):
            hs = slice(h * GLA_DK, (h + 1) * GLA_DK)
            vs = slice(h * GLA_DV, (h + 1) * GLA_DV)
            attn = (jnp.concatenate(attn_rows[h], axis=0) * causal).astype(BF16)
            st = st_ref[h]
            o_ref[rows, vs] = _dot(attn, v[:, vs]) + _dot_nt(q_state[:, hs], st.astype(BF16))
            st_ref[h] = st * decay[:, hs] + _dot_tn(v[:, vs], k_state[:, hs])


def _gla(gq, gk, gv, la, n_ctx):
    b, tt, _ = gq.shape
    rows = TOKEN_TILE
    nb = tt // rows
    n_ctx_blk = n_ctx // rows

    def blk(d, s):
        bwd = jnp.where(s < n_ctx_blk, n_ctx_blk - 1 - s, nb - 1 + n_ctx_blk - s)
        return jnp.where(d == 0, s, bwd)

    return pl.pallas_call(
        _gla_kernel,
        grid=(2, b, nb),
        in_specs=[
            pl.BlockSpec((None, rows, GLA_QK), lambda d, bi, s: (bi, blk(d, s), 0)),
            pl.BlockSpec((None, rows, GLA_QK), lambda d, bi, s: (bi, blk(d, s), 0)),
            pl.BlockSpec((None, rows, GLA_V), lambda d, bi, s: (bi, blk(d, s), 0)),
            pl.BlockSpec((None, rows, GLA_QK), lambda d, bi, s: (bi, blk(d, s), d)),
        ],
        out_specs=pl.BlockSpec((None, None, rows, GLA_V), lambda d, bi, s: (d, bi, blk(d, s), 0)),
        out_shape=jax.ShapeDtypeStruct((2, b, tt, GLA_V), F32),
        scratch_shapes=[pltpu.VMEM((GLA_HEADS, GLA_DV, GLA_DK), F32)],
        compiler_params=_params(("arbitrary", "arbitrary", "arbitrary")),
        name="gla_scan",
    )(gq, gk, gv, la)


def _na_bias_table(rpb, kh):
    n_heads = rpb.shape[0]
    qc = np.arange(GRID_W)[:, None]
    kc = np.arange(GRID_W)[None, :]
    win0 = np.clip(qc - NA_KW // 2, 0, GRID_W - NA_KW)
    valid = (kc >= win0) & (kc < win0 + NA_KW)
    side = GRID_W - NA_KW
    rp = jnp.pad(rpb, ((0, 0), (0, 0), (side, side)))
    base = jnp.stack([rp[:, :, GRID_W - 1 - c:2 * GRID_W - 1 - c] for c in range(GRID_W)], axis=2)
    base = jnp.where(valid[None, None], base, NEG)
    tab = jnp.stack([base[:, NA_KH - 1 - a:NA_KH - 1 - a + kh] for a in range(kh)], axis=1)
    tab = tab.reshape(n_heads // 2, 2, kh, kh, GRID_W, GRID_W).transpose(0, 2, 1, 4, 3, 5)
    return tab.reshape(n_heads // 2, kh, 2 * GRID_W, kh * GRID_W).astype(F32)


def _na_kernel(q_ref, k_ref, v_ref, bias_ref, o_ref, *, n_ctx, n_ctx_blk, rows_per_blk, n_rows, kh):
    t = pl.program_id(2)
    lane = lax.broadcasted_iota(I32, (1, LANES), 1)
    first = lane < HEAD_DIM
    kc = k_ref[0:n_ctx, :]
    vc = v_ref[0:n_ctx, :]

    def stack_heads(q):
        z = jnp.zeros_like(q)
        return jnp.concatenate([jnp.where(first, q, z), jnp.where(first, z, q)], axis=0)

    def unstack_heads(o):
        n = o.shape[0] // 2
        return jnp.where(first, o[0:n], o[n:2 * n])

    @pl.when(t < n_ctx_blk)
    def _():
        s = _dot_nt(stack_heads(q_ref[...]), kc)
        m = jnp.max(s, axis=-1, keepdims=True)
        p = jnp.exp(s - m)
        den = jnp.sum(p, axis=-1, keepdims=True)
        o_ref[...] = unstack_heads(_dot(p.astype(BF16), vc) / den).astype(o_ref.dtype)

    @pl.when(t >= n_ctx_blk)
    def _():
        for i in range(rows_per_blk):
            r = (t - n_ctx_blk) * rows_per_blk + i
            row0 = jnp.clip(r - kh // 2, 0, n_rows - kh)
            cls = r - row0
            start = pl.multiple_of(n_ctx + row0 * GRID_W, GRID_W)
            kb = k_ref[pl.ds(start, kh * GRID_W), :]
            vb = v_ref[pl.ds(start, kh * GRID_W), :]
            qs = stack_heads(q_ref[i * GRID_W:(i + 1) * GRID_W, :])
            s = _dot_nt(qs, kb) + bias_ref[cls]
            sc = _dot_nt(qs, kc)
            m = jnp.maximum(jnp.max(s, axis=-1, keepdims=True), jnp.max(sc, axis=-1, keepdims=True))
            p = jnp.exp(s - m)
            pc = jnp.exp(sc - m)
            den = jnp.sum(p, axis=-1, keepdims=True) + jnp.sum(pc, axis=-1, keepdims=True)
            o = (_dot(p.astype(BF16), vb) + _dot(pc.astype(BF16), vc)) / den
            o_ref[i * GRID_W:(i + 1) * GRID_W, :] = unstack_heads(o).astype(o_ref.dtype)


def _na(nq, nk, nv, bias_tab, n_ctx):
    b, tt, _ = nq.shape
    tq = TOKEN_TILE
    n_rows = (tt - n_ctx) // GRID_W
    kh = bias_tab.shape[1]
    kern = functools.partial(_na_kernel, n_ctx=n_ctx, n_ctx_blk=n_ctx // tq, rows_per_blk=tq // GRID_W,
                             n_rows=n_rows, kh=kh)
    return pl.pallas_call(
        kern,
        grid=(b, NA_HEADS // 2, tt // tq),
        in_specs=[
            pl.BlockSpec((None, tq, LANES), lambda bi, p, t: (bi, t, p)),
            pl.BlockSpec((None, tt, LANES), lambda bi, p, t: (bi, 0, p)),
            pl.BlockSpec((None, tt, LANES), lambda bi, p, t: (bi, 0, p)),
            pl.BlockSpec((None, kh, 2 * GRID_W, kh * GRID_W), lambda bi, p, t: (p, 0, 0, 0)),
        ],
        out_specs=pl.BlockSpec((None, tq, LANES), lambda bi, p, t: (bi, t, p)),
        out_shape=jax.ShapeDtypeStruct((b, tt, NA_W), BF16),
        compiler_params=_params(("parallel", "parallel", "arbitrary")),
        name="na_attn",
    )(nq, nk, nv, bias_tab)


def _swa_kernel(sink_ref, q_ref, k_ref, v_ref, o_ref, *, n_ctx, n_lat_blk):
    blk = SW_BLOCK
    n = pl.program_id(1)
    n_ctx_blk = n_ctx // blk
    jb = n - n_ctx_blk
    sb = jnp.clip(jb - 1, 0, n_lat_blk - 3)
    start = pl.multiple_of(n_ctx + sb * blk, blk)
    keys = jnp.concatenate([k_ref[pl.ds(start, 3 * blk), :], k_ref[0:n_ctx, :]], axis=0)
    vals = jnp.concatenate([v_ref[pl.ds(start, 3 * blk), :], v_ref[0:n_ctx, :]], axis=0)
    nk = 3 * blk + n_ctx
    iq = lax.broadcasted_iota(I32, (blk, nk), 0)
    ik = lax.broadcasted_iota(I32, (blk, nk), 1)
    offset = jnp.where(n < n_ctx_blk, 4 * SW_WINDOW, (jb - sb) * blk)
    dist = jnp.where(ik >= 3 * blk, 0, iq - ik + offset)
    bias = jnp.where(jnp.abs(dist) <= SW_WINDOW, 0.0, NEG)
    q = q_ref[...]
    group = SW_HEADS // SW_KV_HEADS
    bias_g = jnp.concatenate([bias] * group, axis=0)
    for g in range(SW_KV_HEADS):
        gs = slice(g * HEAD_DIM, (g + 1) * HEAD_DIM)
        heads = range(g * group, (g + 1) * group)
        qg = jnp.concatenate([q[:, h * HEAD_DIM:(h + 1) * HEAD_DIM] for h in heads], axis=0)
        sink = jnp.concatenate([jnp.full((blk, 1), sink_ref[h], F32) for h in heads], axis=0)
        s = _dot_nt(qg, keys[:, gs]) + bias_g
        m = jnp.maximum(jnp.max(s, axis=-1, keepdims=True), sink)
        p = jnp.exp(s - m)
        den = jnp.sum(p, axis=-1, keepdims=True) + jnp.exp(sink - m)
        o = _dot(p.astype(BF16), vals[:, gs]) / den
        for pr in range(group // 2):
            pair = [o[(2 * pr + e) * blk:(2 * pr + e + 1) * blk, :] for e in range(2)]
            col = (g * group // 2 + pr) * LANES
            o_ref[:, col:col + LANES] = jnp.concatenate(pair, axis=1).astype(o_ref.dtype)


def _swa(q, k, v, sinks, n_ctx):
    b, tt, _ = q.shape
    blk = SW_BLOCK
    kern = functools.partial(_swa_kernel, n_ctx=n_ctx, n_lat_blk=(tt - n_ctx) // blk)
    return pl.pallas_call(
        kern,
        grid=(b, tt // blk),
        in_specs=[
            pl.BlockSpec(memory_space=pltpu.SMEM),
            pl.BlockSpec((None, blk, SW_Q), lambda bi, n: (bi, n, 0)),
            pl.BlockSpec((None, tt, SW_KV), lambda bi, n: (bi, 0, 0)),
            pl.BlockSpec((None, tt, SW_KV), lambda bi, n: (bi, 0, 0)),
        ],
        out_specs=pl.BlockSpec((None, blk, SW_Q), lambda bi, n: (bi, n, 0)),
        out_shape=jax.ShapeDtypeStruct((b, tt, SW_Q), BF16),
        compiler_params=_params(("parallel", "arbitrary")),
        name="swa_attn",
    )(sinks, q, k, v)


def _route(f, wr_ref, br_ref, before_ref, run_ref, idx_ref, wt_ref, rank_ref, cnt_ref, base_ref):
    tm = f.shape[0]
    f_hi, f_lo = _split_bf16(f)
    logits = _dot(f_hi, wr_ref[0]) + (_dot(f_lo, wr_ref[0]) + _dot(f_hi, wr_ref[1])) + br_ref[...]
    e_iota = lax.broadcasted_iota(I32, (tm, N_EXPERTS), 1).astype(F32)
    lane8 = lax.broadcasted_iota(I32, (tm, 8), 1)
    vals, hots = [], []
    idx_out = jnp.zeros((tm, 8), I32)
    for k in range(TOP_K):
        mx = jnp.max(logits, axis=-1, keepdims=True)
        am = jnp.min(jnp.where(logits == mx, e_iota, float(N_EXPERTS)), axis=-1, keepdims=True)
        hot = e_iota == am
        vals.append(mx)
        hots.append(hot)
        idx_out = jnp.where(lane8 == k, am.astype(I32), idx_out)
        logits = jnp.where(hot, -jnp.inf, logits)
    exps = [jnp.exp(v - vals[0]) for v in vals]
    den = exps[0] + exps[1] + exps[2] + exps[3]
    multihot = jnp.zeros((tm, N_EXPERTS), F32)
    wt_out = jnp.zeros((tm, 8), F32)
    for k in range(TOP_K):
        multihot = multihot + jnp.where(hots[k], 1.0, 0.0)
        wt_out = jnp.where(lane8 == k, exps[k] / den, wt_out)
    base = _dot(before_ref[...], multihot.astype(BF16)) + run_ref[...]
    rank_out = jnp.zeros((tm, 8), I32)
    for k in range(TOP_K):
        rk = jnp.sum(jnp.where(hots[k], base, 0.0), axis=-1, keepdims=True)
        rank_out = jnp.where(lane8 == k, rk.astype(I32), rank_out)
    base_ref[...] = run_ref[...]
    run_ref[...] = run_ref[...] + jnp.sum(multihot, axis=0, keepdims=True)
    idx_ref[...] = idx_out
    wt_ref[...] = wt_out
    rank_ref[...] = rank_out
    cnt_ref[...] = run_ref[...]


def _post_common(mix, x_ref, m_ref, fg_ref, wr_ref, br_ref, before_ref, xn_ref, f_ref, idx_ref, wt_ref, rank_ref,
                 cnt_ref, base_ref, run_ref):
    @pl.when((pl.program_id(0) == 0) & (pl.program_id(1) == 0))
    def _():
        run_ref[...] = jnp.zeros_like(run_ref)

    xn = x_ref[...] + m_ref[2:3, :] * mix
    xn_ref[...] = xn
    f = _norm_mod(xn, fg_ref[...], m_ref, 3, 4)
    f_ref[...] = f.astype(f_ref.dtype)
    _route(f, wr_ref, br_ref, before_ref, run_ref, idx_ref, wt_ref, rank_ref, cnt_ref, base_ref)


def _post_even_kernel(of_ref, ob_ref, gate_ref, na_ref, gng_ref, w_ref, x_ref, m_ref, fg_ref, wr_ref, br_ref,
                      before_ref, xn_ref, f_ref, idx_ref, wt_ref, rank_ref, cnt_ref, base_ref, run_ref):
    o = of_ref[...] + ob_ref[...]
    gate = gate_ref[...].astype(F32)
    gn = gng_ref[...]
    parts = []
    for h in range(GLA_HEADS):
        oh = o[:, h * GLA_DV:(h + 1) * GLA_DV]
        ms = jnp.mean(oh * oh, axis=-1, keepdims=True)
        parts.append(oh * lax.rsqrt(ms + EPS) * gn)
    gla = jnp.concatenate(parts, axis=1) * (gate * jax.nn.sigmoid(gate))
    mix = _dot(gla.astype(BF16), w_ref[0:GLA_V, :]) + _dot(na_ref[...], w_ref[GLA_V:GLA_V + NA_W, :])
    _post_common(mix, x_ref, m_ref, fg_ref, wr_ref, br_ref, before_ref, xn_ref, f_ref, idx_ref, wt_ref, rank_ref,
                 cnt_ref, base_ref, run_ref)


def _post_odd_kernel(o_ref, w_ref, x_ref, m_ref, fg_ref, wr_ref, br_ref, before_ref,
                     xn_ref, f_ref, idx_ref, wt_ref, rank_ref, cnt_ref, base_ref, run_ref):
    mix = _dot(o_ref[...], w_ref[...])
    _post_common(mix, x_ref, m_ref, fg_ref, wr_ref, br_ref, before_ref, xn_ref, f_ref, idx_ref, wt_ref, rank_ref,
                 cnt_ref, base_ref, run_ref)


def _post(even, mixer_outs, extra, w_bf, xa, mods, fg, wr, br, n_ctx_tiles):
    b, tt, d = xa.shape
    tm = TOKEN_TILE
    tok = lambda w: pl.BlockSpec((None, tm, w), lambda bi, t: (bi, t, 0))
    full = lambda shape: pl.BlockSpec(shape, lambda bi, t: (0,) * len(shape))
    if even:
        o2, gate, na = mixer_outs
        lead_specs = [
            pl.BlockSpec((None, None, tm, GLA_V), lambda bi, t: (0, bi, t, 0)),
            pl.BlockSpec((None, None, tm, GLA_V), lambda bi, t: (1, bi, t, 0)),
            tok(GLA_V), tok(NA_W), full((1, GLA_DV)),
        ]
        lead_args = [o2, o2, gate, na, extra]
        kern = _post_even_kernel
    else:
        (o,) = mixer_outs
        lead_specs = [tok(SW_Q)]
        lead_args = [o]
        kern = _post_odd_kernel
    nt = tt // tm
    small = lambda: pl.BlockSpec((None, tm, 8), lambda bi, t: (bi, t, 0))
    before = jnp.asarray(np.tril(np.ones((tm, tm), np.float32), -1), BF16)
    return pl.pallas_call(
        kern,
        grid=(b, tt // tm),
        in_specs=lead_specs + [
            full(w_bf.shape), tok(d),
            pl.BlockSpec((None, 6, d), _mod_row_map(n_ctx_tiles, b)),
            full((1, d)), full((2, d, N_EXPERTS)), full((1, N_EXPERTS)), full((tm, tm)),
        ],
        out_specs=[tok(d), tok(d), small(), small(), small(), full((1, N_EXPERTS)),
                   pl.BlockSpec((None, 1, N_EXPERTS), lambda bi, t: (bi * nt + t, 0, 0))],
        out_shape=[
            jax.ShapeDtypeStruct((b, tt, d), F32), jax.ShapeDtypeStruct((b, tt, d), BF16),
            jax.ShapeDtypeStruct((b, tt, 8), I32), jax.ShapeDtypeStruct((b, tt, 8), F32),
            jax.ShapeDtypeStruct((b, tt, 8), I32), jax.ShapeDtypeStruct((1, N_EXPERTS), F32),
            jax.ShapeDtypeStruct((b * nt, 1, N_EXPERTS), F32),
        ],
        scratch_shapes=[pltpu.VMEM((1, N_EXPERTS), F32)],
        compiler_params=_params(("arbitrary", "arbitrary")),
        name="post_even" if even else "post_odd",
    )(*lead_args, w_bf, xa, mods, fg, wr, br, before)


def _stage_rows(tm):
    return tm * TOP_K + N_EXPERTS * (MOE_CHUNK + SUBLANES)


def _for_tile_chunks(seg_ref, nch_ref, g, fn):
    def per_expert(e, ptr):
        s0 = seg_ref[g * N_EXPERTS + e]

        def per_chunk(c, p):
            fn(pl.multiple_of(s0 + c * MOE_CHUNK, SUBLANES), pl.multiple_of(p, MOE_CHUNK))
            return p + MOE_CHUNK

        return lax.fori_loop(0, nch_ref[g * N_EXPERTS + e], per_chunk, ptr)

    return lax.fori_loop(0, N_EXPERTS, per_expert, 0)


def _tile_chunk_count(nch_ref, g):
    return lax.fori_loop(0, N_EXPERTS, lambda e, n: n + nch_ref[g * N_EXPERTS + e], 0)


def _dispatch_kernel(seg_ref, nch_ref, lead_ref, cnt_ref, lo_ref, hi_ref, colt_ref, f_ref, xs_ref,
                     stage, carry, zeros_ref, sems, zsem):
    g = pl.program_id(0)
    n_tiles = pl.num_programs(0)
    ch = MOE_CHUNK
    buf = g % 2

    def chunk_copy(which, slot_row, stage_row):
        return pltpu.make_async_copy(stage.at[which, pl.ds(stage_row, ch)], xs_ref.at[pl.ds(slot_row, ch)],
                                     sems.at[which])

    def wait_tile(tile, which):
        def one(i, c):
            chunk_copy(which, 0, 0).wait()
            return c
        lax.fori_loop(0, _tile_chunk_count(nch_ref, tile), one, 0)

    @pl.when(g == 0)
    def _():
        carry[...] = jnp.zeros_like(carry)
        zeros_ref[...] = jnp.zeros_like(zeros_ref)

        def pad_pass(do):
            def per_expert(e, c):
                lo = lo_ref[e]
                hi = hi_ref[e]
                n_full = (hi - lo) // ch

                def full(j, c2):
                    do(pltpu.make_async_copy(
                        zeros_ref, xs_ref.at[pl.ds(pl.multiple_of(hi - (j + 1) * ch, SUBLANES), ch)], zsem))
                    return c2

                lax.fori_loop(0, n_full, full, 0)

                @pl.when((hi - lo) - n_full * ch > 0)
                def _():
                    do(pltpu.make_async_copy(zeros_ref.at[pl.ds(0, SUBLANES)],
                                             xs_ref.at[pl.ds(pl.multiple_of(lo, SUBLANES), SUBLANES)], zsem))
                return c

            lax.fori_loop(0, lo_ref.shape[0], per_expert, 0)

        pad_pass(lambda cp: cp.start())
        pad_pass(lambda cp: cp.wait())

    colt = colt_ref[...]
    rs = stage.shape[1]
    r_iota = lax.broadcasted_iota(I32, (rs, colt.shape[1]), 0)
    hit = jnp.zeros(r_iota.shape, F32)
    for k in range(TOP_K):
        hit = jnp.where(r_iota == colt[k:k + 1, :], 1.0, hit)
    stage[buf] = _dot(hit.astype(BF16), f_ref[...])

    def splice(e, ptr):
        n = nch_ref[g * N_EXPERTS + e]

        @pl.when(n > 0)
        def _():
            head = pl.ds(pl.multiple_of(ptr, SUBLANES), SUBLANES)
            stage[buf, head, :] = stage[buf, head, :] + carry[e]
            end = lead_ref[g * N_EXPERTS + e] + cnt_ref[g * N_EXPERTS + e]
            new_lead = end % SUBLANES
            tail = pl.ds(pl.multiple_of(ptr + end - new_lead, SUBLANES), SUBLANES)
            carry[e] = jnp.where(new_lead > 0, stage[buf, tail, :], 0.0)

        return ptr + n * ch

    lax.fori_loop(0, N_EXPERTS, splice, 0)

    @pl.when(g > 0)
    def _():
        wait_tile(g - 1, 1 - buf)

    _for_tile_chunks(seg_ref, nch_ref, g, lambda slot_row, stage_row: chunk_copy(buf, slot_row, stage_row).start())

    @pl.when(g == n_tiles - 1)
    def _():
        wait_tile(g, buf)


def _dispatch(seg, nch, lead, cnt, pad_lo, pad_hi, colt, f2, n_slots):
    t, d = f2.shape
    tm = TOKEN_TILE
    rs = _stage_rows(tm)
    return pl.pallas_call(
        _dispatch_kernel,
        grid_spec=pltpu.PrefetchScalarGridSpec(
            num_scalar_prefetch=6,
            grid=(t // tm,),
            in_specs=[
                pl.BlockSpec((None, 8, tm), lambda g, *_: (g, 0, 0)),
                pl.BlockSpec((tm, d), lambda g, *_: (g, 0)),
            ],
            out_specs=pl.BlockSpec(memory_space=pl.ANY),
            scratch_shapes=[pltpu.VMEM((2, rs, d), F32), pltpu.VMEM((N_EXPERTS, SUBLANES, d), F32),
                            pltpu.VMEM((MOE_CHUNK, d), F32),
                            pltpu.SemaphoreType.DMA((2,)), pltpu.SemaphoreType.DMA(())],
        ),
        out_shape=jax.ShapeDtypeStruct((n_slots, d), F32),
        compiler_params=_params(("arbitrary",)),
        name="moe_dispatch",
    )(seg, nch, lead, cnt, pad_lo, pad_hi, colt, f2)


def _expert_kernel(be_ref, nu_ref, xs_ref, wgu_ref, bgu_ref, wd_ref, bd_ref, ys_ref, wgu_bf, wd_bf):
    i = pl.program_id(0)
    prev = be_ref[jnp.maximum(i - 1, 0)]
    fresh = (i == 0) | (be_ref[i] != prev)

    @pl.when(fresh & (i < nu_ref[0]))
    def _():
        wgu_bf[...] = wgu_ref[...].astype(BF16)
        wd_bf[...] = wd_ref[...].astype(BF16)

    @pl.when(i < nu_ref[0])
    def _():
        x = xs_ref[...].astype(BF16)
        de = wd_bf.shape[0]
        half = de // 2
        acc = None
        for c in range(2):
            glu = _dot(x, wgu_bf[:, c * half:(c + 1) * half]) + bgu_ref[:, c * half:(c + 1) * half]
            lin = _dot(x, wgu_bf[:, de + c * half:de + (c + 1) * half]) + bgu_ref[:, de + c * half:de + (c + 1) * half]
            glu = jnp.minimum(glu, SWIGLU_LIMIT)
            lin = jnp.clip(lin, -SWIGLU_LIMIT, SWIGLU_LIMIT)
            act = (glu * jax.nn.sigmoid(SWIGLU_ALPHA * glu) * (lin + 1.0)).astype(BF16)
            part = _dot(act, wd_bf[c * half:(c + 1) * half, :])
            acc = part if acc is None else acc + part
        ys_ref[...] = acc + bd_ref[...]

    @pl.when(i >= nu_ref[0])
    def _():
        ys_ref[...] = jnp.zeros_like(ys_ref)


def _experts(block_e, n_used, xs, layer, wgu, bgu, wd, bd):
    n_slots, d = xs.shape
    bm = EXPERT_BLOCK
    depth, ne, _, de2 = wgu.shape
    de = wd.shape[2]
    row_blk = lambda i, be, nu: (jnp.minimum(i, nu[0] - 1), 0)
    exp_blk = lambda i, be, nu: (layer, be[i], 0, 0)
    return pl.pallas_call(
        _expert_kernel,
        grid_spec=pltpu.PrefetchScalarGridSpec(
            num_scalar_prefetch=2,
            grid=(n_slots // bm,),
            in_specs=[
                pl.BlockSpec((bm, d), row_blk),
                pl.BlockSpec((None, None, d, de2), exp_blk),
                pl.BlockSpec((None, None, 1, de2), exp_blk),
                pl.BlockSpec((None, None, de, d), exp_blk),
                pl.BlockSpec((None, None, 1, d), exp_blk),
            ],
            out_specs=pl.BlockSpec((bm, d), lambda i, be, nu: (i, 0)),
            scratch_shapes=[pltpu.VMEM((d, de2), BF16), pltpu.VMEM((de, d), BF16)],
        ),
        out_shape=jax.ShapeDtypeStruct((n_slots, d), F32),
        compiler_params=_params(("arbitrary",)),
        name="moe_experts",
    )(block_e, n_used, xs, wgu, bgu.reshape(depth, ne, 1, de2), wd, bd.reshape(depth, ne, 1, d))


def _combine_kernel(seg_ref, nch_ref, col_ref, wt_ref, x_ref, m_ref, ys_ref, o_ref, stage, sems):
    g = pl.program_id(0)
    n_tiles = pl.num_programs(0)
    ch = MOE_CHUNK
    buf = g % 2

    def chunk_copy(which, slot_row, stage_row):
        return pltpu.make_async_copy(ys_ref.at[pl.ds(slot_row, ch)], stage.at[which, pl.ds(stage_row, ch)],
                                     sems.at[which])

    def fetch_tile(tile, which):
        _for_tile_chunks(seg_ref, nch_ref, tile,
                         lambda slot_row, stage_row: chunk_copy(which, slot_row, stage_row).start())

    @pl.when(g == 0)
    def _():
        stage[...] = jnp.zeros_like(stage)
        fetch_tile(0, 0)

    @pl.when(g + 1 < n_tiles)
    def _():
        fetch_tile(g + 1, 1 - buf)

    def one(i, c):
        chunk_copy(buf, 0, 0).wait()
        return c

    lax.fori_loop(0, _tile_chunk_count(nch_ref, g), one, 0)

    col = col_ref[...]
    wt = wt_ref[...]
    tm = col.shape[0]
    group = 2 * LANES
    y = jnp.zeros(x_ref.shape, F32)
    for j in range(stage.shape[1] // group):
        c_iota = lax.broadcasted_iota(I32, (tm, group), 1) + j * group
        w = jnp.zeros((tm, group), F32)
        for k in range(TOP_K):
            w = jnp.where(c_iota == col[:, k:k + 1], wt[:, k:k + 1], w)
        w_hi, w_lo = _split_bf16(w)
        rows = stage[buf, j * group:(j + 1) * group, :].astype(BF16)
        y = y + (_dot(w_hi, rows) + _dot(w_lo, rows))
    o_ref[...] = x_ref[...] + m_ref[5:6, :] * y


def _combine(seg, nch, col, wts, xn, mods, ys, n_ctx_tiles):
    b, tt, d = xn.shape
    tm = TOKEN_TILE
    nt = tt // tm
    rs = _stage_rows(tm)
    assert rs % (2 * LANES) == 0
    return pl.pallas_call(
        _combine_kernel,
        grid_spec=pltpu.PrefetchScalarGridSpec(
            num_scalar_prefetch=2,
            grid=(b * nt,),
            in_specs=[
                pl.BlockSpec((tm, 8), lambda g, *_: (g, 0)),
                pl.BlockSpec((tm, 8), lambda g, *_: (g, 0)),
                pl.BlockSpec((tm, d), lambda g, *_: (g, 0)),
                pl.BlockSpec((None, 6, d), lambda g, *_: (jnp.where(g % nt < n_ctx_tiles, b, g // nt), 0, 0)),
                pl.BlockSpec(memory_space=pl.ANY),
            ],
            out_specs=pl.BlockSpec((tm, d), lambda g, *_: (g, 0)),
            scratch_shapes=[pltpu.VMEM((2, rs, d), F32), pltpu.SemaphoreType.DMA((2,))],
        ),
        out_shape=jax.ShapeDtypeStruct((b * tt, d), F32),
        compiler_params=_params(("arbitrary",)),
        name="moe_combine",
    )(seg, nch, col, wts.reshape(b * tt, 8), xn.reshape(b * tt, d), mods, ys).reshape(b, tt, d)


def _moe(xn, f, idx, wts, rank, cnt, tile_base, mods, layer, wgu, bgu, wd, bd, n_ctx_tiles):
    b, tt, d = xn.shape
    t = b * tt
    bm = EXPERT_BLOCK
    tm = TOKEN_TILE
    ch = MOE_CHUNK
    n_tiles = t // tm
    n_blocks = (t * TOP_K + N_EXPERTS * ch + bm - 1) // bm + N_EXPERTS
    e_ids = np.arange(N_EXPERTS)
    lower = e_ids[None, :] < e_ids[:, None]
    counts = cnt[0].astype(I32)
    padded = (counts + ch + bm - 1) // bm * bm
    pstart = jnp.sum(jnp.where(lower, padded[None, :], 0), axis=1)
    pend = pstart + padded
    base = tile_base.reshape(n_tiles, N_EXPERTS).astype(I32)
    tile_cnt = jnp.concatenate([base[1:], counts[None, :]], axis=0) - base
    first = pstart[None, :] + base
    seg = first // SUBLANES * SUBLANES
    lead = first - seg
    nch = jnp.where(tile_cnt > 0, (lead + tile_cnt + ch - 1) // ch, 0)
    off = jnp.sum(jnp.where(lower[None], nch[:, None, :], 0), axis=2) * ch
    idx4 = idx[..., :TOP_K].reshape(n_tiles, tm, TOP_K)
    rank4 = rank[..., :TOP_K].reshape(n_tiles, tm, TOP_K)
    delta = (off + lead - base)[:, None, None, :]
    col = rank4 + jnp.sum(jnp.where(idx4[..., None] == e_ids, delta, 0), axis=-1)
    col = jnp.concatenate([col, jnp.full((n_tiles, tm, 8 - TOP_K), -1, I32)], axis=-1)
    colt = col.transpose(0, 2, 1)
    n_used = (pend[N_EXPERTS - 1] // bm).reshape(1).astype(I32)
    starts = np.arange(n_blocks, dtype=np.int32)[:, None] * bm
    block_e = jnp.minimum(jnp.sum((pend[None, :] <= starts).astype(I32), axis=1), N_EXPERTS - 1)
    pad_lo = jnp.concatenate([(pstart + counts + SUBLANES - 1) // SUBLANES * SUBLANES, pend[N_EXPERTS - 1:]])
    pad_hi = jnp.concatenate([pend, jnp.full((1,), n_blocks * bm, I32)])
    seg_flat = seg.reshape(-1)
    nch_flat = nch.reshape(-1)
    xs = _dispatch(seg_flat, nch_flat, lead.reshape(-1), tile_cnt.reshape(-1), pad_lo, pad_hi, colt,
                   f.reshape(t, d), n_blocks * bm)
    ys = _experts(block_e, n_used, xs, layer, wgu, bgu, wd, bd)
    return _combine(seg_flat, nch_flat, col.reshape(t, 8), wts, xn, mods, ys, n_ctx_tiles)


def _rope_tables(n_ctx, s_len):
    quarter = HEAD_DIM // 4
    t = np.arange(s_len)
    pos = np.stack([t // GRID_W, t % GRID_W], axis=0).astype(np.float32)
    inv_freq = ROPE_BASE ** (-jnp.arange(quarter, dtype=F32) / quarter)
    ang = jnp.asarray(pos)[:, :, None] * inv_freq
    cos, sin = jnp.cos(ang), jnp.sin(ang)
    cos_h = jnp.concatenate([cos[0], cos[0], cos[1], cos[1]], axis=-1)
    sin_h = jnp.concatenate([-sin[0], sin[0], -sin[1], sin[1]], axis=-1)
    reps = LANES // HEAD_DIM
    cos_l = jnp.tile(cos_h, (1, reps))
    sin_l = jnp.tile(sin_h, (1, reps))
    cos_all = jnp.concatenate([jnp.ones((n_ctx, LANES), F32), cos_l], axis=0)
    sin_all = jnp.concatenate([jnp.zeros((n_ctx, LANES), F32), sin_l], axis=0)
    return cos_all, sin_all


def kernel(x, c, ctx, c_ctx, mod_w, mod_b, norm_mix_g, norm_ffn_g, ev_w_in, ev_w_out, ev_gla_w_a2, ev_gla_b_a2,
           ev_gla_norm_g, ev_na_q_g, ev_na_k_g, ev_na_rpb, od_w_in, od_w_out, od_q_g, od_k_g, od_sinks,
           moe_w_router, moe_b_router, moe_w_gate_up, moe_b_gate_up, moe_w_down, moe_b_down):
    b, s_len, d = x.shape
    n_ctx = ctx.shape[1]
    depth = mod_w.shape[0]
    assert b < MOD_ROWS and n_ctx % TOKEN_TILE == 0 and s_len % TOKEN_TILE == 0
    n_ctx_tiles = n_ctx // TOKEN_TILE
    n_rows = s_len // GRID_W
    kh = min(NA_KH, n_rows)

    xa = jnp.concatenate([ctx, x], axis=1)
    cv = jnp.zeros((MOD_ROWS, d), F32).at[:b].set(c).at[b].set(c_ctx)
    mods_all = _modulation(cv, mod_w, mod_b).reshape(depth, MOD_ROWS, 6, d)
    cos, sin = _rope_tables(n_ctx, s_len)

    for layer in range(depth):
        i = layer // 2
        mods = mods_all[layer]
        gn = norm_mix_g[layer].reshape(1, d)
        if layer % 2 == 0:
            w = ev_w_in[i]
            cuts = np.cumsum([0, GLA_QK, GLA_QK, GLA_V, GLA_V, GLA_RANK, GLA_RANK, NA_W, NA_W, NA_W])
            seg = lambda j: w[:, cuts[j]:cuts[j + 1]]
            pad = jnp.zeros((d, LANES - 2 * GLA_RANK), F32)
            w_bf = jnp.concatenate([seg(0), seg(1), seg(2), seg(3), seg(6), seg(7), seg(8), seg(4), seg(5), pad],
                                   axis=1).astype(BF16)
            a2 = jnp.zeros((LANES, 2 * GLA_QK), F32)
            a2 = a2.at[0:GLA_RANK, 0:GLA_QK].set(ev_gla_w_a2[i, 0])
            a2 = a2.at[GLA_RANK:2 * GLA_RANK, GLA_QK:].set(ev_gla_w_a2[i, 1])
            ba2 = ev_gla_b_a2[i].reshape(1, 2 * GLA_QK)
            qg = jnp.tile(ev_na_q_g[i], NA_HEADS).reshape(1, NA_W)
            kg = jnp.tile(ev_na_k_g[i], NA_HEADS).reshape(1, NA_W)
            gq, gk, gv, gate, la, nq, nk, nv = _proj_even(xa, mods, gn, w_bf, a2, ba2, qg, kg, n_ctx_tiles)
            o2 = _gla(gq, gk, gv, la, n_ctx)
            o_na = _na(nq, nk, nv, _na_bias_table(ev_na_rpb[i], kh), n_ctx)
            mixer_outs = (o2, gate, o_na)
            extra = ev_gla_norm_g[i].reshape(1, GLA_DV)
            w_out = ev_w_out[i].astype(BF16)
        else:
            w_bf = od_w_in[i].astype(BF16)
            qg = jnp.tile(od_q_g[i], SW_HEADS).reshape(1, SW_Q)
            kg = jnp.tile(od_k_g[i], SW_KV_HEADS).reshape(1, SW_KV)
            q, k, v = _proj_odd(xa, mods, gn, w_bf, qg, kg, cos, sin, n_ctx_tiles)
            mixer_outs = (_swa(q, k, v, od_sinks[i], n_ctx),)
            extra = None
            w_out = od_w_out[i].astype(BF16)
        xn, f, idx, wts, rank, cnt, tile_base = _post(
            layer % 2 == 0, mixer_outs, extra, w_out, xa, mods, norm_ffn_g[layer].reshape(1, d),
            jnp.stack(_split_bf16(moe_w_router[layer])), moe_b_router[layer].reshape(1, N_EXPERTS), n_ctx_tiles)
        xa = _moe(xn, f, idx, wts, rank, cnt, tile_base, mods, layer, moe_w_gate_up, moe_b_gate_up, moe_w_down, moe_b_down,
                  n_ctx_tiles)
    return xa[:, n_ctx:, :]
```

```python
import functools

import jax
import jax.numpy as jnp
import numpy as np
from jax import lax
from jax.experimental import pallas as pl
from jax.experimental.pallas import tpu as pltpu

F32 = jnp.float32
BF16 = jnp.bfloat16
I32 = jnp.int32
HIGHEST = lax.Precision.HIGHEST

GRID_W = 64
HEAD_DIM = 64
EPS = 1e-6
GLA_HEADS = 4
GLA_DK = 64
GLA_DV = 128
GLA_RANK = 16
GLA_TAU = 16.0
GLA_CHUNK = 64
GLA_SUB = 16
NA_HEADS = 8
NA_KH = 8
NA_KW = 16
SW_HEADS = 16
SW_KV_HEADS = 2
SW_WINDOW = 128
SW_BLOCK = 128
ROPE_BASE = 10000.0
N_EXPERTS = 32
TOP_K = 4
SWIGLU_LIMIT = 7.0
SWIGLU_ALPHA = 1.702

GLA_QK = GLA_HEADS * GLA_DK
GLA_V = GLA_HEADS * GLA_DV
NA_W = NA_HEADS * HEAD_DIM
SW_Q = SW_HEADS * HEAD_DIM
SW_KV = SW_KV_HEADS * HEAD_DIM

LANES = 128
SUBLANES = 8
VMEM_LIMIT_BYTES = 56 * 1024 * 1024
NEG = -1e30

TOKEN_TILE = 256
EXPERT_BLOCK = 512
MOE_CHUNK = 16
MOD_ROWS = 16


def _dot(a, b, **kw):
    return jnp.dot(a, b, preferred_element_type=F32, **kw)


def _dot_nt(a, b):
    return lax.dot_general(a, b, (((1,), (1,)), ((), ())), preferred_element_type=F32)


def _dot_tn(a, b):
    return lax.dot_general(a, b, (((0,), (0,)), ((), ())), preferred_element_type=F32)


def _split_bf16(x):
    hi = x.astype(BF16)
    lo = (x - hi.astype(F32)).astype(BF16)
    return hi, lo


def _params(sem, vmem=VMEM_LIMIT_BYTES):
    return pltpu.CompilerParams(dimension_semantics=sem, vmem_limit_bytes=vmem)


def _mod_kernel(cv_ref, w_ref, b_ref, o_ref):
    cv = cv_ref[...]
    s = cv * jax.nn.sigmoid(cv)
    o_ref[...] = _dot(s, w_ref[...], precision=HIGHEST) + b_ref[...]


def _modulation(cv, mod_w, mod_b):
    depth, d, n = mod_w.shape
    tn = 1536
    return pl.pallas_call(
        _mod_kernel,
        grid=(depth, n // tn),
        in_specs=[
            pl.BlockSpec((MOD_ROWS, d), lambda l, j: (0, 0)),
            pl.BlockSpec((None, d, tn), lambda l, j: (l, 0, j)),
            pl.BlockSpec((None, 1, tn), lambda l, j: (l, 0, j)),
        ],
        out_specs=pl.BlockSpec((None, MOD_ROWS, tn), lambda l, j: (l, 0, j)),
        out_shape=jax.ShapeDtypeStruct((depth, MOD_ROWS, n), F32),
        compiler_params=_params(("arbitrary", "arbitrary")),
        name="modulation",
    )(cv, mod_w, mod_b.reshape(depth, 1, n))


def _norm_mod(x, g_row, m_ref, shift_row, scale_row):
    ms = jnp.mean(x * x, axis=-1, keepdims=True)
    h = x * lax.rsqrt(ms + EPS) * g_row
    return h * (1.0 + m_ref[scale_row:scale_row + 1, :]) + m_ref[shift_row:shift_row + 1, :]


def _head_pair_ones():
    r = lax.broadcasted_iota(I32, (LANES, LANES), 0) // HEAD_DIM
    c = lax.broadcasted_iota(I32, (LANES, LANES), 1) // HEAD_DIM
    return jnp.where(r == c, 1.0, 0.0).astype(BF16)


def _head_rms(y, g_row, ones_blk):
    outs = []
    for j in range(y.shape[1] // LANES):
        s = y[:, j * LANES:(j + 1) * LANES]
        hi, lo = _split_bf16(s * s)
        ms = (_dot(hi, ones_blk) + _dot(lo, ones_blk)) * (1.0 / HEAD_DIM)
        outs.append(s * lax.rsqrt(ms + EPS))
    return jnp.concatenate(outs, axis=1) * g_row


def _mod_row_map(n_ctx_tiles, n_batch):
    return lambda b, t: (jnp.where(t < n_ctx_tiles, n_batch, b), 0, 0)


def _proj_even_kernel(x_ref, m_ref, gn_ref, w_ref, a2_ref, ba2_ref, qg_ref, kg_ref,
                      gq_ref, gk_ref, gv_ref, gate_ref, la_ref, nq_ref, nk_ref, nv_ref):
    h = _norm_mod(x_ref[...], gn_ref[...], m_ref, 0, 1)
    y = _dot(h.astype(BF16), w_ref[...])
    c0 = 0
    gq_ref[...] = y[:, c0:c0 + GLA_QK] * (GLA_DK ** -0.5)
    c0 += GLA_QK
    gk_ref[...] = y[:, c0:c0 + GLA_QK]
    c0 += GLA_QK
    gv_ref[...] = y[:, c0:c0 + GLA_V].astype(BF16)
    c0 += GLA_V
    gate_ref[...] = y[:, c0:c0 + GLA_V].astype(BF16)
    c0 += GLA_V
    ones_blk = _head_pair_ones()
    nq_ref[...] = (_head_rms(y[:, c0:c0 + NA_W], qg_ref[...], ones_blk) * (HEAD_DIM ** -0.5)).astype(BF16)
    c0 += NA_W
    nk_ref[...] = _head_rms(y[:, c0:c0 + NA_W], kg_ref[...], ones_blk).astype(BF16)
    c0 += NA_W
    nv_ref[...] = y[:, c0:c0 + NA_W].astype(BF16)
    c0 += NA_W
    a = y[:, c0:c0 + LANES]
    z = _dot(a, a2_ref[...], precision=HIGHEST) + ba2_ref[...]
    la_ref[...] = (jnp.minimum(z, 0.0) - jnp.log1p(jnp.exp(-jnp.abs(z)))) * (1.0 / GLA_TAU)


def _proj_even(xa, mods, gn, w_bf, a2, ba2, qg, kg, n_ctx_tiles):
    b, tt, d = xa.shape
    tm = TOKEN_TILE
    n = w_bf.shape[1]
    tok = lambda w: pl.BlockSpec((None, tm, w), lambda bi, t: (bi, t, 0))
    full = lambda shape: pl.BlockSpec(shape, lambda bi, t: (0,) * len(shape))
    widths = [(GLA_QK, F32), (GLA_QK, F32), (GLA_V, BF16), (GLA_V, BF16), (2 * GLA_QK, F32),
              (NA_W, BF16), (NA_W, BF16), (NA_W, BF16)]
    return pl.pallas_call(
        _proj_even_kernel,
        grid=(b, tt // tm),
        in_specs=[
            tok(d),
            pl.BlockSpec((None, 6, d), _mod_row_map(n_ctx_tiles, b)),
            full((1, d)), full((d, n)), full((LANES, 2 * GLA_QK)), full((1, 2 * GLA_QK)),
            full((1, NA_W)), full((1, NA_W)),
        ],
        out_specs=[tok(w) for w, _ in widths],
        out_shape=[jax.ShapeDtypeStruct((b, tt, w), dt) for w, dt in widths],
        compiler_params=_params(("parallel", "parallel")),
        name="proj_even",
    )(xa, mods, gn, w_bf, a2, ba2, qg, kg)


def _rope(y, cos, sin_signed, lo_mask):
    w = y.shape[1]
    reps = w // LANES
    cos_t = jnp.concatenate([cos] * reps, axis=1) if reps > 1 else cos
    sin_t = jnp.concatenate([sin_signed] * reps, axis=1) if reps > 1 else sin_signed
    msk = jnp.concatenate([lo_mask] * reps, axis=1) if reps > 1 else lo_mask
    quarter = HEAD_DIM // 4
    swapped = jnp.where(msk, pltpu.roll(y, w - quarter, 1), pltpu.roll(y, quarter, 1))
    return y * cos_t + swapped * sin_t


def _proj_odd_kernel(x_ref, m_ref, gn_ref, w_ref, qg_ref, kg_ref, cos_ref, sin_ref,
                     q_ref, k_ref, v_ref):
    h = _norm_mod(x_ref[...], gn_ref[...], m_ref, 0, 1)
    y = _dot(h.astype(BF16), w_ref[...])
    ones_blk = _head_pair_ones()
    lane = lax.broadcasted_iota(I32, (1, LANES), 1)
    lo_mask = (lane % (HEAD_DIM // 2)) < (HEAD_DIM // 4)
    cos = cos_ref[...]
    sin = sin_ref[...]
    q = _head_rms(y[:, 0:SW_Q], qg_ref[...], ones_blk)
    q_ref[...] = (_rope(q, cos, sin, lo_mask) * (HEAD_DIM ** -0.5)).astype(BF16)
    k = _head_rms(y[:, SW_Q:SW_Q + SW_KV], kg_ref[...], ones_blk)
    k_ref[...] = _rope(k, cos, sin, lo_mask).astype(BF16)
    v_ref[...] = y[:, SW_Q + SW_KV:SW_Q + 2 * SW_KV].astype(BF16)


def _proj_odd(xa, mods, gn, w_bf, qg, kg, cos, sin, n_ctx_tiles):
    b, tt, d = xa.shape
    tm = TOKEN_TILE
    n = w_bf.shape[1]
    tok = lambda w: pl.BlockSpec((None, tm, w), lambda bi, t: (bi, t, 0))
    full = lambda shape: pl.BlockSpec(shape, lambda bi, t: (0,) * len(shape))
    pos = pl.BlockSpec((tm, LANES), lambda bi, t: (t, 0))
    widths = [SW_Q, SW_KV, SW_KV]
    return pl.pallas_call(
        _proj_odd_kernel,
        grid=(b, tt // tm),
        in_specs=[
            tok(d),
            pl.BlockSpec((None, 6, d), _mod_row_map(n_ctx_tiles, b)),
            full((1, d)), full((d, n)), full((1, SW_Q)), full((1, SW_KV)), pos, pos,
        ],
        out_specs=[tok(w) for w in widths],
        out_shape=[jax.ShapeDtypeStruct((b, tt, w), BF16) for w in widths],
        compiler_params=_params(("parallel", "parallel")),
        name="proj_odd",
    )(xa, mods, gn, w_bf, qg, kg, cos, sin)


def _gla_kernel(q_ref, k_ref, v_ref, g_ref, o_ref, st_ref):
    c = GLA_CHUNK
    n_chunks = q_ref.shape[0] // c
    direction = pl.program_id(0)

    @pl.when(pl.program_id(2) == 0)
    def _():
        st_ref[...] = jnp.zeros_like(st_ref)

    sgn = jnp.where(direction == 0, 1, -1)
    r = lax.broadcasted_iota(I32, (2 * c, c), 0)
    j = lax.broadcasted_iota(I32, (2 * c, c), 1)
    i = r % c
    incl = jnp.where(sgn * (i - j) >= 0, 1.0, 0.0)
    blk = jnp.where(sgn * (i // GLA_SUB - j // GLA_SUB) > 0, 1.0, 0.0)
    scan_mat = jnp.where(r >= c, blk, incl).astype(BF16)
    causal = scan_mat[0:c, :].astype(F32)
    row = lax.broadcasted_iota(I32, (c, 1), 0)
    reachable = []
    for a in range(c // GLA_SUB):
        edge = jnp.where(direction == 0, a * GLA_SUB + GLA_SUB - 1, a * GLA_SUB)
        reachable.append(sgn * (edge - row) >= 0)

    for u in range(n_chunks):
        ci = jnp.where(direction == 0, u, n_chunks - 1 - u)
        rows = pl.ds(pl.multiple_of(ci * c, c), c)
        g_hi, g_lo = _split_bf16(g_ref[rows, :])
        cr = _dot(scan_mat, g_hi) + _dot(scan_mat, g_lo)
        cum = cr[0:c, :]
        ref_pt = cr[c:2 * c, :]
        tot = jnp.where(direction == 0, cum[c - 1:c, :], cum[0:1, :])

        q = q_ref[rows, :]
        k = k_ref[rows, :]
        q_state = (q * jnp.exp(cum)).astype(BF16)
        q_sub = (q * jnp.exp(cum - ref_pt)).astype(BF16)
        k_state = (k * jnp.exp(tot - cum)).astype(BF16)
        decay = jnp.exp(tot)

        attn_rows = [[] for _ in range(GLA_HEADS)]
        for a in range(c // GLA_SUB):
            lo = a * GLA_SUB
            ra = ref_pt[lo:lo + 1, :]
            k_sub = (k * jnp.exp(jnp.where(reachable[a], ra - cum, 0.0))).astype(BF16)
            for h in range(GLA_HEADS):
                hs = slice(h * GLA_DK, (h + 1) * GLA_DK)
                attn_rows[h].append(_dot_nt(q_sub[lo:lo + GLA_SUB, hs], k_sub[:, hs]))

        v = v_ref[rows, :]
        for h in range(GLA_HEADS):
            hs = slice(h * GLA_DK, (h + 1) * GLA_DK)
            vs = slice(h * GLA_DV, (h + 1) * GLA_DV)
            attn = (jnp.concatenate(attn_rows[h], axis=0) * causal).astype(BF16)
            st = st_ref[h]
            o_ref[rows, vs] = _dot(attn, v[:, vs]) + _dot_nt(q_state[:, hs], st.astype(BF16))
            st_ref[h] = st * decay[:, hs] + _dot_tn(v[:, vs], k_state[:, hs])


def _gla(gq, gk, gv, la, n_ctx):
    b, tt, _ = gq.shape
    rows = TOKEN_TILE
    nb = tt // rows
    n_ctx_blk = n_ctx // rows

    def blk(d, s):
        bwd = jnp.where(s < n_ctx_blk, n_ctx_blk - 1 - s, nb - 1 + n_ctx_blk - s)
        return jnp.where(d == 0, s, bwd)

    return pl.pallas_call(
        _gla_kernel,
        grid=(2, b, nb),
        in_specs=[
            pl.BlockSpec((None, rows, GLA_QK), lambda d, bi, s: (bi, blk(d, s), 0)),
            pl.BlockSpec((None, rows, GLA_QK), lambda d, bi, s: (bi, blk(d, s), 0)),
            pl.BlockSpec((None, rows, GLA_V), lambda d, bi, s: (bi, blk(d, s), 0)),
            pl.BlockSpec((None, rows, GLA_QK), lambda d, bi, s: (bi, blk(d, s), d)),
        ],
        out_specs=pl.BlockSpec((None, None, rows, GLA_V), lambda d, bi, s: (d, bi, blk(d, s), 0)),
        out_shape=jax.ShapeDtypeStruct((2, b, tt, GLA_V), F32),
        scratch_shapes=[pltpu.VMEM((GLA_HEADS, GLA_DV, GLA_DK), F32)],
        compiler_params=_params(("arbitrary", "arbitrary", "arbitrary")),
        name="gla_scan",
    )(gq, gk, gv, la)


def _na_bias_table(rpb, kh):
    n_heads = rpb.shape[0]
    qc = np.arange(GRID_W)[:, None]
    kc = np.arange(GRID_W)[None, :]
    win0 = np.clip(qc - NA_KW // 2, 0, GRID_W - NA_KW)
    valid = (kc >= win0) & (kc < win0 + NA_KW)
    side = GRID_W - NA_KW
    rp = jnp.pad(rpb, ((0, 0), (0, 0), (side, side)))
    base = jnp.stack([rp[:, :, GRID_W - 1 - c:2 * GRID_W - 1 - c] for c in range(GRID_W)], axis=2)
    base = jnp.where(valid[None, None], base, NEG)
    tab = jnp.stack([base[:, NA_KH - 1 - a:NA_KH - 1 - a + kh] for a in range(kh)], axis=1)
    tab = tab.reshape(n_heads // 2, 2, kh, kh, GRID_W, GRID_W).transpose(0, 2, 1, 4, 3, 5)
    return tab.reshape(n_heads // 2, kh, 2 * GRID_W, kh * GRID_W).astype(F32)


def _na_kernel(q_ref, k_ref, v_ref, bias_ref, o_ref, *, n_ctx, n_ctx_blk, rows_per_blk, n_rows, kh):
    t = pl.program_id(2)
    lane = lax.broadcasted_iota(I32, (1, LANES), 1)
    first = lane < HEAD_DIM
    kc = k_ref[0:n_ctx, :]
    vc = v_ref[0:n_ctx, :]

    def stack_heads(q):
        z = jnp.zeros_like(q)
        return jnp.concatenate([jnp.where(first, q, z), jnp.where(first, z, q)], axis=0)

    def unstack_heads(o):
        n = o.shape[0] // 2
        return jnp.where(first, o[0:n], o[n:2 * n])

    @pl.when(t < n_ctx_blk)
    def _():
        s = _dot_nt(stack_heads(q_ref[...]), kc)
        m = jnp.max(s, axis=-1, keepdims=True)
        p = jnp.exp(s - m)
        den = jnp.sum(p, axis=-1, keepdims=True)
        o_ref[...] = unstack_heads(_dot(p.astype(BF16), vc) / den).astype(o_ref.dtype)

    @pl.when(t >= n_ctx_blk)
    def _():
        for i in range(rows_per_blk):
            r = (t - n_ctx_blk) * rows_per_blk + i
            row0 = jnp.clip(r - kh // 2, 0, n_rows - kh)
            cls = r - row0
            start = pl.multiple_of(n_ctx + row0 * GRID_W, GRID_W)
            kb = k_ref[pl.ds(start, kh * GRID_W), :]
            vb = v_ref[pl.ds(start, kh * GRID_W), :]
            qs = stack_heads(q_ref[i * GRID_W:(i + 1) * GRID_W, :])
            s = _dot_nt(qs, kb) + bias_ref[cls]
            sc = _dot_nt(qs, kc)
            m = jnp.maximum(jnp.max(s, axis=-1, keepdims=True), jnp.max(sc, axis=-1, keepdims=True))
            p = jnp.exp(s - m)
            pc = jnp.exp(sc - m)
            den = jnp.sum(p, axis=-1, keepdims=True) + jnp.sum(pc, axis=-1, keepdims=True)
            o = (_dot(p.astype(BF16), vb) + _dot(pc.astype(BF16), vc)) / den
            o_ref[i * GRID_W:(i + 1) * GRID_W, :] = unstack_heads(o).astype(o_ref.dtype)


def _na(nq, nk, nv, bias_tab, n_ctx):
    b, tt, _ = nq.shape
    tq = TOKEN_TILE
    n_rows = (tt - n_ctx) // GRID_W
    kh = bias_tab.shape[1]
    kern = functools.partial(_na_kernel, n_ctx=n_ctx, n_ctx_blk=n_ctx // tq, rows_per_blk=tq // GRID_W,
                             n_rows=n_rows, kh=kh)
    return pl.pallas_call(
        kern,
        grid=(b, NA_HEADS // 2, tt // tq),
        in_specs=[
            pl.BlockSpec((None, tq, LANES), lambda bi, p, t: (bi, t, p)),
            pl.BlockSpec((None, tt, LANES), lambda bi, p, t: (bi, 0, p)),
            pl.BlockSpec((None, tt, LANES), lambda bi, p, t: (bi, 0, p)),
            pl.BlockSpec((None, kh, 2 * GRID_W, kh * GRID_W), lambda bi, p, t: (p, 0, 0, 0)),
        ],
        out_specs=pl.BlockSpec((None, tq, LANES), lambda bi, p, t: (bi, t, p)),
        out_shape=jax.ShapeDtypeStruct((b, tt, NA_W), BF16),
        compiler_params=_params(("parallel", "parallel", "arbitrary")),
        name="na_attn",
    )(nq, nk, nv, bias_tab)


def _swa_kernel(sink_ref, q_ref, k_ref, v_ref, o_ref, *, n_ctx, n_lat_blk):
    blk = SW_BLOCK
    n = pl.program_id(1)
    n_ctx_blk = n_ctx // blk
    jb = n - n_ctx_blk
    sb = jnp.clip(jb - 1, 0, n_lat_blk - 3)
    start = pl.multiple_of(n_ctx + sb * blk, blk)
    keys = jnp.concatenate([k_ref[pl.ds(start, 3 * blk), :], k_ref[0:n_ctx, :]], axis=0)
    vals = jnp.concatenate([v_ref[pl.ds(start, 3 * blk), :], v_ref[0:n_ctx, :]], axis=0)
    nk = 3 * blk + n_ctx
    iq = lax.broadcasted_iota(I32, (blk, nk), 0)
    ik = lax.broadcasted_iota(I32, (blk, nk), 1)
    offset = jnp.where(n < n_ctx_blk, 4 * SW_WINDOW, (jb - sb) * blk)
    dist = jnp.where(ik >= 3 * blk, 0, iq - ik + offset)
    bias = jnp.where(jnp.abs(dist) <= SW_WINDOW, 0.0, NEG)
    q = q_ref[...]
    group = SW_HEADS // SW_KV_HEADS
    bias_g = jnp.concatenate([bias] * group, axis=0)
    lane = lax.broadcasted_iota(I32, (1, LANES), 1)
    assert SW_KV_HEADS * HEAD_DIM == LANES
    for g in range(SW_KV_HEADS):
        gs = slice(g * HEAD_DIM, (g + 1) * HEAD_DIM)
        in_g = (lane // HEAD_DIM) == g
        heads = range(g * group, (g + 1) * group)
        blocks = []
        for h in heads:
            two = q[:, (h // 2) * LANES:(h // 2 + 1) * LANES]
            if h % 2 != g:
                two = jnp.concatenate([two[:, HEAD_DIM:], two[:, :HEAD_DIM]], axis=1)
            blocks.append(jnp.where(in_g, two, jnp.zeros_like(two)))
        qg = jnp.concatenate(blocks, axis=0)
        sink = jnp.concatenate([jnp.full((blk, 1), sink_ref[h], F32) for h in heads], axis=0)
        s = _dot_nt(qg, keys) + bias_g
        m = jnp.maximum(jnp.max(s, axis=-1, keepdims=True), sink)
        p = jnp.exp((s - m).astype(BF16))
        o2 = _dot(p, jnp.where(in_g, vals, jnp.ones_like(vals)))
        other = (1 - g) * HEAD_DIM
        den = o2[:, other:other + 1] + jnp.exp(sink - m)
        o = o2[:, gs] / den
        for pr in range(group // 2):
            pair = [o[(2 * pr + e) * blk:(2 * pr + e + 1) * blk, :] for e in range(2)]
            col = (g * group // 2 + pr) * LANES
            o_ref[:, col:col + LANES] = jnp.concatenate(pair, axis=1).astype(o_ref.dtype)


def _swa(q, k, v, sinks, n_ctx):
    b, tt, _ = q.shape
    blk = SW_BLOCK
    kern = functools.partial(_swa_kernel, n_ctx=n_ctx, n_lat_blk=(tt - n_ctx) // blk)
    return pl.pallas_call(
        kern,
        grid=(b, tt // blk),
        in_specs=[
            pl.BlockSpec(memory_space=pltpu.SMEM),
            pl.BlockSpec((None, blk, SW_Q), lambda bi, n: (bi, n, 0)),
            pl.BlockSpec((None, tt, SW_KV), lambda bi, n: (bi, 0, 0)),
            pl.BlockSpec((None, tt, SW_KV), lambda bi, n: (bi, 0, 0)),
        ],
        out_specs=pl.BlockSpec((None, blk, SW_Q), lambda bi, n: (bi, n, 0)),
        out_shape=jax.ShapeDtypeStruct((b, tt, SW_Q), BF16),
        compiler_params=_params(("parallel", "arbitrary")),
        name="swa_attn",
    )(sinks, q, k, v)


def _route(f, wr_ref, br_ref, before_ref, run_ref, idx_ref, wt_ref, rank_ref, cnt_ref, base_ref):
    tm = f.shape[0]
    f_hi, f_lo = _split_bf16(f)
    logits = _dot(f_hi, wr_ref[0]) + (_dot(f_lo, wr_ref[0]) + _dot(f_hi, wr_ref[1])) + br_ref[...]
    e_iota = lax.broadcasted_iota(I32, (tm, N_EXPERTS), 1).astype(F32)
    lane8 = lax.broadcasted_iota(I32, (tm, 8), 1)
    vals, hots = [], []
    idx_out = jnp.zeros((tm, 8), I32)
    for k in range(TOP_K):
        mx = jnp.max(logits, axis=-1, keepdims=True)
        am = jnp.min(jnp.where(logits == mx, e_iota, float(N_EXPERTS)), axis=-1, keepdims=True)
        hot = e_iota == am
        vals.append(mx)
        hots.append(hot)
        idx_out = jnp.where(lane8 == k, am.astype(I32), idx_out)
        logits = jnp.where(hot, -jnp.inf, logits)
    exps = [jnp.exp(v - vals[0]) for v in vals]
    den = exps[0] + exps[1] + exps[2] + exps[3]
    multihot = jnp.zeros((tm, N_EXPERTS), F32)
    wt_out = jnp.zeros((tm, 8), F32)
    for k in range(TOP_K):
        multihot = multihot + jnp.where(hots[k], 1.0, 0.0)
        wt_out = jnp.where(lane8 == k, exps[k] / den, wt_out)
    base = _dot(before_ref[...], multihot.astype(BF16)) + run_ref[...]
    rank_out = jnp.zeros((tm, 8), I32)
    for k in range(TOP_K):
        rk = jnp.sum(jnp.where(hots[k], base, 0.0), axis=-1, keepdims=True)
        rank_out = jnp.where(lane8 == k, rk.astype(I32), rank_out)
    base_ref[...] = run_ref[...]
    run_ref[...] = run_ref[...] + jnp.sum(multihot, axis=0, keepdims=True)
    idx_ref[...] = idx_out
    wt_ref[...] = wt_out
    rank_ref[...] = rank_out
    cnt_ref[...] = run_ref[...]


def _post_common(mix, x_ref, m_ref, fg_ref, wr_ref, br_ref, before_ref, xn_ref, f_ref, idx_ref, wt_ref, rank_ref,
                 cnt_ref, base_ref, run_ref):
    @pl.when((pl.program_id(0) == 0) & (pl.program_id(1) == 0))
    def _():
        run_ref[...] = jnp.zeros_like(run_ref)

    xn = x_ref[...] + m_ref[2:3, :] * mix
    xn_ref[...] = xn
    f = _norm_mod(xn, fg_ref[...], m_ref, 3, 4)
    f_ref[...] = f.astype(f_ref.dtype)
    _route(f, wr_ref, br_ref, before_ref, run_ref, idx_ref, wt_ref, rank_ref, cnt_ref, base_ref)


def _post_even_kernel(of_ref, ob_ref, gate_ref, na_ref, gng_ref, w_ref, x_ref, m_ref, fg_ref, wr_ref, br_ref,
                      before_ref, xn_ref, f_ref, idx_ref, wt_ref, rank_ref, cnt_ref, base_ref, run_ref):
    o = of_ref[...] + ob_ref[...]
    gate = gate_ref[...].astype(F32)
    gn = gng_ref[...]
    parts = []
    for h in range(GLA_HEADS):
        oh = o[:, h * GLA_DV:(h + 1) * GLA_DV]
        ms = jnp.mean(oh * oh, axis=-1, keepdims=True)
        parts.append(oh * lax.rsqrt(ms + EPS) * gn)
    gla = jnp.concatenate(parts, axis=1) * (gate * jax.nn.sigmoid(gate))
    mix = _dot(gla.astype(BF16), w_ref[0:GLA_V, :]) + _dot(na_ref[...], w_ref[GLA_V:GLA_V + NA_W, :])
    _post_common(mix, x_ref, m_ref, fg_ref, wr_ref, br_ref, before_ref, xn_ref, f_ref, idx_ref, wt_ref, rank_ref,
                 cnt_ref, base_ref, run_ref)


def _post_odd_kernel(o_ref, w_ref, x_ref, m_ref, fg_ref, wr_ref, br_ref, before_ref,
                     xn_ref, f_ref, idx_ref, wt_ref, rank_ref, cnt_ref, base_ref, run_ref):
    mix = _dot(o_ref[...], w_ref[...])
    _post_common(mix, x_ref, m_ref, fg_ref, wr_ref, br_ref, before_ref, xn_ref, f_ref, idx_ref, wt_ref, rank_ref,
                 cnt_ref, base_ref, run_ref)


def _post(even, mixer_outs, extra, w_bf, xa, mods, fg, wr, br, n_ctx_tiles):
    b, tt, d = xa.shape
    tm = TOKEN_TILE
    tok = lambda w: pl.BlockSpec((None, tm, w), lambda bi, t: (bi, t, 0))
    full = lambda shape: pl.BlockSpec(shape, lambda bi, t: (0,) * len(shape))
    if even:
        o2, gate, na = mixer_outs
        lead_specs = [
            pl.BlockSpec((None, None, tm, GLA_V), lambda bi, t: (0, bi, t, 0)),
            pl.BlockSpec((None, None, tm, GLA_V), lambda bi, t: (1, bi, t, 0)),
            tok(GLA_V), tok(NA_W), full((1, GLA_DV)),
        ]
        lead_args = [o2, o2, gate, na, extra]
        kern = _post_even_kernel
    else:
        (o,) = mixer_outs
        lead_specs = [tok(SW_Q)]
        lead_args = [o]
        kern = _post_odd_kernel
    nt = tt // tm
    small = lambda: pl.BlockSpec((None, tm, 8), lambda bi, t: (bi, t, 0))
    before = jnp.asarray(np.tril(np.ones((tm, tm), np.float32), -1), BF16)
    return pl.pallas_call(
        kern,
        grid=(b, tt // tm),
        in_specs=lead_specs + [
            full(w_bf.shape), tok(d),
            pl.BlockSpec((None, 6, d), _mod_row_map(n_ctx_tiles, b)),
            full((1, d)), full((2, d, N_EXPERTS)), full((1, N_EXPERTS)), full((tm, tm)),
        ],
        out_specs=[tok(d), tok(d), small(), small(), small(), full((1, N_EXPERTS)),
                   pl.BlockSpec((None, 1, N_EXPERTS), lambda bi, t: (bi * nt + t, 0, 0))],
        out_shape=[
            jax.ShapeDtypeStruct((b, tt, d), F32), jax.ShapeDtypeStruct((b, tt, d), BF16),
            jax.ShapeDtypeStruct((b, tt, 8), I32), jax.ShapeDtypeStruct((b, tt, 8), F32),
            jax.ShapeDtypeStruct((b, tt, 8), I32), jax.ShapeDtypeStruct((1, N_EXPERTS), F32),
            jax.ShapeDtypeStruct((b * nt, 1, N_EXPERTS), F32),
        ],
        scratch_shapes=[pltpu.VMEM((1, N_EXPERTS), F32)],
        compiler_params=_params(("arbitrary", "arbitrary")),
        name="post_even" if even else "post_odd",
    )(*lead_args, w_bf, xa, mods, fg, wr, br, before)


def _stage_rows(tm):
    return tm * TOP_K + N_EXPERTS * (MOE_CHUNK + SUBLANES)


WAIT_BATCH = 4


def _for_tile_chunks(rows_ref, tot_ref, g, max_chunks, fn):
    def per_chunk(j, c):
        fn(pl.multiple_of(rows_ref[g * max_chunks + j], SUBLANES), pl.multiple_of(j * MOE_CHUNK, MOE_CHUNK))
        return c

    lax.fori_loop(0, tot_ref[g], per_chunk, 0)


def _wait_chunks(n, make_copy):
    def batch(i, c):
        make_copy(WAIT_BATCH * MOE_CHUNK).wait()
        return c

    def single(i, c):
        make_copy(MOE_CHUNK).wait()
        return c

    lax.fori_loop(0, n // WAIT_BATCH, batch, 0)
    lax.fori_loop(0, n % WAIT_BATCH, single, 0)


def _dispatch_kernel(rows_ref, tot_ref, nch_ref, off_ref, lead_ref, cnt_ref, lo_ref, hi_ref, colt_ref, f_ref, xs_ref,
                     stage, carry, zeros_ref, sems, zsem):
    g = pl.program_id(0)
    n_tiles = pl.num_programs(0)
    ch = MOE_CHUNK
    buf = g % 2
    max_chunks = stage.shape[1] // ch

    def chunk_copy(which, slot_row, stage_row, rows=ch):
        return pltpu.make_async_copy(stage.at[which, pl.ds(stage_row, rows)], xs_ref.at[pl.ds(slot_row, rows)],
                                     sems.at[which])

    def wait_tile(tile, which):
        _wait_chunks(tot_ref[tile], lambda rows: chunk_copy(which, 0, 0, rows))

    @pl.when(g == 0)
    def _():
        carry[...] = jnp.zeros_like(carry)
        zeros_ref[...] = jnp.zeros_like(zeros_ref)

        def pad_pass(do):
            def per_expert(e, c):
                lo = lo_ref[e]
                hi = hi_ref[e]
                n_full = (hi - lo) // ch

                def full(j, c2):
                    do(pltpu.make_async_copy(
                        zeros_ref, xs_ref.at[pl.ds(pl.multiple_of(hi - (j + 1) * ch, SUBLANES), ch)], zsem))
                    return c2

                lax.fori_loop(0, n_full, full, 0)

                @pl.when((hi - lo) - n_full * ch > 0)
                def _():
                    do(pltpu.make_async_copy(zeros_ref.at[pl.ds(0, SUBLANES)],
                                             xs_ref.at[pl.ds(pl.multiple_of(lo, SUBLANES), SUBLANES)], zsem))
                return c

            lax.fori_loop(0, lo_ref.shape[0], per_expert, 0)

        pad_pass(lambda cp: cp.start())
        pad_pass(lambda cp: cp.wait())

    colt = colt_ref[...]
    rs = stage.shape[1]
    r_iota = lax.broadcasted_iota(I32, (rs, colt.shape[1]), 0)
    hit = jnp.zeros(r_iota.shape, F32)
    for k in range(TOP_K):
        hit = jnp.where(r_iota == colt[k:k + 1, :], 1.0, hit)
    stage[buf] = _dot(hit.astype(BF16), f_ref[...])

    for e in range(N_EXPERTS):
        i = g * N_EXPERTS + e
        active = nch_ref[i] > 0
        ptr = off_ref[i]
        head = pl.ds(pl.multiple_of(ptr, SUBLANES), SUBLANES)
        stage[buf, head, :] = stage[buf, head, :] + jnp.where(active, carry[e], 0.0)
        end = lead_ref[i] + cnt_ref[i]
        new_lead = end % SUBLANES
        tail = pl.ds(pl.multiple_of(ptr + end - new_lead, SUBLANES), SUBLANES)
        carry[e] = jnp.where(active, jnp.where(new_lead > 0, stage[buf, tail, :], 0.0), carry[e])

    @pl.when(g > 0)
    def _():
        wait_tile(g - 1, 1 - buf)

    _for_tile_chunks(rows_ref, tot_ref, g, max_chunks,
                     lambda slot_row, stage_row: chunk_copy(buf, slot_row, stage_row).start())

    @pl.when(g == n_tiles - 1)
    def _():
        wait_tile(g, buf)


def _dispatch(rows, tot, nch, off, lead, cnt, pad_lo, pad_hi, colt, f2, n_slots):
    t, d = f2.shape
    tm = TOKEN_TILE
    rs = _stage_rows(tm)
    return pl.pallas_call(
        _dispatch_kernel,
        grid_spec=pltpu.PrefetchScalarGridSpec(
            num_scalar_prefetch=8,
            grid=(t // tm,),
            in_specs=[
                pl.BlockSpec((None, 8, tm), lambda g, *_: (g, 0, 0)),
                pl.BlockSpec((tm, d), lambda g, *_: (g, 0)),
            ],
            out_specs=pl.BlockSpec(memory_space=pl.ANY),
            scratch_shapes=[pltpu.VMEM((2, rs, d), F32), pltpu.VMEM((N_EXPERTS, SUBLANES, d), F32),
                            pltpu.VMEM((MOE_CHUNK, d), F32),
                            pltpu.SemaphoreType.DMA((2,)), pltpu.SemaphoreType.DMA(())],
        ),
        out_shape=jax.ShapeDtypeStruct((n_slots, d), F32),
        compiler_params=_params(("arbitrary",)),
        name="moe_dispatch",
    )(rows, tot, nch, off, lead, cnt, pad_lo, pad_hi, colt, f2)


def _expert_kernel(be_ref, nu_ref, xs_ref, wgu_ref, bgu_ref, wd_ref, bd_ref, ys_ref, wgu_bf, wd_bf):
    i = pl.program_id(0)
    prev = be_ref[jnp.maximum(i - 1, 0)]
    fresh = (i == 0) | (be_ref[i] != prev)

    @pl.when(fresh & (i < nu_ref[0]))
    def _():
        wgu_bf[...] = wgu_ref[...].astype(BF16)
        wd_bf[...] = wd_ref[...].astype(BF16)

    @pl.when(i < nu_ref[0])
    def _():
        x = xs_ref[...].astype(BF16)
        de = wd_bf.shape[0]
        half = de // 2
        acc = None
        for c in range(2):
            glu = _dot(x, wgu_bf[:, c * half:(c + 1) * half]) + bgu_ref[:, c * half:(c + 1) * half]
            lin = _dot(x, wgu_bf[:, de + c * half:de + (c + 1) * half]) + bgu_ref[:, de + c * half:de + (c + 1) * half]
            glu = jnp.minimum(glu, SWIGLU_LIMIT)
            lin = jnp.clip(lin, -SWIGLU_LIMIT, SWIGLU_LIMIT)
            act = (glu * jax.nn.sigmoid(SWIGLU_ALPHA * glu) * (lin + 1.0)).astype(BF16)
            part = _dot(act, wd_bf[c * half:(c + 1) * half, :])
            acc = part if acc is None else acc + part
        ys_ref[...] = acc + bd_ref[...]

    @pl.when(i >= nu_ref[0])
    def _():
        ys_ref[...] = jnp.zeros_like(ys_ref)


def _experts(block_e, n_used, xs, layer, wgu, bgu, wd, bd):
    n_slots, d = xs.shape
    bm = EXPERT_BLOCK
    depth, ne, _, de2 = wgu.shape
    de = wd.shape[2]
    row_blk = lambda i, be, nu: (jnp.minimum(i, nu[0] - 1), 0)
    exp_blk = lambda i, be, nu: (layer, be[i], 0, 0)
    return pl.pallas_call(
        _expert_kernel,
        grid_spec=pltpu.PrefetchScalarGridSpec(
            num_scalar_prefetch=2,
            grid=(n_slots // bm,),
            in_specs=[
                pl.BlockSpec((bm, d), row_blk),
                pl.BlockSpec((None, None, d, de2), exp_blk),
                pl.BlockSpec((None, None, 1, de2), exp_blk),
                pl.BlockSpec((None, None, de, d), exp_blk),
                pl.BlockSpec((None, None, 1, d), exp_blk),
            ],
            out_specs=pl.BlockSpec((bm, d), lambda i, be, nu: (i, 0)),
            scratch_shapes=[pltpu.VMEM((d, de2), BF16), pltpu.VMEM((de, d), BF16)],
        ),
        out_shape=jax.ShapeDtypeStruct((n_slots, d), F32),
        compiler_params=_params(("arbitrary",)),
        name="moe_experts",
    )(block_e, n_used, xs, wgu, bgu.reshape(depth, ne, 1, de2), wd, bd.reshape(depth, ne, 1, d))


def _combine_kernel(rows_ref, tot_ref, col_ref, wt_ref, x_ref, m_ref, ys_ref, o_ref, stage, sems):
    g = pl.program_id(0)
    n_tiles = pl.num_programs(0)
    ch = MOE_CHUNK
    buf = g % 2
    max_chunks = stage.shape[1] // ch

    def chunk_copy(which, slot_row, stage_row, rows=ch):
        return pltpu.make_async_copy(ys_ref.at[pl.ds(slot_row, rows)], stage.at[which, pl.ds(stage_row, rows)],
                                     sems.at[which])

    def fetch_tile(tile, which):
        _for_tile_chunks(rows_ref, tot_ref, tile, max_chunks,
                         lambda slot_row, stage_row: chunk_copy(which, slot_row, stage_row).start())

    @pl.when(g == 0)
    def _():
        stage[...] = jnp.zeros_like(stage)
        fetch_tile(0, 0)

    @pl.when(g + 1 < n_tiles)
    def _():
        fetch_tile(g + 1, 1 - buf)

    _wait_chunks(tot_ref[g], lambda rows: chunk_copy(buf, 0, 0, rows))

    col = col_ref[...]
    wt = wt_ref[...]
    tm = col.shape[0]
    group = 2 * LANES
    y = jnp.zeros(x_ref.shape, F32)
    for j in range(stage.shape[1] // group):
        c_iota = lax.broadcasted_iota(I32, (tm, group), 1) + j * group
        w = jnp.zeros((tm, group), F32)
        for k in range(TOP_K):
            w = jnp.where(c_iota == col[:, k:k + 1], wt[:, k:k + 1], w)
        w_hi, w_lo = _split_bf16(w)
        rows = stage[buf, j * group:(j + 1) * group, :].astype(BF16)
        y = y + (_dot(w_hi, rows) + _dot(w_lo, rows))
    o_ref[...] = x_ref[...] + m_ref[5:6, :] * y


def _combine(rows, tot, col, wts, xn, mods, ys, n_ctx_tiles):
    b, tt, d = xn.shape
    tm = TOKEN_TILE
    nt = tt // tm
    rs = _stage_rows(tm)
    assert rs % (2 * LANES) == 0
    return pl.pallas_call(
        _combine_kernel,
        grid_spec=pltpu.PrefetchScalarGridSpec(
            num_scalar_prefetch=2,
            grid=(b * nt,),
            in_specs=[
                pl.BlockSpec((tm, 8), lambda g, *_: (g, 0)),
                pl.BlockSpec((tm, 8), lambda g, *_: (g, 0)),
                pl.BlockSpec((tm, d), lambda g, *_: (g, 0)),
                pl.BlockSpec((None, 6, d), lambda g, *_: (jnp.where(g % nt < n_ctx_tiles, b, g // nt), 0, 0)),
                pl.BlockSpec(memory_space=pl.ANY),
            ],
            out_specs=pl.BlockSpec((tm, d), lambda g, *_: (g, 0)),
            scratch_shapes=[pltpu.VMEM((2, rs, d), F32), pltpu.SemaphoreType.DMA((2,))],
        ),
        out_shape=jax.ShapeDtypeStruct((b * tt, d), F32),
        compiler_params=_params(("arbitrary",)),
        name="moe_combine",
    )(rows, tot, col, wts.reshape(b * tt, 8), xn.reshape(b * tt, d), mods, ys).reshape(b, tt, d)


def _moe(xn, f, idx, wts, rank, cnt, tile_base, mods, layer, wgu, bgu, wd, bd, n_ctx_tiles):
    b, tt, d = xn.shape
    t = b * tt
    bm = EXPERT_BLOCK
    tm = TOKEN_TILE
    ch = MOE_CHUNK
    n_tiles = t // tm
    n_blocks = (t * TOP_K + N_EXPERTS * ch + bm - 1) // bm + N_EXPERTS
    e_ids = np.arange(N_EXPERTS)
    lower = e_ids[None, :] < e_ids[:, None]
    counts = cnt[0].astype(I32)
    padded = (counts + ch + bm - 1) // bm * bm
    pstart = jnp.sum(jnp.where(lower, padded[None, :], 0), axis=1)
    pend = pstart + padded
    base = tile_base.reshape(n_tiles, N_EXPERTS).astype(I32)
    tile_cnt = jnp.concatenate([base[1:], counts[None, :]], axis=0) - base
    first = pstart[None, :] + base
    seg = first // SUBLANES * SUBLANES
    lead = first - seg
    nch = jnp.where(tile_cnt > 0, (lead + tile_cnt + ch - 1) // ch, 0)
    off = jnp.sum(jnp.where(lower[None], nch[:, None, :], 0), axis=2) * ch
    idx4 = idx[..., :TOP_K].reshape(n_tiles, tm, TOP_K)
    rank4 = rank[..., :TOP_K].reshape(n_tiles, tm, TOP_K)
    delta = (off + lead - base)[:, None, None, :]
    col = rank4 + jnp.sum(jnp.where(idx4[..., None] == e_ids, delta, 0), axis=-1)
    col = jnp.concatenate([col, jnp.full((n_tiles, tm, 8 - TOP_K), -1, I32)], axis=-1)
    colt = col.transpose(0, 2, 1)
    n_used = (pend[N_EXPERTS - 1] // bm).reshape(1).astype(I32)
    starts = np.arange(n_blocks, dtype=np.int32)[:, None] * bm
    block_e = jnp.minimum(jnp.sum((pend[None, :] <= starts).astype(I32), axis=1), N_EXPERTS - 1)
    pad_lo = jnp.concatenate([(pstart + counts + SUBLANES - 1) // SUBLANES * SUBLANES, pend[N_EXPERTS - 1:]])
    pad_hi = jnp.concatenate([pend, jnp.full((1,), n_blocks * bm, I32)])
    max_chunks = _stage_rows(tm) // ch
    cum = off // ch + nch
    j = np.arange(max_chunks, dtype=np.int32)
    e_of_j = jnp.sum((cum[:, None, :] <= j[None, :, None]).astype(I32), axis=-1)
    hot = e_of_j[..., None] == e_ids
    seg_j = jnp.sum(jnp.where(hot, seg[:, None, :], 0), axis=-1)
    first_j = jnp.sum(jnp.where(hot, (off // ch)[:, None, :], 0), axis=-1)
    rows = (seg_j + (j[None, :] - first_j) * ch).reshape(-1)
    tot = cum[:, N_EXPERTS - 1]
    xs = _dispatch(rows, tot, nch.reshape(-1), off.reshape(-1), lead.reshape(-1), tile_cnt.reshape(-1),
                   pad_lo, pad_hi, colt, f.reshape(t, d), n_blocks * bm)
    ys = _experts(block_e, n_used, xs, layer, wgu, bgu, wd, bd)
    return _combine(rows, tot, col.reshape(t, 8), wts, xn, mods, ys, n_ctx_tiles)


def _rope_tables(n_ctx, s_len):
    quarter = HEAD_DIM // 4
    t = np.arange(s_len)
    pos = np.stack([t // GRID_W, t % GRID_W], axis=0).astype(np.float32)
    inv_freq = ROPE_BASE ** (-jnp.arange(quarter, dtype=F32) / quarter)
    ang = jnp.asarray(pos)[:, :, None] * inv_freq
    cos, sin = jnp.cos(ang), jnp.sin(ang)
    cos_h = jnp.concatenate([cos[0], cos[0], cos[1], cos[1]], axis=-1)
    sin_h = jnp.concatenate([-sin[0], sin[0], -sin[1], sin[1]], axis=-1)
    reps = LANES // HEAD_DIM
    cos_l = jnp.tile(cos_h, (1, reps))
    sin_l = jnp.tile(sin_h, (1, reps))
    cos_all = jnp.concatenate([jnp.ones((n_ctx, LANES), F32), cos_l], axis=0)
    sin_all = jnp.concatenate([jnp.zeros((n_ctx, LANES), F32), sin_l], axis=0)
    return cos_all, sin_all


def kernel(x, c, ctx, c_ctx, mod_w, mod_b, norm_mix_g, norm_ffn_g, ev_w_in, ev_w_out, ev_gla_w_a2, ev_gla_b_a2,
           ev_gla_norm_g, ev_na_q_g, ev_na_k_g, ev_na_rpb, od_w_in, od_w_out, od_q_g, od_k_g, od_sinks,
           moe_w_router, moe_b_router, moe_w_gate_up, moe_b_gate_up, moe_w_down, moe_b_down):
    b, s_len, d = x.shape
    n_ctx = ctx.shape[1]
    depth = mod_w.shape[0]
    assert b < MOD_ROWS and n_ctx % TOKEN_TILE == 0 and s_len % TOKEN_TILE == 0
    n_ctx_tiles = n_ctx // TOKEN_TILE
    n_rows = s_len // GRID_W
    kh = min(NA_KH, n_rows)

    xa = jnp.concatenate([ctx, x], axis=1)
    cv = jnp.zeros((MOD_ROWS, d), F32).at[:b].set(c).at[b].set(c_ctx)
    mods_all = _modulation(cv, mod_w, mod_b).reshape(depth, MOD_ROWS, 6, d)
    cos, sin = _rope_tables(n_ctx, s_len)

    for layer in range(depth):
        i = layer // 2
        mods = mods_all[layer]
        gn = norm_mix_g[layer].reshape(1, d)
        if layer % 2 == 0:
            w = ev_w_in[i]
            cuts = np.cumsum([0, GLA_QK, GLA_QK, GLA_V, GLA_V, GLA_RANK, GLA_RANK, NA_W, NA_W, NA_W])
            seg = lambda j: w[:, cuts[j]:cuts[j + 1]]
            pad = jnp.zeros((d, LANES - 2 * GLA_RANK), F32)
            w_bf = jnp.concatenate([seg(0), seg(1), seg(2), seg(3), seg(6), seg(7), seg(8), seg(4), seg(5), pad],
                                   axis=1).astype(BF16)
            a2 = jnp.zeros((LANES, 2 * GLA_QK), F32)
            a2 = a2.at[0:GLA_RANK, 0:GLA_QK].set(ev_gla_w_a2[i, 0])
            a2 = a2.at[GLA_RANK:2 * GLA_RANK, GLA_QK:].set(ev_gla_w_a2[i, 1])
            ba2 = ev_gla_b_a2[i].reshape(1, 2 * GLA_QK)
            qg = jnp.tile(ev_na_q_g[i], NA_HEADS).reshape(1, NA_W)
            kg = jnp.tile(ev_na_k_g[i], NA_HEADS).reshape(1, NA_W)
            gq, gk, gv, gate, la, nq, nk, nv = _proj_even(xa, mods, gn, w_bf, a2, ba2, qg, kg, n_ctx_tiles)
            o2 = _gla(gq, gk, gv, la, n_ctx)
            o_na = _na(nq, nk, nv, _na_bias_table(ev_na_rpb[i], kh), n_ctx)
            mixer_outs = (o2, gate, o_na)
            extra = ev_gla_norm_g[i].reshape(1, GLA_DV)
            w_out = ev_w_out[i].astype(BF16)
        else:
            w_bf = od_w_in[i].astype(BF16)
            qg = jnp.tile(od_q_g[i], SW_HEADS).reshape(1, SW_Q)
            kg = jnp.tile(od_k_g[i], SW_KV_HEADS).reshape(1, SW_KV)
            q, k, v = _proj_odd(xa, mods, gn, w_bf, qg, kg, cos, sin, n_ctx_tiles)
            mixer_outs = (_swa(q, k, v, od_sinks[i], n_ctx),)
            extra = None
            w_out = od_w_out[i].astype(BF16)
        xn, f, idx, wts, rank, cnt, tile_base = _post(
            layer % 2 == 0, mixer_outs, extra, w_out, xa, mods, norm_ffn_g[layer].reshape(1, d),
            jnp.stack(_split_bf16(moe_w_router[layer])), moe_b_router[layer].reshape(1, N_EXPERTS), n_ctx_tiles)
        xa = _moe(xn, f, idx, wts, rank, cnt, tile_base, mods, layer, moe_w_gate_up, moe_b_gate_up, moe_w_down, moe_b_down,
                  n_ctx_tiles)
    return xa[:, n_ctx:, :]
```

```python
import functools

import jax
import jax.numpy as jnp
import numpy as np
from jax import lax
from jax.experimental import pallas as pl
from jax.experimental.pallas import tpu as pltpu

F32 = jnp.float32
BF16 = jnp.bfloat16
I32 = jnp.int32
HIGHEST = lax.Precision.HIGHEST

GRID_W = 64
HEAD_DIM = 64
EPS = 1e-6
GLA_HEADS = 4
GLA_DK = 64
GLA_DV = 128
GLA_RANK = 16
GLA_TAU = 16.0
GLA_CHUNK = 64
GLA_SUB = 16
NA_HEADS = 8
NA_KH = 8
NA_KW = 16
SW_HEADS = 16
SW_KV_HEADS = 2
SW_WINDOW = 128
SW_BLOCK = 128
ROPE_BASE = 10000.0
N_EXPERTS = 32
TOP_K = 4
SWIGLU_LIMIT = 7.0
SWIGLU_ALPHA = 1.702

GLA_QK = GLA_HEADS * GLA_DK
GLA_V = GLA_HEADS * GLA_DV
NA_W = NA_HEADS * HEAD_DIM
SW_Q = SW_HEADS * HEAD_DIM
SW_KV = SW_KV_HEADS * HEAD_DIM

LANES = 128
SUBLANES = 8
VMEM_LIMIT_BYTES = 56 * 1024 * 1024
NEG = -1e30

TOKEN_TILE = 256
EXPERT_BLOCK = 512
MOE_CHUNK = 16
MOD_ROWS = 16


def _dot(a, b, **kw):
    return jnp.dot(a, b, preferred_element_type=F32, **kw)


def _dot_nt(a, b):
    return lax.dot_general(a, b, (((1,), (1,)), ((), ())), preferred_element_type=F32)


def _dot_tn(a, b):
    return lax.dot_general(a, b, (((0,), (0,)), ((), ())), preferred_element_type=F32)


def _split_bf16(x):
    hi = x.astype(BF16)
    lo = (x - hi.astype(F32)).astype(BF16)
    return hi, lo


def _params(sem, vmem=VMEM_LIMIT_BYTES):
    return pltpu.CompilerParams(dimension_semantics=sem, vmem_limit_bytes=vmem)


def _mod_kernel(cv_ref, w_ref, b_ref, o_ref):
    cv = cv_ref[...]
    s = cv * jax.nn.sigmoid(cv)
    o_ref[...] = _dot(s, w_ref[...], precision=HIGHEST) + b_ref[...]


def _modulation(cv, mod_w, mod_b):
    depth, d, n = mod_w.shape
    tn = 1536
    return pl.pallas_call(
        _mod_kernel,
        grid=(depth, n // tn),
        in_specs=[
            pl.BlockSpec((MOD_ROWS, d), lambda l, j: (0, 0)),
            pl.BlockSpec((None, d, tn), lambda l, j: (l, 0, j)),
            pl.BlockSpec((None, 1, tn), lambda l, j: (l, 0, j)),
        ],
        out_specs=pl.BlockSpec((None, MOD_ROWS, tn), lambda l, j: (l, 0, j)),
        out_shape=jax.ShapeDtypeStruct((depth, MOD_ROWS, n), F32),
        compiler_params=_params(("arbitrary", "arbitrary")),
        name="modulation",
    )(cv, mod_w, mod_b.reshape(depth, 1, n))


def _norm_mod(x, g_row, m_ref, shift_row, scale_row):
    ms = jnp.mean(x * x, axis=-1, keepdims=True)
    h = x * lax.rsqrt(ms + EPS) * g_row
    return h * (1.0 + m_ref[scale_row:scale_row + 1, :]) + m_ref[shift_row:shift_row + 1, :]


def _head_pair_ones():
    r = lax.broadcasted_iota(I32, (LANES, LANES), 0) // HEAD_DIM
    c = lax.broadcasted_iota(I32, (LANES, LANES), 1) // HEAD_DIM
    return jnp.where(r == c, 1.0, 0.0).astype(BF16)


def _head_rms(y, g_row, ones_blk):
    outs = []
    for j in range(y.shape[1] // LANES):
        s = y[:, j * LANES:(j + 1) * LANES]
        hi, lo = _split_bf16(s * s)
        ms = (_dot(hi, ones_blk) + _dot(lo, ones_blk)) * (1.0 / HEAD_DIM)
        outs.append(s * lax.rsqrt(ms + EPS))
    return jnp.concatenate(outs, axis=1) * g_row


def _mod_row_map(n_ctx_tiles, n_batch):
    return lambda b, t: (jnp.where(t < n_ctx_tiles, n_batch, b), 0, 0)


def _proj_even_kernel(x_ref, m_ref, gn_ref, w_ref, a2_ref, ba2_ref, qg_ref, kg_ref,
                      gq_ref, gk_ref, gv_ref, gate_ref, la_ref, nq_ref, nk_ref, nv_ref):
    h = _norm_mod(x_ref[...], gn_ref[...], m_ref, 0, 1)
    y = _dot(h.astype(BF16), w_ref[...])
    c0 = 0
    gq_ref[...] = y[:, c0:c0 + GLA_QK] * (GLA_DK ** -0.5)
    c0 += GLA_QK
    gk_ref[...] = y[:, c0:c0 + GLA_QK]
    c0 += GLA_QK
    gv_ref[...] = y[:, c0:c0 + GLA_V].astype(BF16)
    c0 += GLA_V
    gate_ref[...] = y[:, c0:c0 + GLA_V].astype(BF16)
    c0 += GLA_V
    ones_blk = _head_pair_ones()
    nq_ref[...] = (_head_rms(y[:, c0:c0 + NA_W], qg_ref[...], ones_blk) * (HEAD_DIM ** -0.5)).astype(BF16)
    c0 += NA_W
    nk_ref[...] = _head_rms(y[:, c0:c0 + NA_W], kg_ref[...], ones_blk).astype(BF16)
    c0 += NA_W
    nv_ref[...] = y[:, c0:c0 + NA_W].astype(BF16)
    c0 += NA_W
    a = y[:, c0:c0 + LANES]
    z = _dot(a, a2_ref[...], precision=HIGHEST) + ba2_ref[...]
    la_ref[...] = (jnp.minimum(z, 0.0) - jnp.log1p(jnp.exp(-jnp.abs(z)))) * (1.0 / GLA_TAU)


def _proj_even(xa, mods, gn, w_bf, a2, ba2, qg, kg, n_ctx_tiles):
    b, tt, d = xa.shape
    tm = TOKEN_TILE
    n = w_bf.shape[1]
    tok = lambda w: pl.BlockSpec((None, tm, w), lambda bi, t: (bi, t, 0))
    full = lambda shape: pl.BlockSpec(shape, lambda bi, t: (0,) * len(shape))
    widths = [(GLA_QK, F32), (GLA_QK, F32), (GLA_V, BF16), (GLA_V, BF16), (2 * GLA_QK, F32),
              (NA_W, BF16), (NA_W, BF16), (NA_W, BF16)]
    return pl.pallas_call(
        _proj_even_kernel,
        grid=(b, tt // tm),
        in_specs=[
            tok(d),
            pl.BlockSpec((None, 6, d), _mod_row_map(n_ctx_tiles, b)),
            full((1, d)), full((d, n)), full((LANES, 2 * GLA_QK)), full((1, 2 * GLA_QK)),
            full((1, NA_W)), full((1, NA_W)),
        ],
        out_specs=[tok(w) for w, _ in widths],
        out_shape=[jax.ShapeDtypeStruct((b, tt, w), dt) for w, dt in widths],
        compiler_params=_params(("parallel", "parallel")),
        name="proj_even",
    )(xa, mods, gn, w_bf, a2, ba2, qg, kg)


def _rope(y, cos, sin_signed, lo_mask):
    w = y.shape[1]
    reps = w // LANES
    cos_t = jnp.concatenate([cos] * reps, axis=1) if reps > 1 else cos
    sin_t = jnp.concatenate([sin_signed] * reps, axis=1) if reps > 1 else sin_signed
    msk = jnp.concatenate([lo_mask] * reps, axis=1) if reps > 1 else lo_mask
    quarter = HEAD_DIM // 4
    swapped = jnp.where(msk, pltpu.roll(y, w - quarter, 1), pltpu.roll(y, quarter, 1))
    return y * cos_t + swapped * sin_t


def _proj_odd_kernel(x_ref, m_ref, gn_ref, w_ref, qg_ref, kg_ref, cos_ref, sin_ref,
                     q_ref, k_ref, v_ref):
    h = _norm_mod(x_ref[...], gn_ref[...], m_ref, 0, 1)
    y = _dot(h.astype(BF16), w_ref[...])
    ones_blk = _head_pair_ones()
    lane = lax.broadcasted_iota(I32, (1, LANES), 1)
    lo_mask = (lane % (HEAD_DIM // 2)) < (HEAD_DIM // 4)
    cos = cos_ref[...]
    sin = sin_ref[...]
    q = _head_rms(y[:, 0:SW_Q], qg_ref[...], ones_blk)
    q_ref[...] = (_rope(q, cos, sin, lo_mask) * (HEAD_DIM ** -0.5)).astype(BF16)
    k = _head_rms(y[:, SW_Q:SW_Q + SW_KV], kg_ref[...], ones_blk)
    k_ref[...] = _rope(k, cos, sin, lo_mask).astype(BF16)
    v_ref[...] = y[:, SW_Q + SW_KV:SW_Q + 2 * SW_KV].astype(BF16)


def _proj_odd(xa, mods, gn, w_bf, qg, kg, cos, sin, n_ctx_tiles):
    b, tt, d = xa.shape
    tm = TOKEN_TILE
    n = w_bf.shape[1]
    tok = lambda w: pl.BlockSpec((None, tm, w), lambda bi, t: (bi, t, 0))
    full = lambda shape: pl.BlockSpec(shape, lambda bi, t: (0,) * len(shape))
    pos = pl.BlockSpec((tm, LANES), lambda bi, t: (t, 0))
    widths = [SW_Q, SW_KV, SW_KV]
    return pl.pallas_call(
        _proj_odd_kernel,
        grid=(b, tt // tm),
        in_specs=[
            tok(d),
            pl.BlockSpec((None, 6, d), _mod_row_map(n_ctx_tiles, b)),
            full((1, d)), full((d, n)), full((1, SW_Q)), full((1, SW_KV)), pos, pos,
        ],
        out_specs=[tok(w) for w in widths],
        out_shape=[jax.ShapeDtypeStruct((b, tt, w), BF16) for w in widths],
        compiler_params=_params(("parallel", "parallel")),
        name="proj_odd",
    )(xa, mods, gn, w_bf, qg, kg, cos, sin)


def _gla_direction(backward, q_ref, k_ref, v_ref, g_ref, o_ref, st_ref):
    c = GLA_CHUNK
    assert c == GLA_DK
    n_sub = c // GLA_SUB
    n_chunks = q_ref.shape[0] // c
    sgn = -1 if backward else 1
    r = lax.broadcasted_iota(I32, (2 * c, c), 0)
    j = lax.broadcasted_iota(I32, (2 * c, c), 1)
    i = r % c
    incl = jnp.where(sgn * (i - j) >= 0, 1.0, 0.0)
    blk = jnp.where(sgn * (i // GLA_SUB - j // GLA_SUB) > 0, 1.0, 0.0)
    scan_mat = jnp.where(r >= c, blk, incl).astype(BF16)
    causal = scan_mat[0:c, :].astype(F32)
    row = lax.broadcasted_iota(I32, (c, 1), 0)
    reachable = [sgn * ((a * GLA_SUB if backward else a * GLA_SUB + GLA_SUB - 1) - row) >= 0 for a in range(n_sub)]
    lane_head = lax.broadcasted_iota(I32, (GLA_HEADS * c, GLA_QK), 1) // GLA_DK
    row_idx = lax.broadcasted_iota(I32, (GLA_HEADS * c, GLA_QK), 0)
    own_chunk = jnp.where(row_idx // c == lane_head, 1.0, 0.0).astype(BF16)
    sub_shape = (GLA_HEADS * GLA_SUB, GLA_QK)
    own_sub_f32 = jnp.where(lax.broadcasted_iota(I32, sub_shape, 0) // GLA_SUB
                            == lax.broadcasted_iota(I32, sub_shape, 1) // GLA_DK, 1.0, 0.0)
    own_sub = own_sub_f32.astype(BF16)
    ones_cols = jnp.ones((c, GLA_DV), BF16)
    last = 0 if backward else c - 1

    for u in range(n_chunks):
        ci = n_chunks - 1 - u if backward else u
        rows = slice(ci * c, (ci + 1) * c)
        g_hi, g_lo = _split_bf16(g_ref[rows, :])
        cr = _dot(scan_mat, g_hi) + _dot(scan_mat, g_lo)
        cum = cr[0:c, :]
        ref_pt = cr[c:2 * c, :]
        tot = cum[last:last + 1, :]
        tot_rows = _dot_tn(g_hi, ones_cols) + _dot_tn(g_lo, ones_cols)

        q = q_ref[rows, :]
        k = k_ref[rows, :]
        q_state = (q * jnp.exp(cum)).astype(BF16)
        q_sub = (q * jnp.exp(cum - ref_pt)).astype(BF16)
        k_state = (k * jnp.exp(tot - cum)).astype(BF16)

        placed = []
        for a in range(n_sub):
            lo = a * GLA_SUB
            k_sub = (k * jnp.exp(jnp.where(reachable[a], ref_pt[lo:lo + 1, :] - cum, 0.0))).astype(BF16)
            q4 = jnp.concatenate([q_sub[lo:lo + GLA_SUB, :]] * GLA_HEADS, axis=0) * own_sub
            a4 = _dot_nt(q4, k_sub) * jnp.concatenate([causal[lo:lo + GLA_SUB, :]] * GLA_HEADS, axis=0)
            placed.append(jnp.concatenate([a4] * GLA_HEADS, axis=1) * own_sub_f32)
        attn = jnp.concatenate([placed[a][h * GLA_SUB:(h + 1) * GLA_SUB, :]
                                for h in range(GLA_HEADS) for a in range(n_sub)], axis=0).astype(BF16)
        v = v_ref[rows, :]
        v_rows = jnp.concatenate([v[:, h * GLA_DV:(h + 1) * GLA_DV] for h in range(GLA_HEADS)], axis=0)
        q4s = jnp.concatenate([q_state] * GLA_HEADS, axis=0) * own_chunk
        st = st_ref[...]
        o = _dot(attn, v_rows) + _dot(q4s, st.astype(BF16))
        for h in range(GLA_HEADS):
            o_ref[rows, h * GLA_DV:(h + 1) * GLA_DV] = o[h * c:(h + 1) * c, :]
        kv = _dot_tn(k_state, v)
        kv_own = jnp.concatenate([kv[h * GLA_DK:(h + 1) * GLA_DK, h * GLA_DV:(h + 1) * GLA_DV]
                                  for h in range(GLA_HEADS)], axis=0)
        st_ref[...] = st * jnp.exp(tot_rows) + kv_own


def _gla_kernel(qf_ref, kf_ref, vf_ref, gf_ref, qb_ref, kb_ref, vb_ref, gb_ref, of_ref, ob_ref, stf_ref, stb_ref):
    @pl.when(pl.program_id(1) == 0)
    def _():
        stf_ref[...] = jnp.zeros_like(stf_ref)
        stb_ref[...] = jnp.zeros_like(stb_ref)

    _gla_direction(False, qf_ref, kf_ref, vf_ref, gf_ref, of_ref, stf_ref)
    _gla_direction(True, qb_ref, kb_ref, vb_ref, gb_ref, ob_ref, stb_ref)


def _gla(gq, gk, gv, la, n_ctx):
    b, tt, _ = gq.shape
    rows = TOKEN_TILE
    nb = tt // rows
    n_ctx_blk = n_ctx // rows

    def bwd(s):
        return jnp.where(s < n_ctx_blk, n_ctx_blk - 1 - s, nb - 1 + n_ctx_blk - s)

    def tok(w, col, order):
        return pl.BlockSpec((None, rows, w), lambda bi, s: (bi, order(s), col))

    fwd = lambda s: s
    return pl.pallas_call(
        _gla_kernel,
        grid=(b, nb),
        in_specs=[tok(GLA_QK, 0, fwd), tok(GLA_QK, 0, fwd), tok(GLA_V, 0, fwd), tok(GLA_QK, 0, fwd),
                  tok(GLA_QK, 0, bwd), tok(GLA_QK, 0, bwd), tok(GLA_V, 0, bwd), tok(GLA_QK, 1, bwd)],
        out_specs=[tok(GLA_V, 0, fwd), tok(GLA_V, 0, bwd)],
        out_shape=[jax.ShapeDtypeStruct((b, tt, GLA_V), F32)] * 2,
        scratch_shapes=[pltpu.VMEM((GLA_HEADS * GLA_DK, GLA_DV), F32)] * 2,
        compiler_params=_params(("arbitrary", "arbitrary")),
        name="gla_scan",
    )(gq, gk, gv, la, gq, gk, gv, la)


def _na_bias_table(rpb, kh):
    n_heads = rpb.shape[0]
    qc = np.arange(GRID_W)[:, None]
    kc = np.arange(GRID_W)[None, :]
    win0 = np.clip(qc - NA_KW // 2, 0, GRID_W - NA_KW)
    valid = (kc >= win0) & (kc < win0 + NA_KW)
    side = GRID_W - NA_KW
    rp = jnp.pad(rpb, ((0, 0), (0, 0), (side, side)))
    base = jnp.stack([rp[:, :, GRID_W - 1 - c:2 * GRID_W - 1 - c] for c in range(GRID_W)], axis=2)
    base = jnp.where(valid[None, None], base, NEG)
    tab = jnp.stack([base[:, NA_KH - 1 - a:NA_KH - 1 - a + kh] for a in range(kh)], axis=1)
    tab = tab.reshape(n_heads // 2, 2, kh, kh, GRID_W, GRID_W).transpose(0, 2, 1, 4, 3, 5)
    return tab.reshape(n_heads // 2, kh, 2 * GRID_W, kh * GRID_W).astype(F32)


def _na_kernel(q_ref, k_ref, v_ref, bias_ref, o_ref, *, n_ctx, n_ctx_blk, rows_per_blk, n_rows, kh):
    t = pl.program_id(2)
    lane = lax.broadcasted_iota(I32, (1, LANES), 1)
    first = lane < HEAD_DIM
    kc = k_ref[0:n_ctx, :]
    vc = v_ref[0:n_ctx, :]

    def stack_heads(q):
        z = jnp.zeros_like(q)
        return jnp.concatenate([jnp.where(first, q, z), jnp.where(first, z, q)], axis=0)

    def unstack_heads(o):
        n = o.shape[0] // 2
        return jnp.where(first, o[0:n], o[n:2 * n])

    @pl.when(t < n_ctx_blk)
    def _():
        s = _dot_nt(stack_heads(q_ref[...]), kc)
        m = jnp.max(s, axis=-1, keepdims=True)
        p = jnp.exp(s - m)
        den = jnp.sum(p, axis=-1, keepdims=True)
        o_ref[...] = unstack_heads(_dot(p.astype(BF16), vc) / den).astype(o_ref.dtype)

    @pl.when(t >= n_ctx_blk)
    def _():
        for i in range(rows_per_blk):
            r = (t - n_ctx_blk) * rows_per_blk + i
            row0 = jnp.clip(r - kh // 2, 0, n_rows - kh)
            cls = r - row0
            start = pl.multiple_of(n_ctx + row0 * GRID_W, GRID_W)
            kb = k_ref[pl.ds(start, kh * GRID_W), :]
            vb = v_ref[pl.ds(start, kh * GRID_W), :]
            qs = stack_heads(q_ref[i * GRID_W:(i + 1) * GRID_W, :])
            s = _dot_nt(qs, kb) + bias_ref[cls]
            sc = _dot_nt(qs, kc)
            m = jnp.maximum(jnp.max(s, axis=-1, keepdims=True), jnp.max(sc, axis=-1, keepdims=True))
            p = jnp.exp(s - m)
            pc = jnp.exp(sc - m)
            den = jnp.sum(p, axis=-1, keepdims=True) + jnp.sum(pc, axis=-1, keepdims=True)
            o = (_dot(p.astype(BF16), vb) + _dot(pc.astype(BF16), vc)) / den
            o_ref[i * GRID_W:(i + 1) * GRID_W, :] = unstack_heads(o).astype(o_ref.dtype)


def _na(nq, nk, nv, bias_tab, n_ctx):
    b, tt, _ = nq.shape
    tq = TOKEN_TILE
    n_rows = (tt - n_ctx) // GRID_W
    kh = bias_tab.shape[1]
    kern = functools.partial(_na_kernel, n_ctx=n_ctx, n_ctx_blk=n_ctx // tq, rows_per_blk=tq // GRID_W,
                             n_rows=n_rows, kh=kh)
    return pl.pallas_call(
        kern,
        grid=(b, NA_HEADS // 2, tt // tq),
        in_specs=[
            pl.BlockSpec((None, tq, LANES), lambda bi, p, t: (bi, t, p)),
            pl.BlockSpec((None, tt, LANES), lambda bi, p, t: (bi, 0, p)),
            pl.BlockSpec((None, tt, LANES), lambda bi, p, t: (bi, 0, p)),
            pl.BlockSpec((None, kh, 2 * GRID_W, kh * GRID_W), lambda bi, p, t: (p, 0, 0, 0)),
        ],
        out_specs=pl.BlockSpec((None, tq, LANES), lambda bi, p, t: (bi, t, p)),
        out_shape=jax.ShapeDtypeStruct((b, tt, NA_W), BF16),
        compiler_params=_params(("parallel", "parallel", "arbitrary")),
        name="na_attn",
    )(nq, nk, nv, bias_tab)


def _swa_kernel(sink_ref, q_ref, k_ref, v_ref, o_ref, *, n_ctx, n_lat_blk):
    blk = SW_BLOCK
    n = pl.program_id(1)
    n_ctx_blk = n_ctx // blk
    jb = n - n_ctx_blk
    sb = jnp.clip(jb - 1, 0, n_lat_blk - 3)
    start = pl.multiple_of(n_ctx + sb * blk, blk)
    keys = jnp.concatenate([k_ref[pl.ds(start, 3 * blk), :], k_ref[0:n_ctx, :]], axis=0)
    vals = jnp.concatenate([v_ref[pl.ds(start, 3 * blk), :], v_ref[0:n_ctx, :]], axis=0)
    nk = 3 * blk + n_ctx
    iq = lax.broadcasted_iota(I32, (blk, nk), 0)
    ik = lax.broadcasted_iota(I32, (blk, nk), 1)
    offset = jnp.where(n < n_ctx_blk, 4 * SW_WINDOW, (jb - sb) * blk)
    dist = jnp.where(ik >= 3 * blk, 0, iq - ik + offset)
    bias = jnp.where(jnp.abs(dist) <= SW_WINDOW, 0.0, NEG)
    q = q_ref[...]
    group = SW_HEADS // SW_KV_HEADS
    bias_g = jnp.concatenate([bias] * group, axis=0)
    lane = lax.broadcasted_iota(I32, (1, LANES), 1)
    assert SW_KV_HEADS * HEAD_DIM == LANES
    for g in range(SW_KV_HEADS):
        gs = slice(g * HEAD_DIM, (g + 1) * HEAD_DIM)
        in_g = (lane // HEAD_DIM) == g
        heads = range(g * group, (g + 1) * group)
        blocks = []
        for h in heads:
            two = q[:, (h // 2) * LANES:(h // 2 + 1) * LANES]
            if h % 2 != g:
                two = jnp.concatenate([two[:, HEAD_DIM:], two[:, :HEAD_DIM]], axis=1)
            blocks.append(jnp.where(in_g, two, jnp.zeros_like(two)))
        qg = jnp.concatenate(blocks, axis=0)
        sink = jnp.concatenate([jnp.full((blk, 1), sink_ref[h], F32) for h in heads], axis=0)
        s = _dot_nt(qg, keys) + bias_g
        m = jnp.maximum(jnp.max(s, axis=-1, keepdims=True), sink)
        p = jnp.exp((s - m).astype(BF16))
        o2 = _dot(p, jnp.where(in_g, vals, jnp.ones_like(vals)))
        other = (1 - g) * HEAD_DIM
        den = o2[:, other:other + 1] + jnp.exp(sink - m)
        o = o2[:, gs] / den
        for pr in range(group // 2):
            pair = [o[(2 * pr + e) * blk:(2 * pr + e + 1) * blk, :] for e in range(2)]
            col = (g * group // 2 + pr) * LANES
            o_ref[:, col:col + LANES] = jnp.concatenate(pair, axis=1).astype(o_ref.dtype)


def _swa(q, k, v, sinks, n_ctx):
    b, tt, _ = q.shape
    blk = SW_BLOCK
    kern = functools.partial(_swa_kernel, n_ctx=n_ctx, n_lat_blk=(tt - n_ctx) // blk)
    return pl.pallas_call(
        kern,
        grid=(b, tt // blk),
        in_specs=[
            pl.BlockSpec(memory_space=pltpu.SMEM),
            pl.BlockSpec((None, blk, SW_Q), lambda bi, n: (bi, n, 0)),
            pl.BlockSpec((None, tt, SW_KV), lambda bi, n: (bi, 0, 0)),
            pl.BlockSpec((None, tt, SW_KV), lambda bi, n: (bi, 0, 0)),
        ],
        out_specs=pl.BlockSpec((None, blk, SW_Q), lambda bi, n: (bi, n, 0)),
        out_shape=jax.ShapeDtypeStruct((b, tt, SW_Q), BF16),
        compiler_params=_params(("parallel", "arbitrary")),
        name="swa_attn",
    )(sinks, q, k, v)


def _route(f, wr_ref, br_ref, before_ref, run_ref, idx_ref, wt_ref, rank_ref, cnt_ref, base_ref):
    tm = f.shape[0]
    f_hi, f_lo = _split_bf16(f)
    logits = _dot(f_hi, wr_ref[0]) + (_dot(f_lo, wr_ref[0]) + _dot(f_hi, wr_ref[1])) + br_ref[...]
    e_iota = lax.broadcasted_iota(I32, (tm, N_EXPERTS), 1).astype(F32)
    lane8 = lax.broadcasted_iota(I32, (tm, 8), 1)
    vals, hots = [], []
    idx_out = jnp.zeros((tm, 8), I32)
    for k in range(TOP_K):
        mx = jnp.max(logits, axis=-1, keepdims=True)
        am = jnp.min(jnp.where(logits == mx, e_iota, float(N_EXPERTS)), axis=-1, keepdims=True)
        hot = e_iota == am
        vals.append(mx)
        hots.append(hot)
        idx_out = jnp.where(lane8 == k, am.astype(I32), idx_out)
        logits = jnp.where(hot, -jnp.inf, logits)
    exps = [jnp.exp(v - vals[0]) for v in vals]
    den = exps[0] + exps[1] + exps[2] + exps[3]
    multihot = jnp.zeros((tm, N_EXPERTS), F32)
    wt_out = jnp.zeros((tm, 8), F32)
    for k in range(TOP_K):
        multihot = multihot + jnp.where(hots[k], 1.0, 0.0)
        wt_out = jnp.where(lane8 == k, exps[k] / den, wt_out)
    base = _dot(before_ref[...], multihot.astype(BF16)) + run_ref[...]
    rank_out = jnp.zeros((tm, 8), I32)
    for k in range(TOP_K):
        rk = jnp.sum(jnp.where(hots[k], base, 0.0), axis=-1, keepdims=True)
        rank_out = jnp.where(lane8 == k, rk.astype(I32), rank_out)
    base_ref[...] = run_ref[...]
    run_ref[...] = run_ref[...] + jnp.sum(multihot, axis=0, keepdims=True)
    idx_ref[...] = idx_out
    wt_ref[...] = wt_out
    rank_ref[...] = rank_out
    cnt_ref[...] = run_ref[...]


def _post_common(mix, x_ref, m_ref, fg_ref, wr_ref, br_ref, before_ref, xn_ref, f_ref, idx_ref, wt_ref, rank_ref,
                 cnt_ref, base_ref, run_ref):
    @pl.when((pl.program_id(0) == 0) & (pl.program_id(1) == 0))
    def _():
        run_ref[...] = jnp.zeros_like(run_ref)

    xn = x_ref[...] + m_ref[2:3, :] * mix
    xn_ref[...] = xn
    f = _norm_mod(xn, fg_ref[...], m_ref, 3, 4)
    f_ref[...] = f.astype(f_ref.dtype)
    _route(f, wr_ref, br_ref, before_ref, run_ref, idx_ref, wt_ref, rank_ref, cnt_ref, base_ref)


def _post_even_kernel(of_ref, ob_ref, gate_ref, na_ref, gng_ref, w_ref, x_ref, m_ref, fg_ref, wr_ref, br_ref,
                      before_ref, xn_ref, f_ref, idx_ref, wt_ref, rank_ref, cnt_ref, base_ref, run_ref):
    o = of_ref[...] + ob_ref[...]
    gate = gate_ref[...].astype(F32)
    gn = gng_ref[...]
    parts = []
    for h in range(GLA_HEADS):
        oh = o[:, h * GLA_DV:(h + 1) * GLA_DV]
        ms = jnp.mean(oh * oh, axis=-1, keepdims=True)
        parts.append(oh * lax.rsqrt(ms + EPS) * gn)
    gla = jnp.concatenate(parts, axis=1) * (gate * jax.nn.sigmoid(gate))
    mix = _dot(gla.astype(BF16), w_ref[0:GLA_V, :]) + _dot(na_ref[...], w_ref[GLA_V:GLA_V + NA_W, :])
    _post_common(mix, x_ref, m_ref, fg_ref, wr_ref, br_ref, before_ref, xn_ref, f_ref, idx_ref, wt_ref, rank_ref,
                 cnt_ref, base_ref, run_ref)


def _post_odd_kernel(o_ref, w_ref, x_ref, m_ref, fg_ref, wr_ref, br_ref, before_ref,
                     xn_ref, f_ref, idx_ref, wt_ref, rank_ref, cnt_ref, base_ref, run_ref):
    mix = _dot(o_ref[...], w_ref[...])
    _post_common(mix, x_ref, m_ref, fg_ref, wr_ref, br_ref, before_ref, xn_ref, f_ref, idx_ref, wt_ref, rank_ref,
                 cnt_ref, base_ref, run_ref)


def _post(even, mixer_outs, extra, w_bf, xa, mods, fg, wr, br, n_ctx_tiles):
    b, tt, d = xa.shape
    tm = TOKEN_TILE
    tok = lambda w: pl.BlockSpec((None, tm, w), lambda bi, t: (bi, t, 0))
    full = lambda shape: pl.BlockSpec(shape, lambda bi, t: (0,) * len(shape))
    if even:
        o_fwd, o_bwd, gate, na = mixer_outs
        lead_specs = [tok(GLA_V), tok(GLA_V), tok(GLA_V), tok(NA_W), full((1, GLA_DV))]
        lead_args = [o_fwd, o_bwd, gate, na, extra]
        kern = _post_even_kernel
    else:
        (o,) = mixer_outs
        lead_specs = [tok(SW_Q)]
        lead_args = [o]
        kern = _post_odd_kernel
    nt = tt // tm
    small = lambda: pl.BlockSpec((None, tm, 8), lambda bi, t: (bi, t, 0))
    before = jnp.asarray(np.tril(np.ones((tm, tm), np.float32), -1), BF16)
    return pl.pallas_call(
        kern,
        grid=(b, tt // tm),
        in_specs=lead_specs + [
            full(w_bf.shape), tok(d),
            pl.BlockSpec((None, 6, d), _mod_row_map(n_ctx_tiles, b)),
            full((1, d)), full((2, d, N_EXPERTS)), full((1, N_EXPERTS)), full((tm, tm)),
        ],
        out_specs=[tok(d), tok(d), small(), small(), small(), full((1, N_EXPERTS)),
                   pl.BlockSpec((None, 1, N_EXPERTS), lambda bi, t: (bi * nt + t, 0, 0))],
        out_shape=[
            jax.ShapeDtypeStruct((b, tt, d), F32), jax.ShapeDtypeStruct((b, tt, d), BF16),
            jax.ShapeDtypeStruct((b, tt, 8), I32), jax.ShapeDtypeStruct((b, tt, 8), F32),
            jax.ShapeDtypeStruct((b, tt, 8), I32), jax.ShapeDtypeStruct((1, N_EXPERTS), F32),
            jax.ShapeDtypeStruct((b * nt, 1, N_EXPERTS), F32),
        ],
        scratch_shapes=[pltpu.VMEM((1, N_EXPERTS), F32)],
        compiler_params=_params(("arbitrary", "arbitrary")),
        name="post_even" if even else "post_odd",
    )(*lead_args, w_bf, xa, mods, fg, wr, br, before)


def _stage_rows(tm):
    return tm * TOP_K + N_EXPERTS * (MOE_CHUNK + SUBLANES)


WAIT_BATCH = 4


def _for_tile_chunks(rows_ref, tot_ref, g, max_chunks, fn):
    def per_chunk(j, c):
        fn(pl.multiple_of(rows_ref[g * max_chunks + j], SUBLANES), pl.multiple_of(j * MOE_CHUNK, MOE_CHUNK))
        return c

    lax.fori_loop(0, tot_ref[g], per_chunk, 0)


def _wait_chunks(n, make_copy):
    def batch(i, c):
        make_copy(WAIT_BATCH * MOE_CHUNK).wait()
        return c

    def single(i, c):
        make_copy(MOE_CHUNK).wait()
        return c

    lax.fori_loop(0, n // WAIT_BATCH, batch, 0)
    lax.fori_loop(0, n % WAIT_BATCH, single, 0)


def _dispatch_kernel(rows_ref, tot_ref, nch_ref, off_ref, lead_ref, cnt_ref, lo_ref, hi_ref, colt_ref, f_ref, xs_ref,
                     stage, carry, zeros_ref, sems, zsem):
    g = pl.program_id(0)
    n_tiles = pl.num_programs(0)
    ch = MOE_CHUNK
    buf = g % 2
    max_chunks = stage.shape[1] // ch

    def chunk_copy(which, slot_row, stage_row, rows=ch):
        return pltpu.make_async_copy(stage.at[which, pl.ds(stage_row, rows)], xs_ref.at[pl.ds(slot_row, rows)],
                                     sems.at[which])

    def wait_tile(tile, which):
        _wait_chunks(tot_ref[tile], lambda rows: chunk_copy(which, 0, 0, rows))

    @pl.when(g == 0)
    def _():
        carry[...] = jnp.zeros_like(carry)
        zeros_ref[...] = jnp.zeros_like(zeros_ref)

        def pad_pass(do):
            def per_expert(e, c):
                lo = lo_ref[e]
                hi = hi_ref[e]
                n_full = (hi - lo) // ch

                def full(j, c2):
                    do(pltpu.make_async_copy(
                        zeros_ref, xs_ref.at[pl.ds(pl.multiple_of(hi - (j + 1) * ch, SUBLANES), ch)], zsem))
                    return c2

                lax.fori_loop(0, n_full, full, 0)

                @pl.when((hi - lo) - n_full * ch > 0)
                def _():
                    do(pltpu.make_async_copy(zeros_ref.at[pl.ds(0, SUBLANES)],
                                             xs_ref.at[pl.ds(pl.multiple_of(lo, SUBLANES), SUBLANES)], zsem))
                return c

            lax.fori_loop(0, lo_ref.shape[0], per_expert, 0)

        pad_pass(lambda cp: cp.start())
        pad_pass(lambda cp: cp.wait())

    colt = colt_ref[...]
    rs = stage.shape[1]
    r_iota = lax.broadcasted_iota(I32, (rs, colt.shape[1]), 0)
    hit = jnp.zeros(r_iota.shape, F32)
    for k in range(TOP_K):
        hit = jnp.where(r_iota == colt[k:k + 1, :], 1.0, hit)
    stage[buf] = _dot(hit.astype(BF16), f_ref[...])

    for e in range(N_EXPERTS):
        i = g * N_EXPERTS + e
        active = nch_ref[i] > 0
        ptr = off_ref[i]
        head = pl.ds(pl.multiple_of(ptr, SUBLANES), SUBLANES)
        stage[buf, head, :] = stage[buf, head, :] + jnp.where(active, carry[e], 0.0)
        end = lead_ref[i] + cnt_ref[i]
        new_lead = end % SUBLANES
        tail = pl.ds(pl.multiple_of(ptr + end - new_lead, SUBLANES), SUBLANES)
        carry[e] = jnp.where(active, jnp.where(new_lead > 0, stage[buf, tail, :], 0.0), carry[e])

    @pl.when(g > 0)
    def _():
        wait_tile(g - 1, 1 - buf)

    _for_tile_chunks(rows_ref, tot_ref, g, max_chunks,
                     lambda slot_row, stage_row: chunk_copy(buf, slot_row, stage_row).start())

    @pl.when(g == n_tiles - 1)
    def _():
        wait_tile(g, buf)


def _dispatch(rows, tot, nch, off, lead, cnt, pad_lo, pad_hi, colt, f2, n_slots):
    t, d = f2.shape
    tm = TOKEN_TILE
    rs = _stage_rows(tm)
    return pl.pallas_call(
        _dispatch_kernel,
        grid_spec=pltpu.PrefetchScalarGridSpec(
            num_scalar_prefetch=8,
            grid=(t // tm,),
            in_specs=[
                pl.BlockSpec((None, 8, tm), lambda g, *_: (g, 0, 0)),
                pl.BlockSpec((tm, d), lambda g, *_: (g, 0)),
            ],
            out_specs=pl.BlockSpec(memory_space=pl.ANY),
            scratch_shapes=[pltpu.VMEM((2, rs, d), F32), pltpu.VMEM((N_EXPERTS, SUBLANES, d), F32),
                            pltpu.VMEM((MOE_CHUNK, d), F32),
                            pltpu.SemaphoreType.DMA((2,)), pltpu.SemaphoreType.DMA(())],
        ),
        out_shape=jax.ShapeDtypeStruct((n_slots, d), F32),
        compiler_params=_params(("arbitrary",)),
        name="moe_dispatch",
    )(rows, tot, nch, off, lead, cnt, pad_lo, pad_hi, colt, f2)


def _expert_kernel(be_ref, nu_ref, xs_ref, wgu_ref, bgu_ref, wd_ref, bd_ref, ys_ref, wgu_bf, wd_bf):
    i = pl.program_id(0)
    prev = be_ref[jnp.maximum(i - 1, 0)]
    fresh = (i == 0) | (be_ref[i] != prev)

    @pl.when(fresh & (i < nu_ref[0]))
    def _():
        wgu_bf[...] = wgu_ref[...].astype(BF16)
        wd_bf[...] = wd_ref[...].astype(BF16)

    @pl.when(i < nu_ref[0])
    def _():
        x = xs_ref[...].astype(BF16)
        de = wd_bf.shape[0]
        half = de // 2
        acc = None
        for c in range(2):
            glu = _dot(x, wgu_bf[:, c * half:(c + 1) * half]) + bgu_ref[:, c * half:(c + 1) * half]
            lin = _dot(x, wgu_bf[:, de + c * half:de + (c + 1) * half]) + bgu_ref[:, de + c * half:de + (c + 1) * half]
            glu = jnp.minimum(glu, SWIGLU_LIMIT)
            lin = jnp.clip(lin, -SWIGLU_LIMIT, SWIGLU_LIMIT)
            act = (glu * jax.nn.sigmoid(SWIGLU_ALPHA * glu) * (lin + 1.0)).astype(BF16)
            part = _dot(act, wd_bf[c * half:(c + 1) * half, :])
            acc = part if acc is None else acc + part
        ys_ref[...] = acc + bd_ref[...]

    @pl.when(i >= nu_ref[0])
    def _():
        ys_ref[...] = jnp.zeros_like(ys_ref)


def _experts(block_e, n_used, xs, layer, wgu, bgu, wd, bd):
    n_slots, d = xs.shape
    bm = EXPERT_BLOCK
    depth, ne, _, de2 = wgu.shape
    de = wd.shape[2]
    row_blk = lambda i, be, nu: (jnp.minimum(i, nu[0] - 1), 0)
    exp_blk = lambda i, be, nu: (layer, be[i], 0, 0)
    return pl.pallas_call(
        _expert_kernel,
        grid_spec=pltpu.PrefetchScalarGridSpec(
            num_scalar_prefetch=2,
            grid=(n_slots // bm,),
            in_specs=[
                pl.BlockSpec((bm, d), row_blk),
                pl.BlockSpec((None, None, d, de2), exp_blk),
                pl.BlockSpec((None, None, 1, de2), exp_blk),
                pl.BlockSpec((None, None, de, d), exp_blk),
                pl.BlockSpec((None, None, 1, d), exp_blk),
            ],
            out_specs=pl.BlockSpec((bm, d), lambda i, be, nu: (i, 0)),
            scratch_shapes=[pltpu.VMEM((d, de2), BF16), pltpu.VMEM((de, d), BF16)],
        ),
        out_shape=jax.ShapeDtypeStruct((n_slots, d), F32),
        compiler_params=_params(("arbitrary",)),
        name="moe_experts",
    )(block_e, n_used, xs, wgu, bgu.reshape(depth, ne, 1, de2), wd, bd.reshape(depth, ne, 1, d))


def _combine_kernel(rows_ref, tot_ref, col_ref, wt_ref, x_ref, m_ref, ys_ref, o_ref, stage, sems):
    g = pl.program_id(0)
    n_tiles = pl.num_programs(0)
    ch = MOE_CHUNK
    buf = g % 2
    max_chunks = stage.shape[1] // ch

    def chunk_copy(which, slot_row, stage_row, rows=ch):
        return pltpu.make_async_copy(ys_ref.at[pl.ds(slot_row, rows)], stage.at[which, pl.ds(stage_row, rows)],
                                     sems.at[which])

    def fetch_tile(tile, which):
        _for_tile_chunks(rows_ref, tot_ref, tile, max_chunks,
                         lambda slot_row, stage_row: chunk_copy(which, slot_row, stage_row).start())

    @pl.when(g == 0)
    def _():
        stage[...] = jnp.zeros_like(stage)
        fetch_tile(0, 0)

    @pl.when(g + 1 < n_tiles)
    def _():
        fetch_tile(g + 1, 1 - buf)

    _wait_chunks(tot_ref[g], lambda rows: chunk_copy(buf, 0, 0, rows))

    col = col_ref[...]
    wt = wt_ref[...]
    tm = col.shape[0]
    group = 2 * LANES
    y = jnp.zeros(x_ref.shape, F32)
    for j in range(stage.shape[1] // group):
        c_iota = lax.broadcasted_iota(I32, (tm, group), 1) + j * group
        w = jnp.zeros((tm, group), F32)
        for k in range(TOP_K):
            w = jnp.where(c_iota == col[:, k:k + 1], wt[:, k:k + 1], w)
        rows = stage[buf, j * group:(j + 1) * group, :].astype(BF16)
        y = y + _dot(w.astype(BF16), rows)
    o_ref[...] = x_ref[...] + m_ref[5:6, :] * y


def _combine(rows, tot, col, wts, xn, mods, ys, n_ctx_tiles):
    b, tt, d = xn.shape
    tm = TOKEN_TILE
    nt = tt // tm
    rs = _stage_rows(tm)
    assert rs % (2 * LANES) == 0
    return pl.pallas_call(
        _combine_kernel,
        grid_spec=pltpu.PrefetchScalarGridSpec(
            num_scalar_prefetch=2,
            grid=(b * nt,),
            in_specs=[
                pl.BlockSpec((tm, 8), lambda g, *_: (g, 0)),
                pl.BlockSpec((tm, 8), lambda g, *_: (g, 0)),
                pl.BlockSpec((tm, d), lambda g, *_: (g, 0)),
                pl.BlockSpec((None, 6, d), lambda g, *_: (jnp.where(g % nt < n_ctx_tiles, b, g // nt), 0, 0)),
                pl.BlockSpec(memory_space=pl.ANY),
            ],
            out_specs=pl.BlockSpec((tm, d), lambda g, *_: (g, 0)),
            scratch_shapes=[pltpu.VMEM((2, rs, d), F32), pltpu.SemaphoreType.DMA((2,))],
        ),
        out_shape=jax.ShapeDtypeStruct((b * tt, d), F32),
        compiler_params=_params(("arbitrary",)),
        name="moe_combine",
    )(rows, tot, col, wts.reshape(b * tt, 8), xn.reshape(b * tt, d), mods, ys).reshape(b, tt, d)


def _moe(xn, f, idx, wts, rank, cnt, tile_base, mods, layer, wgu, bgu, wd, bd, n_ctx_tiles):
    b, tt, d = xn.shape
    t = b * tt
    bm = EXPERT_BLOCK
    tm = TOKEN_TILE
    ch = MOE_CHUNK
    n_tiles = t // tm
    n_blocks = (t * TOP_K + N_EXPERTS * ch + bm - 1) // bm + N_EXPERTS
    e_ids = np.arange(N_EXPERTS)
    lower = e_ids[None, :] < e_ids[:, None]
    counts = cnt[0].astype(I32)
    padded = (counts + ch + bm - 1) // bm * bm
    pstart = jnp.sum(jnp.where(lower, padded[None, :], 0), axis=1)
    pend = pstart + padded
    base = tile_base.reshape(n_tiles, N_EXPERTS).astype(I32)
    tile_cnt = jnp.concatenate([base[1:], counts[None, :]], axis=0) - base
    first = pstart[None, :] + base
    seg = first // SUBLANES * SUBLANES
    lead = first - seg
    nch = jnp.where(tile_cnt > 0, (lead + tile_cnt + ch - 1) // ch, 0)
    off = jnp.sum(jnp.where(lower[None], nch[:, None, :], 0), axis=2) * ch
    idx4 = idx[..., :TOP_K].reshape(n_tiles, tm, TOP_K)
    rank4 = rank[..., :TOP_K].reshape(n_tiles, tm, TOP_K)
    delta = (off + lead - base)[:, None, None, :]
    col = rank4 + jnp.sum(jnp.where(idx4[..., None] == e_ids, delta, 0), axis=-1)
    col = jnp.concatenate([col, jnp.full((n_tiles, tm, 8 - TOP_K), -1, I32)], axis=-1)
    colt = col.transpose(0, 2, 1)
    n_used = (pend[N_EXPERTS - 1] // bm).reshape(1).astype(I32)
    starts = np.arange(n_blocks, dtype=np.int32)[:, None] * bm
    block_e = jnp.minimum(jnp.sum((pend[None, :] <= starts).astype(I32), axis=1), N_EXPERTS - 1)
    pad_lo = jnp.concatenate([(pstart + counts + SUBLANES - 1) // SUBLANES * SUBLANES, pend[N_EXPERTS - 1:]])
    pad_hi = jnp.concatenate([pend, jnp.full((1,), n_blocks * bm, I32)])
    max_chunks = _stage_rows(tm) // ch
    cum = off // ch + nch
    j = np.arange(max_chunks, dtype=np.int32)
    e_of_j = jnp.sum((cum[:, None, :] <= j[None, :, None]).astype(I32), axis=-1)
    hot = e_of_j[..., None] == e_ids
    seg_j = jnp.sum(jnp.where(hot, seg[:, None, :], 0), axis=-1)
    first_j = jnp.sum(jnp.where(hot, (off // ch)[:, None, :], 0), axis=-1)
    rows = (seg_j + (j[None, :] - first_j) * ch).reshape(-1)
    tot = cum[:, N_EXPERTS - 1]
    xs = _dispatch(rows, tot, nch.reshape(-1), off.reshape(-1), lead.reshape(-1), tile_cnt.reshape(-1),
                   pad_lo, pad_hi, colt, f.reshape(t, d), n_blocks * bm)
    ys = _experts(block_e, n_used, xs, layer, wgu, bgu, wd, bd)
    return _combine(rows, tot, col.reshape(t, 8), wts, xn, mods, ys, n_ctx_tiles)


def _rope_tables(n_ctx, s_len):
    quarter = HEAD_DIM // 4
    t = np.arange(s_len)
    pos = np.stack([t // GRID_W, t % GRID_W], axis=0).astype(np.float32)
    inv_freq = ROPE_BASE ** (-jnp.arange(quarter, dtype=F32) / quarter)
    ang = jnp.asarray(pos)[:, :, None] * inv_freq
    cos, sin = jnp.cos(ang), jnp.sin(ang)
    cos_h = jnp.concatenate([cos[0], cos[0], cos[1], cos[1]], axis=-1)
    sin_h = jnp.concatenate([-sin[0], sin[0], -sin[1], sin[1]], axis=-1)
    reps = LANES // HEAD_DIM
    cos_l = jnp.tile(cos_h, (1, reps))
    sin_l = jnp.tile(sin_h, (1, reps))
    cos_all = jnp.concatenate([jnp.ones((n_ctx, LANES), F32), cos_l], axis=0)
    sin_all = jnp.concatenate([jnp.zeros((n_ctx, LANES), F32), sin_l], axis=0)
    return cos_all, sin_all


def kernel(x, c, ctx, c_ctx, mod_w, mod_b, norm_mix_g, norm_ffn_g, ev_w_in, ev_w_out, ev_gla_w_a2, ev_gla_b_a2,
           ev_gla_norm_g, ev_na_q_g, ev_na_k_g, ev_na_rpb, od_w_in, od_w_out, od_q_g, od_k_g, od_sinks,
           moe_w_router, moe_b_router, moe_w_gate_up, moe_b_gate_up, moe_w_down, moe_b_down):
    b, s_len, d = x.shape
    n_ctx = ctx.shape[1]
    depth = mod_w.shape[0]
    assert b < MOD_ROWS and n_ctx % TOKEN_TILE == 0 and s_len % TOKEN_TILE == 0
    n_ctx_tiles = n_ctx // TOKEN_TILE
    n_rows = s_len // GRID_W
    kh = min(NA_KH, n_rows)

    xa = jnp.concatenate([ctx, x], axis=1)
    cv = jnp.zeros((MOD_ROWS, d), F32).at[:b].set(c).at[b].set(c_ctx)
    mods_all = _modulation(cv, mod_w, mod_b).reshape(depth, MOD_ROWS, 6, d)
    cos, sin = _rope_tables(n_ctx, s_len)

    for layer in range(depth):
        i = layer // 2
        mods = mods_all[layer]
        gn = norm_mix_g[layer].reshape(1, d)
        if layer % 2 == 0:
            w = ev_w_in[i]
            cuts = np.cumsum([0, GLA_QK, GLA_QK, GLA_V, GLA_V, GLA_RANK, GLA_RANK, NA_W, NA_W, NA_W])
            seg = lambda j: w[:, cuts[j]:cuts[j + 1]]
            pad = jnp.zeros((d, LANES - 2 * GLA_RANK), F32)
            w_bf = jnp.concatenate([seg(0), seg(1), seg(2), seg(3), seg(6), seg(7), seg(8), seg(4), seg(5), pad],
                                   axis=1).astype(BF16)
            a2 = jnp.zeros((LANES, 2 * GLA_QK), F32)
            a2 = a2.at[0:GLA_RANK, 0:GLA_QK].set(ev_gla_w_a2[i, 0])
            a2 = a2.at[GLA_RANK:2 * GLA_RANK, GLA_QK:].set(ev_gla_w_a2[i, 1])
            ba2 = ev_gla_b_a2[i].reshape(1, 2 * GLA_QK)
            qg = jnp.tile(ev_na_q_g[i], NA_HEADS).reshape(1, NA_W)
            kg = jnp.tile(ev_na_k_g[i], NA_HEADS).reshape(1, NA_W)
            gq, gk, gv, gate, la, nq, nk, nv = _proj_even(xa, mods, gn, w_bf, a2, ba2, qg, kg, n_ctx_tiles)
            o_fwd, o_bwd = _gla(gq, gk, gv, la, n_ctx)
            o_na = _na(nq, nk, nv, _na_bias_table(ev_na_rpb[i], kh), n_ctx)
            mixer_outs = (o_fwd, o_bwd, gate, o_na)
            extra = ev_gla_norm_g[i].reshape(1, GLA_DV)
            w_out = ev_w_out[i].astype(BF16)
        else:
            w_bf = od_w_in[i].astype(BF16)
            qg = jnp.tile(od_q_g[i], SW_HEADS).reshape(1, SW_Q)
            kg = jnp.tile(od_k_g[i], SW_KV_HEADS).reshape(1, SW_KV)
            q, k, v = _proj_odd(xa, mods, gn, w_bf, qg, kg, cos, sin, n_ctx_tiles)
            mixer_outs = (_swa(q, k, v, od_sinks[i], n_ctx),)
            extra = None
            w_out = od_w_out[i].astype(BF16)
        xn, f, idx, wts, rank, cnt, tile_base = _post(
            layer % 2 == 0, mixer_outs, extra, w_out, xa, mods, norm_ffn_g[layer].reshape(1, d),
            jnp.stack(_split_bf16(moe_w_router[layer])), moe_b_router[layer].reshape(1, N_EXPERTS), n_ctx_tiles)
        xa = _moe(xn, f, idx, wts, rank, cnt, tile_base, mods, layer, moe_w_gate_up, moe_b_gate_up, moe_w_down, moe_b_down,
                  n_ctx_tiles)
    return xa[:, n_ctx:, :]
```

```python
import functools

import jax
import jax.numpy as jnp
import numpy as np
from jax import lax
from jax.experimental import pallas as pl
from jax.experimental.pallas import tpu as pltpu

F32 = jnp.float32
BF16 = jnp.bfloat16
I32 = jnp.int32
HIGHEST = lax.Precision.HIGHEST

GRID_W = 64
HEAD_DIM = 64
EPS = 1e-6
GLA_HEADS = 4
GLA_DK = 64
GLA_DV = 128
GLA_RANK = 16
GLA_TAU = 16.0
GLA_CHUNK = 64
GLA_SUB = 16
NA_HEADS = 8
NA_KH = 8
NA_KW = 16
SW_HEADS = 16
SW_KV_HEADS = 2
SW_WINDOW = 128
SW_BLOCK = 128
ROPE_BASE = 10000.0
N_EXPERTS = 32
TOP_K = 4
SWIGLU_LIMIT = 7.0
SWIGLU_ALPHA = 1.702

GLA_QK = GLA_HEADS * GLA_DK
GLA_V = GLA_HEADS * GLA_DV
NA_W = NA_HEADS * HEAD_DIM
SW_Q = SW_HEADS * HEAD_DIM
SW_KV = SW_KV_HEADS * HEAD_DIM

LANES = 128
MOE_ALIGN = 16
VMEM_LIMIT_BYTES = 56 * 1024 * 1024
NEG = -1e30

TOKEN_TILE = 256
EXPERT_BLOCK = 512
MOE_CHUNK = 16
MOD_ROWS = 16


def _dot(a, b, **kw):
    return jnp.dot(a, b, preferred_element_type=F32, **kw)


def _dot_nt(a, b):
    return lax.dot_general(a, b, (((1,), (1,)), ((), ())), preferred_element_type=F32)


def _dot_tn(a, b):
    return lax.dot_general(a, b, (((0,), (0,)), ((), ())), preferred_element_type=F32)


def _split_bf16(x):
    hi = x.astype(BF16)
    lo = (x - hi.astype(F32)).astype(BF16)
    return hi, lo


def _params(sem, vmem=VMEM_LIMIT_BYTES):
    return pltpu.CompilerParams(dimension_semantics=sem, vmem_limit_bytes=vmem)


def _mod_kernel(cv_ref, w_ref, b_ref, o_ref):
    cv = cv_ref[...]
    s = cv * jax.nn.sigmoid(cv)
    o_ref[...] = _dot(s, w_ref[...], precision=HIGHEST) + b_ref[...]


def _modulation(cv, mod_w, mod_b):
    depth, d, n = mod_w.shape
    tn = 1536
    return pl.pallas_call(
        _mod_kernel,
        grid=(depth, n // tn),
        in_specs=[
            pl.BlockSpec((MOD_ROWS, d), lambda l, j: (0, 0)),
            pl.BlockSpec((None, d, tn), lambda l, j: (l, 0, j)),
            pl.BlockSpec((None, 1, tn), lambda l, j: (l, 0, j)),
        ],
        out_specs=pl.BlockSpec((None, MOD_ROWS, tn), lambda l, j: (l, 0, j)),
        out_shape=jax.ShapeDtypeStruct((depth, MOD_ROWS, n), F32),
        compiler_params=_params(("arbitrary", "arbitrary")),
        name="modulation",
    )(cv, mod_w, mod_b.reshape(depth, 1, n))


def _norm_mod(x, g_row, m_ref, shift_row, scale_row):
    ms = jnp.mean(x * x, axis=-1, keepdims=True)
    h = x * lax.rsqrt(ms + EPS) * g_row
    return h * (1.0 + m_ref[scale_row:scale_row + 1, :]) + m_ref[shift_row:shift_row + 1, :]


def _head_pair_ones():
    r = lax.broadcasted_iota(I32, (LANES, LANES), 0) // HEAD_DIM
    c = lax.broadcasted_iota(I32, (LANES, LANES), 1) // HEAD_DIM
    return jnp.where(r == c, 1.0, 0.0).astype(BF16)


def _head_rms(y, g_row, ones_blk):
    outs = []
    for j in range(y.shape[1] // LANES):
        s = y[:, j * LANES:(j + 1) * LANES]
        hi, lo = _split_bf16(s * s)
        ms = (_dot(hi, ones_blk) + _dot(lo, ones_blk)) * (1.0 / HEAD_DIM)
        outs.append(s * lax.rsqrt(ms + EPS))
    return jnp.concatenate(outs, axis=1) * g_row


def _mod_row_map(n_ctx_tiles, n_batch):
    return lambda b, t: (jnp.where(t < n_ctx_tiles, n_batch, b), 0, 0)


def _proj_even_kernel(x_ref, m_ref, gn_ref, w_ref, a2_ref, ba2_ref, qg_ref, kg_ref,
                      gq_ref, gk_ref, gv_ref, gate_ref, la_ref, nq_ref, nk_ref, nv_ref):
    h = _norm_mod(x_ref[...], gn_ref[...], m_ref, 0, 1)
    y = _dot(h.astype(BF16), w_ref[...])
    c0 = 0
    gq_ref[...] = y[:, c0:c0 + GLA_QK] * (GLA_DK ** -0.5)
    c0 += GLA_QK
    gk_ref[...] = y[:, c0:c0 + GLA_QK]
    c0 += GLA_QK
    gv_ref[...] = y[:, c0:c0 + GLA_V].astype(BF16)
    c0 += GLA_V
    gate_ref[...] = y[:, c0:c0 + GLA_V].astype(BF16)
    c0 += GLA_V
    ones_blk = _head_pair_ones()
    nq_ref[...] = (_head_rms(y[:, c0:c0 + NA_W], qg_ref[...], ones_blk) * (HEAD_DIM ** -0.5)).astype(BF16)
    c0 += NA_W
    nk_ref[...] = _head_rms(y[:, c0:c0 + NA_W], kg_ref[...], ones_blk).astype(BF16)
    c0 += NA_W
    nv_ref[...] = y[:, c0:c0 + NA_W].astype(BF16)
    c0 += NA_W
    a = y[:, c0:c0 + LANES]
    z = _dot(a, a2_ref[...], precision=HIGHEST) + ba2_ref[...]
    la_ref[...] = (jnp.minimum(z, 0.0) - jnp.log1p(jnp.exp(-jnp.abs(z)))) * (1.0 / GLA_TAU)


def _proj_even(xa, mods, gn, w_bf, a2, ba2, qg, kg, n_ctx_tiles):
    b, tt, d = xa.shape
    tm = TOKEN_TILE
    n = w_bf.shape[1]
    tok = lambda w: pl.BlockSpec((None, tm, w), lambda bi, t: (bi, t, 0))
    full = lambda shape: pl.BlockSpec(shape, lambda bi, t: (0,) * len(shape))
    widths = [(GLA_QK, F32), (GLA_QK, F32), (GLA_V, BF16), (GLA_V, BF16), (2 * GLA_QK, F32),
              (NA_W, BF16), (NA_W, BF16), (NA_W, BF16)]
    return pl.pallas_call(
        _proj_even_kernel,
        grid=(b, tt // tm),
        in_specs=[
            tok(d),
            pl.BlockSpec((None, 6, d), _mod_row_map(n_ctx_tiles, b)),
            full((1, d)), full((d, n)), full((LANES, 2 * GLA_QK)), full((1, 2 * GLA_QK)),
            full((1, NA_W)), full((1, NA_W)),
        ],
        out_specs=[tok(w) for w, _ in widths],
        out_shape=[jax.ShapeDtypeStruct((b, tt, w), dt) for w, dt in widths],
        compiler_params=_params(("parallel", "parallel")),
        name="proj_even",
    )(xa, mods, gn, w_bf, a2, ba2, qg, kg)


def _rope(y, cos, sin_signed, lo_mask):
    w = y.shape[1]
    reps = w // LANES
    cos_t = jnp.concatenate([cos] * reps, axis=1) if reps > 1 else cos
    sin_t = jnp.concatenate([sin_signed] * reps, axis=1) if reps > 1 else sin_signed
    msk = jnp.concatenate([lo_mask] * reps, axis=1) if reps > 1 else lo_mask
    quarter = HEAD_DIM // 4
    swapped = jnp.where(msk, pltpu.roll(y, w - quarter, 1), pltpu.roll(y, quarter, 1))
    return y * cos_t + swapped * sin_t


def _proj_odd_kernel(x_ref, m_ref, gn_ref, w_ref, qg_ref, kg_ref, cos_ref, sin_ref,
                     q_ref, k_ref, v_ref):
    h = _norm_mod(x_ref[...], gn_ref[...], m_ref, 0, 1)
    y = _dot(h.astype(BF16), w_ref[...])
    ones_blk = _head_pair_ones()
    lane = lax.broadcasted_iota(I32, (1, LANES), 1)
    lo_mask = (lane % (HEAD_DIM // 2)) < (HEAD_DIM // 4)
    cos = cos_ref[...]
    sin = sin_ref[...]
    q = _head_rms(y[:, 0:SW_Q], qg_ref[...], ones_blk)
    q_ref[...] = (_rope(q, cos, sin, lo_mask) * (HEAD_DIM ** -0.5)).astype(BF16)
    k = _head_rms(y[:, SW_Q:SW_Q + SW_KV], kg_ref[...], ones_blk)
    k_ref[...] = _rope(k, cos, sin, lo_mask).astype(BF16)
    v_ref[...] = y[:, SW_Q + SW_KV:SW_Q + 2 * SW_KV].astype(BF16)


def _proj_odd(xa, mods, gn, w_bf, qg, kg, cos, sin, n_ctx_tiles):
    b, tt, d = xa.shape
    tm = TOKEN_TILE
    n = w_bf.shape[1]
    tok = lambda w: pl.BlockSpec((None, tm, w), lambda bi, t: (bi, t, 0))
    full = lambda shape: pl.BlockSpec(shape, lambda bi, t: (0,) * len(shape))
    pos = pl.BlockSpec((tm, LANES), lambda bi, t: (t, 0))
    widths = [SW_Q, SW_KV, SW_KV]
    return pl.pallas_call(
        _proj_odd_kernel,
        grid=(b, tt // tm),
        in_specs=[
            tok(d),
            pl.BlockSpec((None, 6, d), _mod_row_map(n_ctx_tiles, b)),
            full((1, d)), full((d, n)), full((1, SW_Q)), full((1, SW_KV)), pos, pos,
        ],
        out_specs=[tok(w) for w in widths],
        out_shape=[jax.ShapeDtypeStruct((b, tt, w), BF16) for w in widths],
        compiler_params=_params(("parallel", "parallel")),
        name="proj_odd",
    )(xa, mods, gn, w_bf, qg, kg, cos, sin)


def _gla_direction(backward, q_ref, k_ref, v_ref, g_ref, o_ref, st_ref):
    c = GLA_CHUNK
    assert c == GLA_DK
    n_sub = c // GLA_SUB
    n_chunks = q_ref.shape[0] // c
    sgn = -1 if backward else 1
    r = lax.broadcasted_iota(I32, (2 * c, c), 0)
    j = lax.broadcasted_iota(I32, (2 * c, c), 1)
    i = r % c
    incl = jnp.where(sgn * (i - j) >= 0, 1.0, 0.0)
    blk = jnp.where(sgn * (i // GLA_SUB - j // GLA_SUB) > 0, 1.0, 0.0)
    scan_mat = jnp.where(r >= c, blk, incl).astype(BF16)
    causal = scan_mat[0:c, :].astype(F32)
    row = lax.broadcasted_iota(I32, (c, 1), 0)
    reachable = [sgn * ((a * GLA_SUB if backward else a * GLA_SUB + GLA_SUB - 1) - row) >= 0 for a in range(n_sub)]
    lane_head = lax.broadcasted_iota(I32, (GLA_HEADS * c, GLA_QK), 1) // GLA_DK
    row_idx = lax.broadcasted_iota(I32, (GLA_HEADS * c, GLA_QK), 0)
    own_chunk = jnp.where(row_idx // c == lane_head, 1.0, 0.0).astype(BF16)
    sub_shape = (GLA_HEADS * GLA_SUB, GLA_QK)
    own_sub_f32 = jnp.where(lax.broadcasted_iota(I32, sub_shape, 0) // GLA_SUB
                            == lax.broadcasted_iota(I32, sub_shape, 1) // GLA_DK, 1.0, 0.0)
    own_sub = own_sub_f32.astype(BF16)
    ones_cols = jnp.ones((c, GLA_DV), BF16)
    last = 0 if backward else c - 1

    for u in range(n_chunks):
        ci = n_chunks - 1 - u if backward else u
        rows = slice(ci * c, (ci + 1) * c)
        g_hi, g_lo = _split_bf16(g_ref[rows, :])
        cr = _dot(scan_mat, g_hi) + _dot(scan_mat, g_lo)
        cum = cr[0:c, :]
        ref_pt = cr[c:2 * c, :]
        tot = cum[last:last + 1, :]
        tot_rows = _dot_tn(g_hi, ones_cols) + _dot_tn(g_lo, ones_cols)

        q = q_ref[rows, :]
        k = k_ref[rows, :]
        q_state = (q * jnp.exp(cum)).astype(BF16)
        q_sub = (q * jnp.exp(cum - ref_pt)).astype(BF16)
        k_state = (k * jnp.exp(tot - cum)).astype(BF16)

        placed = []
        for a in range(n_sub):
            lo = a * GLA_SUB
            k_sub = (k * jnp.exp(jnp.where(reachable[a], ref_pt[lo:lo + 1, :] - cum, 0.0))).astype(BF16)
            q4 = jnp.concatenate([q_sub[lo:lo + GLA_SUB, :]] * GLA_HEADS, axis=0) * own_sub
            a4 = _dot_nt(q4, k_sub) * jnp.concatenate([causal[lo:lo + GLA_SUB, :]] * GLA_HEADS, axis=0)
            placed.append(jnp.concatenate([a4] * GLA_HEADS, axis=1) * own_sub_f32)
        attn = jnp.concatenate([placed[a][h * GLA_SUB:(h + 1) * GLA_SUB, :]
                                for h in range(GLA_HEADS) for a in range(n_sub)], axis=0).astype(BF16)
        v = v_ref[rows, :]
        v_rows = jnp.concatenate([v[:, h * GLA_DV:(h + 1) * GLA_DV] for h in range(GLA_HEADS)], axis=0)
        q4s = jnp.concatenate([q_state] * GLA_HEADS, axis=0) * own_chunk
        st = st_ref[...]
        o = _dot(attn, v_rows) + _dot(q4s, st.astype(BF16))
        for h in range(GLA_HEADS):
            o_ref[rows, h * GLA_DV:(h + 1) * GLA_DV] = o[h * c:(h + 1) * c, :]
        kv = _dot_tn(k_state, v)
        kv_own = jnp.concatenate([kv[h * GLA_DK:(h + 1) * GLA_DK, h * GLA_DV:(h + 1) * GLA_DV]
                                  for h in range(GLA_HEADS)], axis=0)
        st_ref[...] = st * jnp.exp(tot_rows) + kv_own


def _gla_kernel(qf_ref, kf_ref, vf_ref, gf_ref, qb_ref, kb_ref, vb_ref, gb_ref, of_ref, ob_ref, stf_ref, stb_ref):
    @pl.when(pl.program_id(1) == 0)
    def _():
        stf_ref[...] = jnp.zeros_like(stf_ref)
        stb_ref[...] = jnp.zeros_like(stb_ref)

    _gla_direction(False, qf_ref, kf_ref, vf_ref, gf_ref, of_ref, stf_ref)
    _gla_direction(True, qb_ref, kb_ref, vb_ref, gb_ref, ob_ref, stb_ref)


def _gla(gq, gk, gv, la, n_ctx):
    b, tt, _ = gq.shape
    rows = TOKEN_TILE
    nb = tt // rows
    n_ctx_blk = n_ctx // rows

    def bwd(s):
        return jnp.where(s < n_ctx_blk, n_ctx_blk - 1 - s, nb - 1 + n_ctx_blk - s)

    def tok(w, col, order):
        return pl.BlockSpec((None, rows, w), lambda bi, s: (bi, order(s), col))

    fwd = lambda s: s
    return pl.pallas_call(
        _gla_kernel,
        grid=(b, nb),
        in_specs=[tok(GLA_QK, 0, fwd), tok(GLA_QK, 0, fwd), tok(GLA_V, 0, fwd), tok(GLA_QK, 0, fwd),
                  tok(GLA_QK, 0, bwd), tok(GLA_QK, 0, bwd), tok(GLA_V, 0, bwd), tok(GLA_QK, 1, bwd)],
        out_specs=[tok(GLA_V, 0, fwd), tok(GLA_V, 0, bwd)],
        out_shape=[jax.ShapeDtypeStruct((b, tt, GLA_V), F32)] * 2,
        scratch_shapes=[pltpu.VMEM((GLA_HEADS * GLA_DK, GLA_DV), F32)] * 2,
        compiler_params=_params(("arbitrary", "arbitrary")),
        name="gla_scan",
    )(gq, gk, gv, la, gq, gk, gv, la)


def _na_bias_table(rpb, kh):
    n_heads = rpb.shape[0]
    qc = np.arange(GRID_W)[:, None]
    kc = np.arange(GRID_W)[None, :]
    win0 = np.clip(qc - NA_KW // 2, 0, GRID_W - NA_KW)
    valid = (kc >= win0) & (kc < win0 + NA_KW)
    side = GRID_W - NA_KW
    rp = jnp.pad(rpb, ((0, 0), (0, 0), (side, side)))
    base = jnp.stack([rp[:, :, GRID_W - 1 - c:2 * GRID_W - 1 - c] for c in range(GRID_W)], axis=2)
    base = jnp.where(valid[None, None], base, NEG)
    tab = jnp.stack([base[:, NA_KH - 1 - a:NA_KH - 1 - a + kh] for a in range(kh)], axis=1)
    tab = tab.reshape(n_heads // 2, 2, kh, kh, GRID_W, GRID_W).transpose(0, 2, 1, 4, 3, 5)
    return tab.reshape(n_heads // 2, kh, 2 * GRID_W, kh * GRID_W).astype(F32)


def _na_kernel(q_ref, k_ref, v_ref, bias_ref, o_ref, *, n_ctx, n_ctx_blk, rows_per_blk, n_rows, kh):
    t = pl.program_id(2)
    lane = lax.broadcasted_iota(I32, (1, LANES), 1)
    first = lane < HEAD_DIM
    kc = k_ref[0:n_ctx, :]
    vc = v_ref[0:n_ctx, :]

    def stack_heads(q):
        z = jnp.zeros_like(q)
        return jnp.concatenate([jnp.where(first, q, z), jnp.where(first, z, q)], axis=0)

    def unstack_heads(o):
        n = o.shape[0] // 2
        return jnp.where(first, o[0:n], o[n:2 * n])

    @pl.when(t < n_ctx_blk)
    def _():
        s = _dot_nt(stack_heads(q_ref[...]), kc)
        m = jnp.max(s, axis=-1, keepdims=True)
        p = jnp.exp(s - m)
        den = jnp.sum(p, axis=-1, keepdims=True)
        o_ref[...] = unstack_heads(_dot(p.astype(BF16), vc) / den).astype(o_ref.dtype)

    @pl.when(t >= n_ctx_blk)
    def _():
        for i in range(rows_per_blk):
            r = (t - n_ctx_blk) * rows_per_blk + i
            row0 = jnp.clip(r - kh // 2, 0, n_rows - kh)
            cls = r - row0
            start = pl.multiple_of(n_ctx + row0 * GRID_W, GRID_W)
            kb = k_ref[pl.ds(start, kh * GRID_W), :]
            vb = v_ref[pl.ds(start, kh * GRID_W), :]
            qs = stack_heads(q_ref[i * GRID_W:(i + 1) * GRID_W, :])
            s = _dot_nt(qs, kb) + bias_ref[cls]
            sc = _dot_nt(qs, kc)
            m = jnp.maximum(jnp.max(s, axis=-1, keepdims=True), jnp.max(sc, axis=-1, keepdims=True))
            p = jnp.exp(s - m)
            pc = jnp.exp(sc - m)
            den = jnp.sum(p, axis=-1, keepdims=True) + jnp.sum(pc, axis=-1, keepdims=True)
            o = (_dot(p.astype(BF16), vb) + _dot(pc.astype(BF16), vc)) / den
            o_ref[i * GRID_W:(i + 1) * GRID_W, :] = unstack_heads(o).astype(o_ref.dtype)


def _na(nq, nk, nv, bias_tab, n_ctx):
    b, tt, _ = nq.shape
    tq = TOKEN_TILE
    n_rows = (tt - n_ctx) // GRID_W
    kh = bias_tab.shape[1]
    kern = functools.partial(_na_kernel, n_ctx=n_ctx, n_ctx_blk=n_ctx // tq, rows_per_blk=tq // GRID_W,
                             n_rows=n_rows, kh=kh)
    return pl.pallas_call(
        kern,
        grid=(b, NA_HEADS // 2, tt // tq),
        in_specs=[
            pl.BlockSpec((None, tq, LANES), lambda bi, p, t: (bi, t, p)),
            pl.BlockSpec((None, tt, LANES), lambda bi, p, t: (bi, 0, p)),
            pl.BlockSpec((None, tt, LANES), lambda bi, p, t: (bi, 0, p)),
            pl.BlockSpec((None, kh, 2 * GRID_W, kh * GRID_W), lambda bi, p, t: (p, 0, 0, 0)),
        ],
        out_specs=pl.BlockSpec((None, tq, LANES), lambda bi, p, t: (bi, t, p)),
        out_shape=jax.ShapeDtypeStruct((b, tt, NA_W), BF16),
        compiler_params=_params(("parallel", "parallel", "arbitrary")),
        name="na_attn",
    )(nq, nk, nv, bias_tab)


def _swa_kernel(sink_ref, q_ref, k_ref, v_ref, o_ref, *, n_ctx, n_lat_blk):
    blk = SW_BLOCK
    n = pl.program_id(1)
    n_ctx_blk = n_ctx // blk
    jb = n - n_ctx_blk
    sb = jnp.clip(jb - 1, 0, n_lat_blk - 3)
    start = pl.multiple_of(n_ctx + sb * blk, blk)
    keys = jnp.concatenate([k_ref[pl.ds(start, 3 * blk), :], k_ref[0:n_ctx, :]], axis=0)
    vals = jnp.concatenate([v_ref[pl.ds(start, 3 * blk), :], v_ref[0:n_ctx, :]], axis=0)
    nk = 3 * blk + n_ctx
    iq = lax.broadcasted_iota(I32, (blk, nk), 0)
    ik = lax.broadcasted_iota(I32, (blk, nk), 1)
    offset = jnp.where(n < n_ctx_blk, 4 * SW_WINDOW, (jb - sb) * blk)
    dist = jnp.where(ik >= 3 * blk, 0, iq - ik + offset)
    bias = jnp.where(jnp.abs(dist) <= SW_WINDOW, 0.0, NEG)
    q = q_ref[...]
    group = SW_HEADS // SW_KV_HEADS
    bias_g = jnp.concatenate([bias] * group, axis=0)
    lane = lax.broadcasted_iota(I32, (1, LANES), 1)
    assert SW_KV_HEADS * HEAD_DIM == LANES
    for g in range(SW_KV_HEADS):
        gs = slice(g * HEAD_DIM, (g + 1) * HEAD_DIM)
        in_g = (lane // HEAD_DIM) == g
        heads = range(g * group, (g + 1) * group)
        blocks = []
        for h in heads:
            two = q[:, (h // 2) * LANES:(h // 2 + 1) * LANES]
            if h % 2 != g:
                two = jnp.concatenate([two[:, HEAD_DIM:], two[:, :HEAD_DIM]], axis=1)
            blocks.append(jnp.where(in_g, two, jnp.zeros_like(two)))
        qg = jnp.concatenate(blocks, axis=0)
        sink = jnp.concatenate([jnp.full((blk, 1), sink_ref[h], F32) for h in heads], axis=0)
        s = _dot_nt(qg, keys) + bias_g
        m = jnp.maximum(jnp.max(s, axis=-1, keepdims=True), sink)
        p = jnp.exp((s - m).astype(BF16))
        o2 = _dot(p, jnp.where(in_g, vals, jnp.ones_like(vals)))
        other = (1 - g) * HEAD_DIM
        den = o2[:, other:other + 1] + jnp.exp(sink - m)
        o = o2[:, gs] / den
        for pr in range(group // 2):
            pair = [o[(2 * pr + e) * blk:(2 * pr + e + 1) * blk, :] for e in range(2)]
            col = (g * group // 2 + pr) * LANES
            o_ref[:, col:col + LANES] = jnp.concatenate(pair, axis=1).astype(o_ref.dtype)


def _swa(q, k, v, sinks, n_ctx):
    b, tt, _ = q.shape
    blk = SW_BLOCK
    kern = functools.partial(_swa_kernel, n_ctx=n_ctx, n_lat_blk=(tt - n_ctx) // blk)
    return pl.pallas_call(
        kern,
        grid=(b, tt // blk),
        in_specs=[
            pl.BlockSpec(memory_space=pltpu.SMEM),
            pl.BlockSpec((None, blk, SW_Q), lambda bi, n: (bi, n, 0)),
            pl.BlockSpec((None, tt, SW_KV), lambda bi, n: (bi, 0, 0)),
            pl.BlockSpec((None, tt, SW_KV), lambda bi, n: (bi, 0, 0)),
        ],
        out_specs=pl.BlockSpec((None, blk, SW_Q), lambda bi, n: (bi, n, 0)),
        out_shape=jax.ShapeDtypeStruct((b, tt, SW_Q), BF16),
        compiler_params=_params(("parallel", "arbitrary")),
        name="swa_attn",
    )(sinks, q, k, v)


def _route(f, wr_ref, br_ref, before_ref, run_ref, idx_ref, wt_ref, rank_ref, cnt_ref, base_ref):
    tm = f.shape[0]
    f_hi, f_lo = _split_bf16(f)
    logits = _dot(f_hi, wr_ref[0]) + (_dot(f_lo, wr_ref[0]) + _dot(f_hi, wr_ref[1])) + br_ref[...]
    e_iota = lax.broadcasted_iota(I32, (tm, N_EXPERTS), 1).astype(F32)
    lane8 = lax.broadcasted_iota(I32, (tm, 8), 1)
    vals, hots = [], []
    idx_out = jnp.zeros((tm, 8), I32)
    for k in range(TOP_K):
        mx = jnp.max(logits, axis=-1, keepdims=True)
        am = jnp.min(jnp.where(logits == mx, e_iota, float(N_EXPERTS)), axis=-1, keepdims=True)
        hot = e_iota == am
        vals.append(mx)
        hots.append(hot)
        idx_out = jnp.where(lane8 == k, am.astype(I32), idx_out)
        logits = jnp.where(hot, -jnp.inf, logits)
    exps = [jnp.exp(v - vals[0]) for v in vals]
    den = exps[0] + exps[1] + exps[2] + exps[3]
    multihot = jnp.zeros((tm, N_EXPERTS), F32)
    wt_out = jnp.zeros((tm, 8), F32)
    for k in range(TOP_K):
        multihot = multihot + jnp.where(hots[k], 1.0, 0.0)
        wt_out = jnp.where(lane8 == k, exps[k] / den, wt_out)
    base = _dot(before_ref[...], multihot.astype(BF16)) + run_ref[...]
    rank_out = jnp.zeros((tm, 8), I32)
    for k in range(TOP_K):
        rk = jnp.sum(jnp.where(hots[k], base, 0.0), axis=-1, keepdims=True)
        rank_out = jnp.where(lane8 == k, rk.astype(I32), rank_out)
    base_ref[...] = run_ref[...]
    run_ref[...] = run_ref[...] + jnp.sum(multihot, axis=0, keepdims=True)
    idx_ref[...] = idx_out
    wt_ref[...] = wt_out
    rank_ref[...] = rank_out
    cnt_ref[...] = run_ref[...]


def _post_common(mix, x_ref, m_ref, fg_ref, wr_ref, br_ref, before_ref, xn_ref, f_ref, idx_ref, wt_ref, rank_ref,
                 cnt_ref, base_ref, run_ref):
    @pl.when((pl.program_id(0) == 0) & (pl.program_id(1) == 0))
    def _():
        run_ref[...] = jnp.zeros_like(run_ref)

    xn = x_ref[...] + m_ref[2:3, :] * mix
    xn_ref[...] = xn
    f = _norm_mod(xn, fg_ref[...], m_ref, 3, 4)
    f_ref[...] = f.astype(f_ref.dtype)
    _route(f, wr_ref, br_ref, before_ref, run_ref, idx_ref, wt_ref, rank_ref, cnt_ref, base_ref)


def _post_even_kernel(of_ref, ob_ref, gate_ref, na_ref, gng_ref, w_ref, x_ref, m_ref, fg_ref, wr_ref, br_ref,
                      before_ref, xn_ref, f_ref, idx_ref, wt_ref, rank_ref, cnt_ref, base_ref, run_ref):
    o = of_ref[...] + ob_ref[...]
    gate = gate_ref[...].astype(F32)
    gn = gng_ref[...]
    parts = []
    for h in range(GLA_HEADS):
        oh = o[:, h * GLA_DV:(h + 1) * GLA_DV]
        ms = jnp.mean(oh * oh, axis=-1, keepdims=True)
        parts.append(oh * lax.rsqrt(ms + EPS) * gn)
    gla = jnp.concatenate(parts, axis=1) * (gate * jax.nn.sigmoid(gate))
    mix = _dot(gla.astype(BF16), w_ref[0:GLA_V, :]) + _dot(na_ref[...], w_ref[GLA_V:GLA_V + NA_W, :])
    _post_common(mix, x_ref, m_ref, fg_ref, wr_ref, br_ref, before_ref, xn_ref, f_ref, idx_ref, wt_ref, rank_ref,
                 cnt_ref, base_ref, run_ref)


def _post_odd_kernel(o_ref, w_ref, x_ref, m_ref, fg_ref, wr_ref, br_ref, before_ref,
                     xn_ref, f_ref, idx_ref, wt_ref, rank_ref, cnt_ref, base_ref, run_ref):
    mix = _dot(o_ref[...], w_ref[...])
    _post_common(mix, x_ref, m_ref, fg_ref, wr_ref, br_ref, before_ref, xn_ref, f_ref, idx_ref, wt_ref, rank_ref,
                 cnt_ref, base_ref, run_ref)


def _post(even, mixer_outs, extra, w_bf, xa, mods, fg, wr, br, n_ctx_tiles):
    b, tt, d = xa.shape
    tm = TOKEN_TILE
    tok = lambda w: pl.BlockSpec((None, tm, w), lambda bi, t: (bi, t, 0))
    full = lambda shape: pl.BlockSpec(shape, lambda bi, t: (0,) * len(shape))
    if even:
        o_fwd, o_bwd, gate, na = mixer_outs
        lead_specs = [tok(GLA_V), tok(GLA_V), tok(GLA_V), tok(NA_W), full((1, GLA_DV))]
        lead_args = [o_fwd, o_bwd, gate, na, extra]
        kern = _post_even_kernel
    else:
        (o,) = mixer_outs
        lead_specs = [tok(SW_Q)]
        lead_args = [o]
        kern = _post_odd_kernel
    nt = tt // tm
    small = lambda: pl.BlockSpec((None, tm, 8), lambda bi, t: (bi, t, 0))
    before = jnp.asarray(np.tril(np.ones((tm, tm), np.float32), -1), BF16)
    return pl.pallas_call(
        kern,
        grid=(b, tt // tm),
        in_specs=lead_specs + [
            full(w_bf.shape), tok(d),
            pl.BlockSpec((None, 6, d), _mod_row_map(n_ctx_tiles, b)),
            full((1, d)), full((2, d, N_EXPERTS)), full((1, N_EXPERTS)), full((tm, tm)),
        ],
        out_specs=[tok(d), tok(d), small(), small(), small(), full((1, N_EXPERTS)),
                   pl.BlockSpec((None, 1, N_EXPERTS), lambda bi, t: (bi * nt + t, 0, 0))],
        out_shape=[
            jax.ShapeDtypeStruct((b, tt, d), F32), jax.ShapeDtypeStruct((b, tt, d), BF16),
            jax.ShapeDtypeStruct((b, tt, 8), I32), jax.ShapeDtypeStruct((b, tt, 8), F32),
            jax.ShapeDtypeStruct((b, tt, 8), I32), jax.ShapeDtypeStruct((1, N_EXPERTS), F32),
            jax.ShapeDtypeStruct((b * nt, 1, N_EXPERTS), F32),
        ],
        scratch_shapes=[pltpu.VMEM((1, N_EXPERTS), F32)],
        compiler_params=_params(("arbitrary", "arbitrary")),
        name="post_even" if even else "post_odd",
    )(*lead_args, w_bf, xa, mods, fg, wr, br, before)


def _stage_rows(tm):
    return tm * TOP_K + N_EXPERTS * (MOE_CHUNK + MOE_ALIGN)


WAIT_BATCH = 4


def _for_tile_chunks(rows_ref, tot_ref, g, max_chunks, fn):
    def per_chunk(j, c):
        fn(pl.multiple_of(rows_ref[g * max_chunks + j], MOE_ALIGN), pl.multiple_of(j * MOE_CHUNK, MOE_CHUNK))
        return c

    lax.fori_loop(0, tot_ref[g], per_chunk, 0)


def _wait_chunks(n, make_copy):
    def batch(i, c):
        make_copy(WAIT_BATCH * MOE_CHUNK).wait()
        return c

    def single(i, c):
        make_copy(MOE_CHUNK).wait()
        return c

    lax.fori_loop(0, n // WAIT_BATCH, batch, 0)
    lax.fori_loop(0, n % WAIT_BATCH, single, 0)


def _dispatch_kernel(rows_ref, tot_ref, nch_ref, off_ref, lead_ref, cnt_ref, lo_ref, hi_ref, colt_ref, f_ref, xs_ref,
                     stage, carry, zeros_ref, sems, zsem):
    g = pl.program_id(0)
    n_tiles = pl.num_programs(0)
    ch = MOE_CHUNK
    buf = g % 2
    max_chunks = stage.shape[1] // ch

    def chunk_copy(which, slot_row, stage_row, rows=ch):
        return pltpu.make_async_copy(stage.at[which, pl.ds(stage_row, rows)], xs_ref.at[pl.ds(slot_row, rows)],
                                     sems.at[which])

    def wait_tile(tile, which):
        _wait_chunks(tot_ref[tile], lambda rows: chunk_copy(which, 0, 0, rows))

    @pl.when(g == 0)
    def _():
        carry[...] = jnp.zeros_like(carry)
        zeros_ref[...] = jnp.zeros_like(zeros_ref)

        def pad_pass(do):
            def per_expert(e, c):
                lo = lo_ref[e]
                hi = hi_ref[e]
                n_full = (hi - lo) // ch

                def full(j, c2):
                    do(pltpu.make_async_copy(
                        zeros_ref, xs_ref.at[pl.ds(pl.multiple_of(hi - (j + 1) * ch, MOE_ALIGN), ch)], zsem))
                    return c2

                lax.fori_loop(0, n_full, full, 0)

                @pl.when((hi - lo) - n_full * ch > 0)
                def _():
                    do(pltpu.make_async_copy(zeros_ref.at[pl.ds(0, MOE_ALIGN)],
                                             xs_ref.at[pl.ds(pl.multiple_of(lo, MOE_ALIGN), MOE_ALIGN)], zsem))
                return c

            lax.fori_loop(0, lo_ref.shape[0], per_expert, 0)

        pad_pass(lambda cp: cp.start())
        pad_pass(lambda cp: cp.wait())

    colt = colt_ref[...]
    rs = stage.shape[1]
    r_iota = lax.broadcasted_iota(I32, (rs, colt.shape[1]), 0)
    hit = jnp.zeros(r_iota.shape, F32)
    for k in range(TOP_K):
        hit = jnp.where(r_iota == colt[k:k + 1, :], 1.0, hit)
    stage[buf] = _dot(hit.astype(BF16), f_ref[...]).astype(stage.dtype)

    for e in range(N_EXPERTS):
        i = g * N_EXPERTS + e
        active = nch_ref[i] > 0
        ptr = off_ref[i]
        head = pl.ds(pl.multiple_of(ptr, MOE_ALIGN), MOE_ALIGN)
        old = carry[e]
        stage[buf, head, :] = stage[buf, head, :] + jnp.where(active, old, jnp.zeros_like(old))
        end = lead_ref[i] + cnt_ref[i]
        new_lead = end % MOE_ALIGN
        tail = pl.ds(pl.multiple_of(ptr + end - new_lead, MOE_ALIGN), MOE_ALIGN)
        kept = jnp.where(new_lead > 0, stage[buf, tail, :], jnp.zeros_like(old))
        carry[e] = jnp.where(active, kept, old)

    @pl.when(g > 0)
    def _():
        wait_tile(g - 1, 1 - buf)

    _for_tile_chunks(rows_ref, tot_ref, g, max_chunks,
                     lambda slot_row, stage_row: chunk_copy(buf, slot_row, stage_row).start())

    @pl.when(g == n_tiles - 1)
    def _():
        wait_tile(g, buf)


def _dispatch(rows, tot, nch, off, lead, cnt, pad_lo, pad_hi, colt, f2, n_slots):
    t, d = f2.shape
    tm = TOKEN_TILE
    rs = _stage_rows(tm)
    return pl.pallas_call(
        _dispatch_kernel,
        grid_spec=pltpu.PrefetchScalarGridSpec(
            num_scalar_prefetch=8,
            grid=(t // tm,),
            in_specs=[
                pl.BlockSpec((None, 8, tm), lambda g, *_: (g, 0, 0)),
                pl.BlockSpec((tm, d), lambda g, *_: (g, 0)),
            ],
            out_specs=pl.BlockSpec(memory_space=pl.ANY),
            scratch_shapes=[pltpu.VMEM((2, rs, d), BF16), pltpu.VMEM((N_EXPERTS, MOE_ALIGN, d), BF16),
                            pltpu.VMEM((MOE_CHUNK, d), BF16),
                            pltpu.SemaphoreType.DMA((2,)), pltpu.SemaphoreType.DMA(())],
        ),
        out_shape=jax.ShapeDtypeStruct((n_slots, d), BF16),
        compiler_params=_params(("arbitrary",)),
        name="moe_dispatch",
    )(rows, tot, nch, off, lead, cnt, pad_lo, pad_hi, colt, f2)


def _expert_kernel(be_ref, nu_ref, xs_ref, wgu_ref, bgu_ref, wd_ref, bd_ref, ys_ref, wgu_bf, wd_bf):
    i = pl.program_id(0)
    prev = be_ref[jnp.maximum(i - 1, 0)]
    fresh = (i == 0) | (be_ref[i] != prev)

    @pl.when(fresh & (i < nu_ref[0]))
    def _():
        wgu_bf[...] = wgu_ref[...].astype(BF16)
        wd_bf[...] = wd_ref[...].astype(BF16)

    @pl.when(i < nu_ref[0])
    def _():
        x = xs_ref[...]
        de = wd_bf.shape[0]
        half = de // 2
        acc = None
        for c in range(2):
            glu = _dot(x, wgu_bf[:, c * half:(c + 1) * half]) + bgu_ref[:, c * half:(c + 1) * half]
            lin = _dot(x, wgu_bf[:, de + c * half:de + (c + 1) * half]) + bgu_ref[:, de + c * half:de + (c + 1) * half]
            glu = jnp.minimum(glu, SWIGLU_LIMIT)
            lin = jnp.clip(lin, -SWIGLU_LIMIT, SWIGLU_LIMIT)
            act = (glu * jax.nn.sigmoid(SWIGLU_ALPHA * glu) * (lin + 1.0)).astype(BF16)
            part = _dot(act, wd_bf[c * half:(c + 1) * half, :])
            acc = part if acc is None else acc + part
        ys_ref[...] = (acc + bd_ref[...]).astype(ys_ref.dtype)

    @pl.when(i >= nu_ref[0])
    def _():
        ys_ref[...] = jnp.zeros_like(ys_ref)


def _experts(block_e, n_used, xs, layer, wgu, bgu, wd, bd):
    n_slots, d = xs.shape
    bm = EXPERT_BLOCK
    depth, ne, _, de2 = wgu.shape
    de = wd.shape[2]
    row_blk = lambda i, be, nu: (jnp.minimum(i, nu[0] - 1), 0)
    exp_blk = lambda i, be, nu: (layer, be[i], 0, 0)
    return pl.pallas_call(
        _expert_kernel,
        grid_spec=pltpu.PrefetchScalarGridSpec(
            num_scalar_prefetch=2,
            grid=(n_slots // bm,),
            in_specs=[
                pl.BlockSpec((bm, d), row_blk),
                pl.BlockSpec((None, None, d, de2), exp_blk),
                pl.BlockSpec((None, None, 1, de2), exp_blk),
                pl.BlockSpec((None, None, de, d), exp_blk),
                pl.BlockSpec((None, None, 1, d), exp_blk),
            ],
            out_specs=pl.BlockSpec((bm, d), lambda i, be, nu: (i, 0)),
            scratch_shapes=[pltpu.VMEM((d, de2), BF16), pltpu.VMEM((de, d), BF16)],
        ),
        out_shape=jax.ShapeDtypeStruct((n_slots, d), BF16),
        compiler_params=_params(("arbitrary",)),
        name="moe_experts",
    )(block_e, n_used, xs, wgu, bgu.reshape(depth, ne, 1, de2), wd, bd.reshape(depth, ne, 1, d))


def _combine_kernel(rows_ref, tot_ref, col_ref, wt_ref, x_ref, m_ref, ys_ref, o_ref, stage, sems):
    g = pl.program_id(0)
    n_tiles = pl.num_programs(0)
    ch = MOE_CHUNK
    buf = g % 2
    max_chunks = stage.shape[1] // ch

    def chunk_copy(which, slot_row, stage_row, rows=ch):
        return pltpu.make_async_copy(ys_ref.at[pl.ds(slot_row, rows)], stage.at[which, pl.ds(stage_row, rows)],
                                     sems.at[which])

    def fetch_tile(tile, which):
        _for_tile_chunks(rows_ref, tot_ref, tile, max_chunks,
                         lambda slot_row, stage_row: chunk_copy(which, slot_row, stage_row).start())

    @pl.when(g == 0)
    def _():
        stage[...] = jnp.zeros_like(stage)
        fetch_tile(0, 0)

    @pl.when(g + 1 < n_tiles)
    def _():
        fetch_tile(g + 1, 1 - buf)

    _wait_chunks(tot_ref[g], lambda rows: chunk_copy(buf, 0, 0, rows))

    col = col_ref[...]
    wt = wt_ref[...]
    tm = col.shape[0]
    group = 2 * LANES
    y = jnp.zeros(x_ref.shape, F32)
    for j in range(stage.shape[1] // group):
        c_iota = lax.broadcasted_iota(I32, (tm, group), 1) + j * group
        w = jnp.zeros((tm, group), F32)
        for k in range(TOP_K):
            w = jnp.where(c_iota == col[:, k:k + 1], wt[:, k:k + 1], w)
        y = y + _dot(w.astype(BF16), stage[buf, j * group:(j + 1) * group, :])
    o_ref[...] = x_ref[...] + m_ref[5:6, :] * y


def _combine(rows, tot, col, wts, xn, mods, ys, n_ctx_tiles, latent_only):
    b, tt, d = xn.shape
    tm = TOKEN_TILE
    nt = tt // tm
    rs = _stage_rows(tm)
    assert rs % (2 * LANES) == 0
    if latent_only:
        n_out = nt - n_ctx_tiles
        out_rows = b * n_out * tm
        out_map = lambda g, *_: ((g // nt) * n_out + jnp.maximum(g % nt - n_ctx_tiles, 0), 0)
    else:
        out_rows = b * tt
        out_map = lambda g, *_: (g, 0)
    out = pl.pallas_call(
        _combine_kernel,
        grid_spec=pltpu.PrefetchScalarGridSpec(
            num_scalar_prefetch=2,
            grid=(b * nt,),
            in_specs=[
                pl.BlockSpec((tm, 8), lambda g, *_: (g, 0)),
                pl.BlockSpec((tm, 8), lambda g, *_: (g, 0)),
                pl.BlockSpec((tm, d), lambda g, *_: (g, 0)),
                pl.BlockSpec((None, 6, d), lambda g, *_: (jnp.where(g % nt < n_ctx_tiles, b, g // nt), 0, 0)),
                pl.BlockSpec(memory_space=pl.ANY),
            ],
            out_specs=pl.BlockSpec((tm, d), out_map),
            scratch_shapes=[pltpu.VMEM((2, rs, d), BF16), pltpu.SemaphoreType.DMA((2,))],
        ),
        out_shape=jax.ShapeDtypeStruct((out_rows, d), F32),
        compiler_params=_params(("arbitrary",)),
        name="moe_combine",
    )(rows, tot, col, wts.reshape(b * tt, 8), xn.reshape(b * tt, d), mods, ys)
    return out.reshape(b, out_rows // b, d)


def _moe(xn, f, idx, wts, rank, cnt, tile_base, mods, layer, wgu, bgu, wd, bd, n_ctx_tiles, latent_only):
    b, tt, d = xn.shape
    t = b * tt
    bm = EXPERT_BLOCK
    tm = TOKEN_TILE
    ch = MOE_CHUNK
    n_tiles = t // tm
    n_blocks = (t * TOP_K + N_EXPERTS * ch + bm - 1) // bm + N_EXPERTS
    e_ids = np.arange(N_EXPERTS)
    lower = e_ids[None, :] < e_ids[:, None]
    counts = cnt[0].astype(I32)
    padded = (counts + ch + bm - 1) // bm * bm
    pstart = jnp.sum(jnp.where(lower, padded[None, :], 0), axis=1)
    pend = pstart + padded
    base = tile_base.reshape(n_tiles, N_EXPERTS).astype(I32)
    tile_cnt = jnp.concatenate([base[1:], counts[None, :]], axis=0) - base
    first = pstart[None, :] + base
    seg = first // MOE_ALIGN * MOE_ALIGN
    lead = first - seg
    nch = jnp.where(tile_cnt > 0, (lead + tile_cnt + ch - 1) // ch, 0)
    off = jnp.sum(jnp.where(lower[None], nch[:, None, :], 0), axis=2) * ch
    idx4 = idx[..., :TOP_K].reshape(n_tiles, tm, TOP_K)
    rank4 = rank[..., :TOP_K].reshape(n_tiles, tm, TOP_K)
    delta = (off + lead - base)[:, None, None, :]
    col = rank4 + jnp.sum(jnp.where(idx4[..., None] == e_ids, delta, 0), axis=-1)
    col = jnp.concatenate([col, jnp.full((n_tiles, tm, 8 - TOP_K), -1, I32)], axis=-1)
    colt = col.transpose(0, 2, 1)
    n_used = (pend[N_EXPERTS - 1] // bm).reshape(1).astype(I32)
    starts = np.arange(n_blocks, dtype=np.int32)[:, None] * bm
    block_e = jnp.minimum(jnp.sum((pend[None, :] <= starts).astype(I32), axis=1), N_EXPERTS - 1)
    pad_lo = jnp.concatenate([(pstart + counts + MOE_ALIGN - 1) // MOE_ALIGN * MOE_ALIGN, pend[N_EXPERTS - 1:]])
    pad_hi = jnp.concatenate([pend, jnp.full((1,), n_blocks * bm, I32)])
    max_chunks = _stage_rows(tm) // ch
    cum = off // ch + nch
    j = np.arange(max_chunks, dtype=np.int32)
    e_of_j = jnp.sum((cum[:, None, :] <= j[None, :, None]).astype(I32), axis=-1)
    hot = e_of_j[..., None] == e_ids
    seg_j = jnp.sum(jnp.where(hot, seg[:, None, :], 0), axis=-1)
    first_j = jnp.sum(jnp.where(hot, (off // ch)[:, None, :], 0), axis=-1)
    rows = (seg_j + (j[None, :] - first_j) * ch).reshape(-1)
    tot = cum[:, N_EXPERTS - 1]
    xs = _dispatch(rows, tot, nch.reshape(-1), off.reshape(-1), lead.reshape(-1), tile_cnt.reshape(-1),
                   pad_lo, pad_hi, colt, f.reshape(t, d), n_blocks * bm)
    ys = _experts(block_e, n_used, xs, layer, wgu, bgu, wd, bd)
    return _combine(rows, tot, col.reshape(t, 8), wts, xn, mods, ys, n_ctx_tiles, latent_only)


def _rope_tables(n_ctx, s_len):
    quarter = HEAD_DIM // 4
    t = np.arange(s_len)
    pos = np.stack([t // GRID_W, t % GRID_W], axis=0).astype(np.float32)
    inv_freq = ROPE_BASE ** (-jnp.arange(quarter, dtype=F32) / quarter)
    ang = jnp.asarray(pos)[:, :, None] * inv_freq
    cos, sin = jnp.cos(ang), jnp.sin(ang)
    cos_h = jnp.concatenate([cos[0], cos[0], cos[1], cos[1]], axis=-1)
    sin_h = jnp.concatenate([-sin[0], sin[0], -sin[1], sin[1]], axis=-1)
    reps = LANES // HEAD_DIM
    cos_l = jnp.tile(cos_h, (1, reps))
    sin_l = jnp.tile(sin_h, (1, reps))
    cos_all = jnp.concatenate([jnp.ones((n_ctx, LANES), F32), cos_l], axis=0)
    sin_all = jnp.concatenate([jnp.zeros((n_ctx, LANES), F32), sin_l], axis=0)
    return cos_all, sin_all


def kernel(x, c, ctx, c_ctx, mod_w, mod_b, norm_mix_g, norm_ffn_g, ev_w_in, ev_w_out, ev_gla_w_a2, ev_gla_b_a2,
           ev_gla_norm_g, ev_na_q_g, ev_na_k_g, ev_na_rpb, od_w_in, od_w_out, od_q_g, od_k_g, od_sinks,
           moe_w_router, moe_b_router, moe_w_gate_up, moe_b_gate_up, moe_w_down, moe_b_down):
    b, s_len, d = x.shape
    n_ctx = ctx.shape[1]
    depth = mod_w.shape[0]
    assert b < MOD_ROWS and n_ctx % TOKEN_TILE == 0 and s_len % TOKEN_TILE == 0
    n_ctx_tiles = n_ctx // TOKEN_TILE
    n_rows = s_len // GRID_W
    kh = min(NA_KH, n_rows)

    xa = jnp.concatenate([ctx, x], axis=1)
    cv = jnp.zeros((MOD_ROWS, d), F32).at[:b].set(c).at[b].set(c_ctx)
    mods_all = _modulation(cv, mod_w, mod_b).reshape(depth, MOD_ROWS, 6, d)
    cos, sin = _rope_tables(n_ctx, s_len)

    for layer in range(depth):
        i = layer // 2
        mods = mods_all[layer]
        gn = norm_mix_g[layer].reshape(1, d)
        if layer % 2 == 0:
            w = ev_w_in[i]
            cuts = np.cumsum([0, GLA_QK, GLA_QK, GLA_V, GLA_V, GLA_RANK, GLA_RANK, NA_W, NA_W, NA_W])
            seg = lambda j: w[:, cuts[j]:cuts[j + 1]]
            pad = jnp.zeros((d, LANES - 2 * GLA_RANK), F32)
            w_bf = jnp.concatenate([seg(0), seg(1), seg(2), seg(3), seg(6), seg(7), seg(8), seg(4), seg(5), pad],
                                   axis=1).astype(BF16)
            a2 = jnp.zeros((LANES, 2 * GLA_QK), F32)
            a2 = a2.at[0:GLA_RANK, 0:GLA_QK].set(ev_gla_w_a2[i, 0])
            a2 = a2.at[GLA_RANK:2 * GLA_RANK, GLA_QK:].set(ev_gla_w_a2[i, 1])
            ba2 = ev_gla_b_a2[i].reshape(1, 2 * GLA_QK)
            qg = jnp.tile(ev_na_q_g[i], NA_HEADS).reshape(1, NA_W)
            kg = jnp.tile(ev_na_k_g[i], NA_HEADS).reshape(1, NA_W)
            gq, gk, gv, gate, la, nq, nk, nv = _proj_even(xa, mods, gn, w_bf, a2, ba2, qg, kg, n_ctx_tiles)
            o_fwd, o_bwd = _gla(gq, gk, gv, la, n_ctx)
            o_na = _na(nq, nk, nv, _na_bias_table(ev_na_rpb[i], kh), n_ctx)
            mixer_outs = (o_fwd, o_bwd, gate, o_na)
            extra = ev_gla_norm_g[i].reshape(1, GLA_DV)
            w_out = ev_w_out[i].astype(BF16)
        else:
            w_bf = od_w_in[i].astype(BF16)
            qg = jnp.tile(od_q_g[i], SW_HEADS).reshape(1, SW_Q)
            kg = jnp.tile(od_k_g[i], SW_KV_HEADS).reshape(1, SW_KV)
            q, k, v = _proj_odd(xa, mods, gn, w_bf, qg, kg, cos, sin, n_ctx_tiles)
            mixer_outs = (_swa(q, k, v, od_sinks[i], n_ctx),)
            extra = None
            w_out = od_w_out[i].astype(BF16)
        xn, f, idx, wts, rank, cnt, tile_base = _post(
            layer % 2 == 0, mixer_outs, extra, w_out, xa, mods, norm_ffn_g[layer].reshape(1, d),
            jnp.stack(_split_bf16(moe_w_router[layer])), moe_b_router[layer].reshape(1, N_EXPERTS), n_ctx_tiles)
        xa = _moe(xn, f, idx, wts, rank, cnt, tile_base, mods, layer, moe_w_gate_up, moe_b_gate_up, moe_w_down, moe_b_down,
                  n_ctx_tiles, latent_only=layer == depth - 1)
    return xa
```

```python
import functools

import jax
import jax.numpy as jnp
import numpy as np
from jax import lax
from jax.experimental import pallas as pl
from jax.experimental.pallas import tpu as pltpu

F32 = jnp.float32
BF16 = jnp.bfloat16
I32 = jnp.int32
HIGHEST = lax.Precision.HIGHEST

GRID_W = 64
HEAD_DIM = 64
EPS = 1e-6
GLA_HEADS = 4
GLA_DK = 64
GLA_DV = 128
GLA_RANK = 16
GLA_TAU = 16.0
GLA_CHUNK = 64
GLA_SUB = 16
NA_HEADS = 8
NA_KH = 8
NA_KW = 16
SW_HEADS = 16
SW_KV_HEADS = 2
SW_WINDOW = 128
SW_BLOCK = 128
ROPE_BASE = 10000.0
N_EXPERTS = 32
TOP_K = 4
SWIGLU_LIMIT = 7.0
SWIGLU_ALPHA = 1.702

GLA_QK = GLA_HEADS * GLA_DK
GLA_V = GLA_HEADS * GLA_DV
NA_W = NA_HEADS * HEAD_DIM
SW_Q = SW_HEADS * HEAD_DIM
SW_KV = SW_KV_HEADS * HEAD_DIM

LANES = 128
MOE_ALIGN = 16
VMEM_LIMIT_BYTES = 56 * 1024 * 1024
NEG = -1e30

TOKEN_TILE = 256
EXPERT_BLOCK = 512
MOE_CHUNK = 16
MOD_ROWS = 16


def _dot(a, b, **kw):
    return jnp.dot(a, b, preferred_element_type=F32, **kw)


def _dot_nt(a, b):
    return lax.dot_general(a, b, (((1,), (1,)), ((), ())), preferred_element_type=F32)


def _dot_tn(a, b):
    return lax.dot_general(a, b, (((0,), (0,)), ((), ())), preferred_element_type=F32)


def _split_bf16(x):
    hi = x.astype(BF16)
    lo = (x - hi.astype(F32)).astype(BF16)
    return hi, lo


def _params(sem, vmem=VMEM_LIMIT_BYTES):
    return pltpu.CompilerParams(dimension_semantics=sem, vmem_limit_bytes=vmem)


def _mod_kernel(cv_ref, w_ref, b_ref, o_ref):
    cv = cv_ref[...]
    s = cv * jax.nn.sigmoid(cv)
    o_ref[...] = _dot(s, w_ref[...], precision=HIGHEST) + b_ref[...]


def _modulation(cv, mod_w, mod_b):
    depth, d, n = mod_w.shape
    tn = 1536
    return pl.pallas_call(
        _mod_kernel,
        grid=(depth, n // tn),
        in_specs=[
            pl.BlockSpec((MOD_ROWS, d), lambda l, j: (0, 0)),
            pl.BlockSpec((None, d, tn), lambda l, j: (l, 0, j)),
            pl.BlockSpec((None, 1, tn), lambda l, j: (l, 0, j)),
        ],
        out_specs=pl.BlockSpec((None, MOD_ROWS, tn), lambda l, j: (l, 0, j)),
        out_shape=jax.ShapeDtypeStruct((depth, MOD_ROWS, n), F32),
        compiler_params=_params(("arbitrary", "arbitrary")),
        name="modulation",
    )(cv, mod_w, mod_b.reshape(depth, 1, n))


def _norm_mod(x, g_row, m_ref, shift_row, scale_row):
    ms = jnp.mean(x * x, axis=-1, keepdims=True)
    h = x * lax.rsqrt(ms + EPS) * g_row
    return h * (1.0 + m_ref[scale_row:scale_row + 1, :]) + m_ref[shift_row:shift_row + 1, :]


def _head_pair_ones():
    r = lax.broadcasted_iota(I32, (LANES, LANES), 0) // HEAD_DIM
    c = lax.broadcasted_iota(I32, (LANES, LANES), 1) // HEAD_DIM
    return jnp.where(r == c, 1.0, 0.0).astype(BF16)


def _head_rms(y, g_row, ones_blk):
    outs = []
    for j in range(y.shape[1] // LANES):
        s = y[:, j * LANES:(j + 1) * LANES]
        hi, lo = _split_bf16(s * s)
        ms = (_dot(hi, ones_blk) + _dot(lo, ones_blk)) * (1.0 / HEAD_DIM)
        outs.append(s * lax.rsqrt(ms + EPS))
    return jnp.concatenate(outs, axis=1) * g_row


def _mod_row_map(n_ctx_tiles, n_batch):
    return lambda b, t: (jnp.where(t < n_ctx_tiles, n_batch, b), 0, 0)


def _proj_even_kernel(x_ref, m_ref, gn_ref, w_ref, a2_ref, ba2_ref, qg_ref, kg_ref,
                      gq_ref, gk_ref, gv_ref, gate_ref, la_ref, nq_ref, nk_ref, nv_ref):
    h = _norm_mod(x_ref[...], gn_ref[...], m_ref, 0, 1)
    y = _dot(h.astype(BF16), w_ref[...])
    c0 = 0
    gq_ref[...] = y[:, c0:c0 + GLA_QK] * (GLA_DK ** -0.5)
    c0 += GLA_QK
    gk_ref[...] = y[:, c0:c0 + GLA_QK]
    c0 += GLA_QK
    gv_ref[...] = y[:, c0:c0 + GLA_V].astype(BF16)
    c0 += GLA_V
    gate_ref[...] = y[:, c0:c0 + GLA_V].astype(BF16)
    c0 += GLA_V
    ones_blk = _head_pair_ones()
    nq_ref[...] = (_head_rms(y[:, c0:c0 + NA_W], qg_ref[...], ones_blk) * (HEAD_DIM ** -0.5)).astype(BF16)
    c0 += NA_W
    nk_ref[...] = _head_rms(y[:, c0:c0 + NA_W], kg_ref[...], ones_blk).astype(BF16)
    c0 += NA_W
    nv_ref[...] = y[:, c0:c0 + NA_W].astype(BF16)
    c0 += NA_W
    a = y[:, c0:c0 + LANES]
    z = _dot(a, a2_ref[...], precision=HIGHEST) + ba2_ref[...]
    la_ref[...] = (jnp.minimum(z, 0.0) - jnp.log1p(jnp.exp(-jnp.abs(z)))) * (1.0 / GLA_TAU)


def _proj_even(xa, mods, gn, w_bf, a2, ba2, qg, kg, n_ctx_tiles):
    b, tt, d = xa.shape
    tm = TOKEN_TILE
    n = w_bf.shape[1]
    tok = lambda w: pl.BlockSpec((None, tm, w), lambda bi, t: (bi, t, 0))
    full = lambda shape: pl.BlockSpec(shape, lambda bi, t: (0,) * len(shape))
    widths = [(GLA_QK, F32), (GLA_QK, F32), (GLA_V, BF16), (GLA_V, BF16), (2 * GLA_QK, F32),
              (NA_W, BF16), (NA_W, BF16), (NA_W, BF16)]
    return pl.pallas_call(
        _proj_even_kernel,
        grid=(b, tt // tm),
        in_specs=[
            tok(d),
            pl.BlockSpec((None, 6, d), _mod_row_map(n_ctx_tiles, b)),
            full((1, d)), full((d, n)), full((LANES, 2 * GLA_QK)), full((1, 2 * GLA_QK)),
            full((1, NA_W)), full((1, NA_W)),
        ],
        out_specs=[tok(w) for w, _ in widths],
        out_shape=[jax.ShapeDtypeStruct((b, tt, w), dt) for w, dt in widths],
        compiler_params=_params(("parallel", "parallel")),
        name="proj_even",
    )(xa, mods, gn, w_bf, a2, ba2, qg, kg)


def _rope(y, cos, sin_signed, lo_mask):
    w = y.shape[1]
    reps = w // LANES
    cos_t = jnp.concatenate([cos] * reps, axis=1) if reps > 1 else cos
    sin_t = jnp.concatenate([sin_signed] * reps, axis=1) if reps > 1 else sin_signed
    msk = jnp.concatenate([lo_mask] * reps, axis=1) if reps > 1 else lo_mask
    quarter = HEAD_DIM // 4
    swapped = jnp.where(msk, pltpu.roll(y, w - quarter, 1), pltpu.roll(y, quarter, 1))
    return y * cos_t + swapped * sin_t


def _proj_odd_kernel(x_ref, m_ref, gn_ref, w_ref, qg_ref, kg_ref, cos_ref, sin_ref,
                     q_ref, k_ref, v_ref):
    h = _norm_mod(x_ref[...], gn_ref[...], m_ref, 0, 1)
    y = _dot(h.astype(BF16), w_ref[...])
    ones_blk = _head_pair_ones()
    lane = lax.broadcasted_iota(I32, (1, LANES), 1)
    lo_mask = (lane % (HEAD_DIM // 2)) < (HEAD_DIM // 4)
    cos = cos_ref[...]
    sin = sin_ref[...]
    q = _head_rms(y[:, 0:SW_Q], qg_ref[...], ones_blk)
    q_ref[...] = (_rope(q, cos, sin, lo_mask) * (HEAD_DIM ** -0.5)).astype(BF16)
    k = _head_rms(y[:, SW_Q:SW_Q + SW_KV], kg_ref[...], ones_blk)
    k_ref[...] = _rope(k, cos, sin, lo_mask).astype(BF16)
    v_ref[...] = y[:, SW_Q + SW_KV:SW_Q + 2 * SW_KV].astype(BF16)


def _proj_odd(xa, mods, gn, w_bf, qg, kg, cos, sin, n_ctx_tiles):
    b, tt, d = xa.shape
    tm = TOKEN_TILE
    n = w_bf.shape[1]
    tok = lambda w: pl.BlockSpec((None, tm, w), lambda bi, t: (bi, t, 0))
    full = lambda shape: pl.BlockSpec(shape, lambda bi, t: (0,) * len(shape))
    pos = pl.BlockSpec((tm, LANES), lambda bi, t: (t, 0))
    widths = [SW_Q, SW_KV, SW_KV]
    return pl.pallas_call(
        _proj_odd_kernel,
        grid=(b, tt // tm),
        in_specs=[
            tok(d),
            pl.BlockSpec((None, 6, d), _mod_row_map(n_ctx_tiles, b)),
            full((1, d)), full((d, n)), full((1, SW_Q)), full((1, SW_KV)), pos, pos,
        ],
        out_specs=[tok(w) for w in widths],
        out_shape=[jax.ShapeDtypeStruct((b, tt, w), BF16) for w in widths],
        compiler_params=_params(("parallel", "parallel")),
        name="proj_odd",
    )(xa, mods, gn, w_bf, qg, kg, cos, sin)


def _gla_direction(backward, q_ref, k_ref, v_ref, g_ref, o_ref, st_ref):
    c = GLA_CHUNK
    assert c == GLA_DK
    n_sub = c // GLA_SUB
    n_chunks = q_ref.shape[0] // c
    sgn = -1 if backward else 1
    r = lax.broadcasted_iota(I32, (2 * c, c), 0)
    j = lax.broadcasted_iota(I32, (2 * c, c), 1)
    i = r % c
    incl = jnp.where(sgn * (i - j) >= 0, 1.0, 0.0)
    blk = jnp.where(sgn * (i // GLA_SUB - j // GLA_SUB) > 0, 1.0, 0.0)
    scan_mat = jnp.where(r >= c, blk, incl).astype(BF16)
    causal = scan_mat[0:c, :].astype(F32)
    row = lax.broadcasted_iota(I32, (c, 1), 0)
    reachable = [sgn * ((a * GLA_SUB if backward else a * GLA_SUB + GLA_SUB - 1) - row) >= 0 for a in range(n_sub)]
    lane_head = lax.broadcasted_iota(I32, (GLA_HEADS * c, GLA_QK), 1) // GLA_DK
    row_idx = lax.broadcasted_iota(I32, (GLA_HEADS * c, GLA_QK), 0)
    own_chunk = jnp.where(row_idx // c == lane_head, 1.0, 0.0).astype(BF16)
    sub_shape = (GLA_HEADS * GLA_SUB, GLA_QK)
    own_sub_f32 = jnp.where(lax.broadcasted_iota(I32, sub_shape, 0) // GLA_SUB
                            == lax.broadcasted_iota(I32, sub_shape, 1) // GLA_DK, 1.0, 0.0)
    own_sub = own_sub_f32.astype(BF16)
    ones_cols = jnp.ones((c, GLA_DV), BF16)
    last = 0 if backward else c - 1

    for u in range(n_chunks):
        ci = n_chunks - 1 - u if backward else u
        rows = slice(ci * c, (ci + 1) * c)
        g_hi, g_lo = _split_bf16(g_ref[rows, :])
        cr = _dot(scan_mat, g_hi) + _dot(scan_mat, g_lo)
        cum = cr[0:c, :]
        ref_pt = cr[c:2 * c, :]
        tot = cum[last:last + 1, :]
        tot_rows = _dot_tn(g_hi, ones_cols) + _dot_tn(g_lo, ones_cols)

        q = q_ref[rows, :]
        k = k_ref[rows, :]
        q_state = (q * jnp.exp(cum)).astype(BF16)
        q_sub = (q * jnp.exp(cum - ref_pt)).astype(BF16)
        k_state = (k * jnp.exp(tot - cum)).astype(BF16)

        placed = []
        for a in range(n_sub):
            lo = a * GLA_SUB
            k_sub = (k * jnp.exp(jnp.where(reachable[a], ref_pt[lo:lo + 1, :] - cum, 0.0))).astype(BF16)
            q4 = jnp.concatenate([q_sub[lo:lo + GLA_SUB, :]] * GLA_HEADS, axis=0) * own_sub
            a4 = _dot_nt(q4, k_sub) * jnp.concatenate([causal[lo:lo + GLA_SUB, :]] * GLA_HEADS, axis=0)
            placed.append(jnp.concatenate([a4] * GLA_HEADS, axis=1) * own_sub_f32)
        attn = jnp.concatenate([placed[a][h * GLA_SUB:(h + 1) * GLA_SUB, :]
                                for h in range(GLA_HEADS) for a in range(n_sub)], axis=0).astype(BF16)
        v = v_ref[rows, :]
        v_rows = jnp.concatenate([v[:, h * GLA_DV:(h + 1) * GLA_DV] for h in range(GLA_HEADS)], axis=0)
        q4s = jnp.concatenate([q_state] * GLA_HEADS, axis=0) * own_chunk
        st = st_ref[...]
        o = _dot(attn, v_rows) + _dot(q4s, st.astype(BF16))
        for h in range(GLA_HEADS):
            o_ref[rows, h * GLA_DV:(h + 1) * GLA_DV] = o[h * c:(h + 1) * c, :]
        kv = _dot_tn(k_state, v)
        kv_own = jnp.concatenate([kv[h * GLA_DK:(h + 1) * GLA_DK, h * GLA_DV:(h + 1) * GLA_DV]
                                  for h in range(GLA_HEADS)], axis=0)
        st_ref[...] = st * jnp.exp(tot_rows) + kv_own


def _gla_kernel(qf_ref, kf_ref, vf_ref, gf_ref, qb_ref, kb_ref, vb_ref, gb_ref, of_ref, ob_ref, stf_ref, stb_ref):
    @pl.when(pl.program_id(1) == 0)
    def _():
        stf_ref[...] = jnp.zeros_like(stf_ref)
        stb_ref[...] = jnp.zeros_like(stb_ref)

    _gla_direction(False, qf_ref, kf_ref, vf_ref, gf_ref, of_ref, stf_ref)
    _gla_direction(True, qb_ref, kb_ref, vb_ref, gb_ref, ob_ref, stb_ref)


def _gla(gq, gk, gv, la, n_ctx):
    b, tt, _ = gq.shape
    rows = TOKEN_TILE
    nb = tt // rows
    n_ctx_blk = n_ctx // rows

    def bwd(s):
        return jnp.where(s < n_ctx_blk, n_ctx_blk - 1 - s, nb - 1 + n_ctx_blk - s)

    def tok(w, col, order):
        return pl.BlockSpec((None, rows, w), lambda bi, s: (bi, order(s), col))

    fwd = lambda s: s
    return pl.pallas_call(
        _gla_kernel,
        grid=(b, nb),
        in_specs=[tok(GLA_QK, 0, fwd), tok(GLA_QK, 0, fwd), tok(GLA_V, 0, fwd), tok(GLA_QK, 0, fwd),
                  tok(GLA_QK, 0, bwd), tok(GLA_QK, 0, bwd), tok(GLA_V, 0, bwd), tok(GLA_QK, 1, bwd)],
        out_specs=[tok(GLA_V, 0, fwd), tok(GLA_V, 0, bwd)],
        out_shape=[jax.ShapeDtypeStruct((b, tt, GLA_V), F32)] * 2,
        scratch_shapes=[pltpu.VMEM((GLA_HEADS * GLA_DK, GLA_DV), F32)] * 2,
        compiler_params=_params(("arbitrary", "arbitrary")),
        name="gla_scan",
    )(gq, gk, gv, la, gq, gk, gv, la)


def _na_bias_table(rpb, kh):
    n_heads = rpb.shape[0]
    qc = np.arange(GRID_W)[:, None]
    kc = np.arange(GRID_W)[None, :]
    win0 = np.clip(qc - NA_KW // 2, 0, GRID_W - NA_KW)
    valid = (kc >= win0) & (kc < win0 + NA_KW)
    side = GRID_W - NA_KW
    rp = jnp.pad(rpb, ((0, 0), (0, 0), (side, side)))
    base = jnp.stack([rp[:, :, GRID_W - 1 - c:2 * GRID_W - 1 - c] for c in range(GRID_W)], axis=2)
    base = jnp.where(valid[None, None], base, NEG)
    tab = jnp.stack([base[:, NA_KH - 1 - a:NA_KH - 1 - a + kh] for a in range(kh)], axis=1)
    tab = tab.reshape(n_heads // 2, 2, kh, kh, GRID_W, GRID_W).transpose(0, 2, 1, 4, 3, 5)
    return tab.reshape(n_heads // 2, kh, 2 * GRID_W, kh * GRID_W).astype(F32)


def _na_kernel(q_ref, k_ref, v_ref, bias_ref, o_ref, *, n_ctx, n_ctx_blk, rows_per_blk, n_rows, kh):
    t = pl.program_id(2)
    lane = lax.broadcasted_iota(I32, (1, LANES), 1)
    first = lane < HEAD_DIM
    kc = k_ref[0:n_ctx, :]
    vc = v_ref[0:n_ctx, :]

    def stack_heads(q):
        z = jnp.zeros_like(q)
        return jnp.concatenate([jnp.where(first, q, z), jnp.where(first, z, q)], axis=0)

    def unstack_heads(o):
        n = o.shape[0] // 2
        return jnp.where(first, o[0:n], o[n:2 * n])

    @pl.when(t < n_ctx_blk)
    def _():
        s = _dot_nt(stack_heads(q_ref[...]), kc)
        m = jnp.max(s, axis=-1, keepdims=True)
        p = jnp.exp(s - m)
        den = jnp.sum(p, axis=-1, keepdims=True)
        o_ref[...] = unstack_heads(_dot(p.astype(BF16), vc) / den).astype(o_ref.dtype)

    @pl.when(t >= n_ctx_blk)
    def _():
        for i in range(rows_per_blk):
            r = (t - n_ctx_blk) * rows_per_blk + i
            row0 = jnp.clip(r - kh // 2, 0, n_rows - kh)
            cls = r - row0
            start = pl.multiple_of(n_ctx + row0 * GRID_W, GRID_W)
            kb = k_ref[pl.ds(start, kh * GRID_W), :]
            vb = v_ref[pl.ds(start, kh * GRID_W), :]
            qs = stack_heads(q_ref[i * GRID_W:(i + 1) * GRID_W, :])
            s = _dot_nt(qs, kb) + bias_ref[cls]
            sc = _dot_nt(qs, kc)
            m = jnp.maximum(jnp.max(s, axis=-1, keepdims=True), jnp.max(sc, axis=-1, keepdims=True))
            p = jnp.exp(s - m)
            pc = jnp.exp(sc - m)
            den = jnp.sum(p, axis=-1, keepdims=True) + jnp.sum(pc, axis=-1, keepdims=True)
            o = (_dot(p.astype(BF16), vb) + _dot(pc.astype(BF16), vc)) / den
            o_ref[i * GRID_W:(i + 1) * GRID_W, :] = unstack_heads(o).astype(o_ref.dtype)


def _na(nq, nk, nv, bias_tab, n_ctx):
    b, tt, _ = nq.shape
    tq = TOKEN_TILE
    n_rows = (tt - n_ctx) // GRID_W
    kh = bias_tab.shape[1]
    kern = functools.partial(_na_kernel, n_ctx=n_ctx, n_ctx_blk=n_ctx // tq, rows_per_blk=tq // GRID_W,
                             n_rows=n_rows, kh=kh)
    return pl.pallas_call(
        kern,
        grid=(b, NA_HEADS // 2, tt // tq),
        in_specs=[
            pl.BlockSpec((None, tq, LANES), lambda bi, p, t: (bi, t, p)),
            pl.BlockSpec((None, tt, LANES), lambda bi, p, t: (bi, 0, p)),
            pl.BlockSpec((None, tt, LANES), lambda bi, p, t: (bi, 0, p)),
            pl.BlockSpec((None, kh, 2 * GRID_W, kh * GRID_W), lambda bi, p, t: (p, 0, 0, 0)),
        ],
        out_specs=pl.BlockSpec((None, tq, LANES), lambda bi, p, t: (bi, t, p)),
        out_shape=jax.ShapeDtypeStruct((b, tt, NA_W), BF16),
        compiler_params=_params(("parallel", "parallel", "arbitrary")),
        name="na_attn",
    )(nq, nk, nv, bias_tab)


def _swa_kernel(sink_ref, q_ref, k_ref, v_ref, o_ref, *, n_ctx, n_lat_blk):
    blk = SW_BLOCK
    n = pl.program_id(1)
    n_ctx_blk = n_ctx // blk
    jb = n - n_ctx_blk
    sb = jnp.clip(jb - 1, 0, n_lat_blk - 3)
    start = pl.multiple_of(n_ctx + sb * blk, blk)
    keys = jnp.concatenate([k_ref[pl.ds(start, 3 * blk), :], k_ref[0:n_ctx, :]], axis=0)
    vals = jnp.concatenate([v_ref[pl.ds(start, 3 * blk), :], v_ref[0:n_ctx, :]], axis=0)
    nk = 3 * blk + n_ctx
    iq = lax.broadcasted_iota(I32, (blk, nk), 0)
    ik = lax.broadcasted_iota(I32, (blk, nk), 1)
    offset = jnp.where(n < n_ctx_blk, 4 * SW_WINDOW, (jb - sb) * blk)
    dist = jnp.where(ik >= 3 * blk, 0, iq - ik + offset)
    bias = jnp.where(jnp.abs(dist) <= SW_WINDOW, 0.0, NEG)
    q = q_ref[...]
    group = SW_HEADS // SW_KV_HEADS
    bias_g = jnp.concatenate([bias] * group, axis=0)
    lane = lax.broadcasted_iota(I32, (1, LANES), 1)
    assert SW_KV_HEADS * HEAD_DIM == LANES
    for g in range(SW_KV_HEADS):
        gs = slice(g * HEAD_DIM, (g + 1) * HEAD_DIM)
        in_g = (lane // HEAD_DIM) == g
        heads = range(g * group, (g + 1) * group)
        blocks = []
        for h in heads:
            two = q[:, (h // 2) * LANES:(h // 2 + 1) * LANES]
            if h % 2 != g:
                two = jnp.concatenate([two[:, HEAD_DIM:], two[:, :HEAD_DIM]], axis=1)
            blocks.append(jnp.where(in_g, two, jnp.zeros_like(two)))
        qg = jnp.concatenate(blocks, axis=0)
        sink = jnp.concatenate([jnp.full((blk, 1), sink_ref[h], F32) for h in heads], axis=0)
        s = _dot_nt(qg, keys) + bias_g
        m = jnp.maximum(jnp.max(s, axis=-1, keepdims=True), sink)
        p = jnp.exp((s - m).astype(BF16))
        o2 = _dot(p, jnp.where(in_g, vals, jnp.ones_like(vals)))
        other = (1 - g) * HEAD_DIM
        den = o2[:, other:other + 1] + jnp.exp(sink - m)
        o = o2[:, gs] / den
        for pr in range(group // 2):
            pair = [o[(2 * pr + e) * blk:(2 * pr + e + 1) * blk, :] for e in range(2)]
            col = (g * group // 2 + pr) * LANES
            o_ref[:, col:col + LANES] = jnp.concatenate(pair, axis=1).astype(o_ref.dtype)


def _swa(q, k, v, sinks, n_ctx):
    b, tt, _ = q.shape
    blk = SW_BLOCK
    kern = functools.partial(_swa_kernel, n_ctx=n_ctx, n_lat_blk=(tt - n_ctx) // blk)
    return pl.pallas_call(
        kern,
        grid=(b, tt // blk),
        in_specs=[
            pl.BlockSpec(memory_space=pltpu.SMEM),
            pl.BlockSpec((None, blk, SW_Q), lambda bi, n: (bi, n, 0)),
            pl.BlockSpec((None, tt, SW_KV), lambda bi, n: (bi, 0, 0)),
            pl.BlockSpec((None, tt, SW_KV), lambda bi, n: (bi, 0, 0)),
        ],
        out_specs=pl.BlockSpec((None, blk, SW_Q), lambda bi, n: (bi, n, 0)),
        out_shape=jax.ShapeDtypeStruct((b, tt, SW_Q), BF16),
        compiler_params=_params(("parallel", "arbitrary")),
        name="swa_attn",
    )(sinks, q, k, v)


def _route(f, wr_ref, br_ref, before_ref, run_ref, idx_ref, wt_ref, rank_ref, cnt_ref, base_ref):
    tm = f.shape[0]
    f_hi, f_lo = _split_bf16(f)
    logits = _dot(f_hi, wr_ref[0]) + (_dot(f_lo, wr_ref[0]) + _dot(f_hi, wr_ref[1])) + br_ref[...]
    e_iota = lax.broadcasted_iota(I32, (tm, N_EXPERTS), 1).astype(F32)
    lane8 = lax.broadcasted_iota(I32, (tm, 8), 1)
    vals, hots = [], []
    idx_out = jnp.zeros((tm, 8), I32)
    for k in range(TOP_K):
        mx = jnp.max(logits, axis=-1, keepdims=True)
        am = jnp.min(jnp.where(logits == mx, e_iota, float(N_EXPERTS)), axis=-1, keepdims=True)
        hot = e_iota == am
        vals.append(mx)
        hots.append(hot)
        idx_out = jnp.where(lane8 == k, am.astype(I32), idx_out)
        logits = jnp.where(hot, -jnp.inf, logits)
    exps = [jnp.exp(v - vals[0]) for v in vals]
    den = exps[0] + exps[1] + exps[2] + exps[3]
    multihot = jnp.zeros((tm, N_EXPERTS), F32)
    wt_out = jnp.zeros((tm, 8), F32)
    for k in range(TOP_K):
        multihot = multihot + jnp.where(hots[k], 1.0, 0.0)
        wt_out = jnp.where(lane8 == k, exps[k] / den, wt_out)
    base = _dot(before_ref[...], multihot.astype(BF16)) + run_ref[...]
    rank_out = jnp.zeros((tm, 8), I32)
    for k in range(TOP_K):
        rk = jnp.sum(jnp.where(hots[k], base, 0.0), axis=-1, keepdims=True)
        rank_out = jnp.where(lane8 == k, rk.astype(I32), rank_out)
    base_ref[...] = run_ref[...]
    run_ref[...] = run_ref[...] + jnp.sum(multihot, axis=0, keepdims=True)
    idx_ref[...] = idx_out
    wt_ref[...] = wt_out
    rank_ref[...] = rank_out
    cnt_ref[...] = run_ref[...]


def _post_common(mix, x_ref, m_ref, fg_ref, wr_ref, br_ref, before_ref, xn_ref, f_ref, idx_ref, wt_ref, rank_ref,
                 cnt_ref, base_ref, run_ref):
    @pl.when((pl.program_id(0) == 0) & (pl.program_id(1) == 0))
    def _():
        run_ref[...] = jnp.zeros_like(run_ref)

    xn = x_ref[...] + m_ref[2:3, :] * mix
    xn_ref[...] = xn
    f = _norm_mod(xn, fg_ref[...], m_ref, 3, 4)
    f_ref[...] = f.astype(f_ref.dtype)
    _route(f, wr_ref, br_ref, before_ref, run_ref, idx_ref, wt_ref, rank_ref, cnt_ref, base_ref)


def _post_even_kernel(of_ref, ob_ref, gate_ref, na_ref, gng_ref, w_ref, x_ref, m_ref, fg_ref, wr_ref, br_ref,
                      before_ref, xn_ref, f_ref, idx_ref, wt_ref, rank_ref, cnt_ref, base_ref, run_ref):
    o = of_ref[...] + ob_ref[...]
    gate = gate_ref[...].astype(F32)
    gn = gng_ref[...]
    parts = []
    for h in range(GLA_HEADS):
        oh = o[:, h * GLA_DV:(h + 1) * GLA_DV]
        ms = jnp.mean(oh * oh, axis=-1, keepdims=True)
        parts.append(oh * lax.rsqrt(ms + EPS) * gn)
    gla = jnp.concatenate(parts, axis=1) * (gate * jax.nn.sigmoid(gate))
    mix = _dot(gla.astype(BF16), w_ref[0:GLA_V, :]) + _dot(na_ref[...], w_ref[GLA_V:GLA_V + NA_W, :])
    _post_common(mix, x_ref, m_ref, fg_ref, wr_ref, br_ref, before_ref, xn_ref, f_ref, idx_ref, wt_ref, rank_ref,
                 cnt_ref, base_ref, run_ref)


def _post_odd_kernel(o_ref, w_ref, x_ref, m_ref, fg_ref, wr_ref, br_ref, before_ref,
                     xn_ref, f_ref, idx_ref, wt_ref, rank_ref, cnt_ref, base_ref, run_ref):
    mix = _dot(o_ref[...], w_ref[...])
    _post_common(mix, x_ref, m_ref, fg_ref, wr_ref, br_ref, before_ref, xn_ref, f_ref, idx_ref, wt_ref, rank_ref,
                 cnt_ref, base_ref, run_ref)


def _post(even, mixer_outs, extra, w_bf, xa, mods, fg, wr, br, n_ctx_tiles):
    b, tt, d = xa.shape
    tm = TOKEN_TILE
    tok = lambda w: pl.BlockSpec((None, tm, w), lambda bi, t: (bi, t, 0))
    full = lambda shape: pl.BlockSpec(shape, lambda bi, t: (0,) * len(shape))
    if even:
        o_fwd, o_bwd, gate, na = mixer_outs
        lead_specs = [tok(GLA_V), tok(GLA_V), tok(GLA_V), tok(NA_W), full((1, GLA_DV))]
        lead_args = [o_fwd, o_bwd, gate, na, extra]
        kern = _post_even_kernel
    else:
        (o,) = mixer_outs
        lead_specs = [tok(SW_Q)]
        lead_args = [o]
        kern = _post_odd_kernel
    nt = tt // tm
    small = lambda: pl.BlockSpec((None, tm, 8), lambda bi, t: (bi, t, 0))
    before = jnp.asarray(np.tril(np.ones((tm, tm), np.float32), -1), BF16)
    return pl.pallas_call(
        kern,
        grid=(b, tt // tm),
        in_specs=lead_specs + [
            full(w_bf.shape), tok(d),
            pl.BlockSpec((None, 6, d), _mod_row_map(n_ctx_tiles, b)),
            full((1, d)), full((2, d, N_EXPERTS)), full((1, N_EXPERTS)), full((tm, tm)),
        ],
        out_specs=[tok(d), tok(d), small(), small(), small(), full((1, N_EXPERTS)),
                   pl.BlockSpec((None, 1, N_EXPERTS), lambda bi, t: (bi * nt + t, 0, 0))],
        out_shape=[
            jax.ShapeDtypeStruct((b, tt, d), F32), jax.ShapeDtypeStruct((b, tt, d), BF16),
            jax.ShapeDtypeStruct((b, tt, 8), I32), jax.ShapeDtypeStruct((b, tt, 8), F32),
            jax.ShapeDtypeStruct((b, tt, 8), I32), jax.ShapeDtypeStruct((1, N_EXPERTS), F32),
            jax.ShapeDtypeStruct((b * nt, 1, N_EXPERTS), F32),
        ],
        scratch_shapes=[pltpu.VMEM((1, N_EXPERTS), F32)],
        compiler_params=_params(("arbitrary", "arbitrary")),
        name="post_even" if even else "post_odd",
    )(*lead_args, w_bf, xa, mods, fg, wr, br, before)


def _stage_rows(tm):
    return tm * TOP_K + N_EXPERTS * (MOE_CHUNK + MOE_ALIGN)


WAIT_BATCH = 4


def _for_tile_chunks(rows_ref, tot_ref, g, max_chunks, fn):
    def per_chunk(j, c):
        fn(pl.multiple_of(rows_ref[g * max_chunks + j], MOE_ALIGN), pl.multiple_of(j * MOE_CHUNK, MOE_CHUNK))
        return c

    lax.fori_loop(0, tot_ref[g], per_chunk, 0)


STAGE_GROUP = 2 * LANES
STAGE_MIN_GROUPS = 6


def _for_used_groups(used_rows, max_groups, body):
    for n in range(STAGE_MIN_GROUPS, max_groups + 1):
        lo = 0 if n == STAGE_MIN_GROUPS else (n - 1) * STAGE_GROUP
        hi = n * STAGE_GROUP
        take = (used_rows > lo) & (used_rows <= hi) if lo else used_rows <= hi

        @pl.when(take)
        def _(n=n):
            body(n)


def _wait_chunks(n, make_copy):
    def batch(i, c):
        make_copy(WAIT_BATCH * MOE_CHUNK).wait()
        return c

    def single(i, c):
        make_copy(MOE_CHUNK).wait()
        return c

    lax.fori_loop(0, n // WAIT_BATCH, batch, 0)
    lax.fori_loop(0, n % WAIT_BATCH, single, 0)


def _dispatch_kernel(rows_ref, tot_ref, nch_ref, off_ref, lead_ref, cnt_ref, lo_ref, hi_ref, colt_ref, f_ref, xs_ref,
                     stage, carry, zeros_ref, sems, zsem):
    g = pl.program_id(0)
    n_tiles = pl.num_programs(0)
    ch = MOE_CHUNK
    buf = g % 2
    max_chunks = stage.shape[1] // ch

    def chunk_copy(which, slot_row, stage_row, rows=ch):
        return pltpu.make_async_copy(stage.at[which, pl.ds(stage_row, rows)], xs_ref.at[pl.ds(slot_row, rows)],
                                     sems.at[which])

    def wait_tile(tile, which):
        _wait_chunks(tot_ref[tile], lambda rows: chunk_copy(which, 0, 0, rows))

    @pl.when(g == 0)
    def _():
        carry[...] = jnp.zeros_like(carry)
        zeros_ref[...] = jnp.zeros_like(zeros_ref)
        stage[...] = jnp.zeros_like(stage)

        def pad_pass(do):
            def per_expert(e, c):
                lo = lo_ref[e]
                hi = hi_ref[e]
                n_full = (hi - lo) // ch

                def full(j, c2):
                    do(pltpu.make_async_copy(
                        zeros_ref, xs_ref.at[pl.ds(pl.multiple_of(hi - (j + 1) * ch, MOE_ALIGN), ch)], zsem))
                    return c2

                lax.fori_loop(0, n_full, full, 0)

                @pl.when((hi - lo) - n_full * ch > 0)
                def _():
                    do(pltpu.make_async_copy(zeros_ref.at[pl.ds(0, MOE_ALIGN)],
                                             xs_ref.at[pl.ds(pl.multiple_of(lo, MOE_ALIGN), MOE_ALIGN)], zsem))
                return c

            lax.fori_loop(0, lo_ref.shape[0], per_expert, 0)

        pad_pass(lambda cp: cp.start())
        pad_pass(lambda cp: cp.wait())

    colt = colt_ref[...]
    group = STAGE_GROUP

    def fill_stage(n_groups):
        for j in range(n_groups):
            r_iota = lax.broadcasted_iota(I32, (group, colt.shape[1]), 0) + j * group
            hit = jnp.zeros(r_iota.shape, F32)
            for k in range(TOP_K):
                hit = jnp.where(r_iota == colt[k:k + 1, :], 1.0, hit)
            stage[buf, j * group:(j + 1) * group, :] = _dot(hit.astype(BF16), f_ref[...]).astype(stage.dtype)

    _for_used_groups(tot_ref[g] * ch, stage.shape[1] // group, fill_stage)

    for e in range(N_EXPERTS):
        i = g * N_EXPERTS + e
        active = nch_ref[i] > 0
        ptr = off_ref[i]
        head = pl.ds(pl.multiple_of(ptr, MOE_ALIGN), MOE_ALIGN)
        old = carry[e]
        stage[buf, head, :] = stage[buf, head, :] + jnp.where(active, old, jnp.zeros_like(old))
        end = lead_ref[i] + cnt_ref[i]
        new_lead = end % MOE_ALIGN
        tail = pl.ds(pl.multiple_of(ptr + end - new_lead, MOE_ALIGN), MOE_ALIGN)
        kept = jnp.where(new_lead > 0, stage[buf, tail, :], jnp.zeros_like(old))
        carry[e] = jnp.where(active, kept, old)

    @pl.when(g > 0)
    def _():
        wait_tile(g - 1, 1 - buf)

    _for_tile_chunks(rows_ref, tot_ref, g, max_chunks,
                     lambda slot_row, stage_row: chunk_copy(buf, slot_row, stage_row).start())

    @pl.when(g == n_tiles - 1)
    def _():
        wait_tile(g, buf)


def _dispatch(rows, tot, nch, off, lead, cnt, pad_lo, pad_hi, colt, f2, n_slots):
    t, d = f2.shape
    tm = TOKEN_TILE
    rs = _stage_rows(tm)
    return pl.pallas_call(
        _dispatch_kernel,
        grid_spec=pltpu.PrefetchScalarGridSpec(
            num_scalar_prefetch=8,
            grid=(t // tm,),
            in_specs=[
                pl.BlockSpec((None, 8, tm), lambda g, *_: (g, 0, 0)),
                pl.BlockSpec((tm, d), lambda g, *_: (g, 0)),
            ],
            out_specs=pl.BlockSpec(memory_space=pl.ANY),
            scratch_shapes=[pltpu.VMEM((2, rs, d), BF16), pltpu.VMEM((N_EXPERTS, MOE_ALIGN, d), BF16),
                            pltpu.VMEM((MOE_CHUNK, d), BF16),
                            pltpu.SemaphoreType.DMA((2,)), pltpu.SemaphoreType.DMA(())],
        ),
        out_shape=jax.ShapeDtypeStruct((n_slots, d), BF16),
        compiler_params=_params(("arbitrary",)),
        name="moe_dispatch",
    )(rows, tot, nch, off, lead, cnt, pad_lo, pad_hi, colt, f2)


def _expert_kernel(be_ref, nu_ref, xs_ref, wgu_ref, bgu_ref, wd_ref, bd_ref, ys_ref, wgu_bf, wd_bf):
    i = pl.program_id(0)
    prev = be_ref[jnp.maximum(i - 1, 0)]
    fresh = (i == 0) | (be_ref[i] != prev)

    @pl.when(fresh & (i < nu_ref[0]))
    def _():
        wgu_bf[...] = wgu_ref[...].astype(BF16)
        wd_bf[...] = wd_ref[...].astype(BF16)

    @pl.when(i < nu_ref[0])
    def _():
        x = xs_ref[...]
        de = wd_bf.shape[0]
        half = de // 2
        acc = None
        for c in range(2):
            glu = _dot(x, wgu_bf[:, c * half:(c + 1) * half]) + bgu_ref[:, c * half:(c + 1) * half]
            lin = _dot(x, wgu_bf[:, de + c * half:de + (c + 1) * half]) + bgu_ref[:, de + c * half:de + (c + 1) * half]
            glu = jnp.minimum(glu, SWIGLU_LIMIT)
            lin = jnp.clip(lin, -SWIGLU_LIMIT, SWIGLU_LIMIT)
            act = (glu * jax.nn.sigmoid(SWIGLU_ALPHA * glu) * (lin + 1.0)).astype(BF16)
            part = _dot(act, wd_bf[c * half:(c + 1) * half, :])
            acc = part if acc is None else acc + part
        ys_ref[...] = (acc + bd_ref[...]).astype(ys_ref.dtype)

    @pl.when(i >= nu_ref[0])
    def _():
        ys_ref[...] = jnp.zeros_like(ys_ref)


def _experts(block_e, n_used, xs, layer, wgu, bgu, wd, bd):
    n_slots, d = xs.shape
    bm = EXPERT_BLOCK
    depth, ne, _, de2 = wgu.shape
    de = wd.shape[2]
    row_blk = lambda i, be, nu: (jnp.minimum(i, nu[0] - 1), 0)
    exp_blk = lambda i, be, nu: (layer, be[i], 0, 0)
    return pl.pallas_call(
        _expert_kernel,
        grid_spec=pltpu.PrefetchScalarGridSpec(
            num_scalar_prefetch=2,
            grid=(n_slots // bm,),
            in_specs=[
                pl.BlockSpec((bm, d), row_blk),
                pl.BlockSpec((None, None, d, de2), exp_blk),
                pl.BlockSpec((None, None, 1, de2), exp_blk),
                pl.BlockSpec((None, None, de, d), exp_blk),
                pl.BlockSpec((None, None, 1, d), exp_blk),
            ],
            out_specs=pl.BlockSpec((bm, d), lambda i, be, nu: (i, 0)),
            scratch_shapes=[pltpu.VMEM((d, de2), BF16), pltpu.VMEM((de, d), BF16)],
        ),
        out_shape=jax.ShapeDtypeStruct((n_slots, d), BF16),
        compiler_params=_params(("arbitrary",)),
        name="moe_experts",
    )(block_e, n_used, xs, wgu, bgu.reshape(depth, ne, 1, de2), wd, bd.reshape(depth, ne, 1, d))


def _combine_kernel(rows_ref, tot_ref, col_ref, wt_ref, x_ref, m_ref, ys_ref, o_ref, stage, sems):
    g = pl.program_id(0)
    n_tiles = pl.num_programs(0)
    ch = MOE_CHUNK
    buf = g % 2
    max_chunks = stage.shape[1] // ch

    def chunk_copy(which, slot_row, stage_row, rows=ch):
        return pltpu.make_async_copy(ys_ref.at[pl.ds(slot_row, rows)], stage.at[which, pl.ds(stage_row, rows)],
                                     sems.at[which])

    def fetch_tile(tile, which):
        _for_tile_chunks(rows_ref, tot_ref, tile, max_chunks,
                         lambda slot_row, stage_row: chunk_copy(which, slot_row, stage_row).start())

    @pl.when(g == 0)
    def _():
        stage[...] = jnp.zeros_like(stage)
        fetch_tile(0, 0)

    @pl.when(g + 1 < n_tiles)
    def _():
        fetch_tile(g + 1, 1 - buf)

    _wait_chunks(tot_ref[g], lambda rows: chunk_copy(buf, 0, 0, rows))

    col = col_ref[...]
    wt = wt_ref[...]
    tm = col.shape[0]
    group = STAGE_GROUP

    def weighted_sum(n_groups):
        y = jnp.zeros(x_ref.shape, F32)
        for j in range(n_groups):
            c_iota = lax.broadcasted_iota(I32, (tm, group), 1) + j * group
            w = jnp.zeros((tm, group), F32)
            for k in range(TOP_K):
                w = jnp.where(c_iota == col[:, k:k + 1], wt[:, k:k + 1], w)
            y = y + _dot(w.astype(BF16), stage[buf, j * group:(j + 1) * group, :])
        o_ref[...] = x_ref[...] + m_ref[5:6, :] * y

    _for_used_groups(tot_ref[g] * ch, stage.shape[1] // group, weighted_sum)


def _combine(rows, tot, col, wts, xn, mods, ys, n_ctx_tiles, latent_only):
    b, tt, d = xn.shape
    tm = TOKEN_TILE
    nt = tt // tm
    rs = _stage_rows(tm)
    assert rs % (2 * LANES) == 0
    if latent_only:
        n_out = nt - n_ctx_tiles
        out_rows = b * n_out * tm
        out_map = lambda g, *_: ((g // nt) * n_out + jnp.maximum(g % nt - n_ctx_tiles, 0), 0)
    else:
        out_rows = b * tt
        out_map = lambda g, *_: (g, 0)
    out = pl.pallas_call(
        _combine_kernel,
        grid_spec=pltpu.PrefetchScalarGridSpec(
            num_scalar_prefetch=2,
            grid=(b * nt,),
            in_specs=[
                pl.BlockSpec((tm, 8), lambda g, *_: (g, 0)),
                pl.BlockSpec((tm, 8), lambda g, *_: (g, 0)),
                pl.BlockSpec((tm, d), lambda g, *_: (g, 0)),
                pl.BlockSpec((None, 6, d), lambda g, *_: (jnp.where(g % nt < n_ctx_tiles, b, g // nt), 0, 0)),
                pl.BlockSpec(memory_space=pl.ANY),
            ],
            out_specs=pl.BlockSpec((tm, d), out_map),
            scratch_shapes=[pltpu.VMEM((2, rs, d), BF16), pltpu.SemaphoreType.DMA((2,))],
        ),
        out_shape=jax.ShapeDtypeStruct((out_rows, d), F32),
        compiler_params=_params(("arbitrary",)),
        name="moe_combine",
    )(rows, tot, col, wts.reshape(b * tt, 8), xn.reshape(b * tt, d), mods, ys)
    return out.reshape(b, out_rows // b, d)


def _moe(xn, f, idx, wts, rank, cnt, tile_base, mods, layer, wgu, bgu, wd, bd, n_ctx_tiles, latent_only):
    b, tt, d = xn.shape
    t = b * tt
    bm = EXPERT_BLOCK
    tm = TOKEN_TILE
    ch = MOE_CHUNK
    n_tiles = t // tm
    n_blocks = (t * TOP_K + N_EXPERTS * ch + bm - 1) // bm + N_EXPERTS
    e_ids = np.arange(N_EXPERTS)
    lower = e_ids[None, :] < e_ids[:, None]
    counts = cnt[0].astype(I32)
    padded = (counts + ch + bm - 1) // bm * bm
    pstart = jnp.sum(jnp.where(lower, padded[None, :], 0), axis=1)
    pend = pstart + padded
    base = tile_base.reshape(n_tiles, N_EXPERTS).astype(I32)
    tile_cnt = jnp.concatenate([base[1:], counts[None, :]], axis=0) - base
    first = pstart[None, :] + base
    seg = first // MOE_ALIGN * MOE_ALIGN
    lead = first - seg
    nch = jnp.where(tile_cnt > 0, (lead + tile_cnt + ch - 1) // ch, 0)
    off = jnp.sum(jnp.where(lower[None], nch[:, None, :], 0), axis=2) * ch
    idx4 = idx[..., :TOP_K].reshape(n_tiles, tm, TOP_K)
    rank4 = rank[..., :TOP_K].reshape(n_tiles, tm, TOP_K)
    delta = (off + lead - base)[:, None, None, :]
    col = rank4 + jnp.sum(jnp.where(idx4[..., None] == e_ids, delta, 0), axis=-1)
    col = jnp.concatenate([col, jnp.full((n_tiles, tm, 8 - TOP_K), -1, I32)], axis=-1)
    colt = col.transpose(0, 2, 1)
    n_used = (pend[N_EXPERTS - 1] // bm).reshape(1).astype(I32)
    starts = np.arange(n_blocks, dtype=np.int32)[:, None] * bm
    block_e = jnp.minimum(jnp.sum((pend[None, :] <= starts).astype(I32), axis=1), N_EXPERTS - 1)
    pad_lo = jnp.concatenate([(pstart + counts + MOE_ALIGN - 1) // MOE_ALIGN * MOE_ALIGN, pend[N_EXPERTS - 1:]])
    pad_hi = jnp.concatenate([pend, jnp.full((1,), n_blocks * bm, I32)])
    max_chunks = _stage_rows(tm) // ch
    cum = off // ch + nch
    j = np.arange(max_chunks, dtype=np.int32)
    e_of_j = jnp.sum((cum[:, None, :] <= j[None, :, None]).astype(I32), axis=-1)
    hot = e_of_j[..., None] == e_ids
    seg_j = jnp.sum(jnp.where(hot, seg[:, None, :], 0), axis=-1)
    first_j = jnp.sum(jnp.where(hot, (off // ch)[:, None, :], 0), axis=-1)
    rows = (seg_j + (j[None, :] - first_j) * ch).reshape(-1)
    tot = cum[:, N_EXPERTS - 1]
    xs = _dispatch(rows, tot, nch.reshape(-1), off.reshape(-1), lead.reshape(-1), tile_cnt.reshape(-1),
                   pad_lo, pad_hi, colt, f.reshape(t, d), n_blocks * bm)
    ys = _experts(block_e, n_used, xs, layer, wgu, bgu, wd, bd)
    return _combine(rows, tot, col.reshape(t, 8), wts, xn, mods, ys, n_ctx_tiles, latent_only)


def _rope_tables(n_ctx, s_len):
    quarter = HEAD_DIM // 4
    t = np.arange(s_len)
    pos = np.stack([t // GRID_W, t % GRID_W], axis=0).astype(np.float32)
    inv_freq = ROPE_BASE ** (-jnp.arange(quarter, dtype=F32) / quarter)
    ang = jnp.asarray(pos)[:, :, None] * inv_freq
    cos, sin = jnp.cos(ang), jnp.sin(ang)
    cos_h = jnp.concatenate([cos[0], cos[0], cos[1], cos[1]], axis=-1)
    sin_h = jnp.concatenate([-sin[0], sin[0], -sin[1], sin[1]], axis=-1)
    reps = LANES // HEAD_DIM
    cos_l = jnp.tile(cos_h, (1, reps))
    sin_l = jnp.tile(sin_h, (1, reps))
    cos_all = jnp.concatenate([jnp.ones((n_ctx, LANES), F32), cos_l], axis=0)
    sin_all = jnp.concatenate([jnp.zeros((n_ctx, LANES), F32), sin_l], axis=0)
    return cos_all, sin_all


def kernel(x, c, ctx, c_ctx, mod_w, mod_b, norm_mix_g, norm_ffn_g, ev_w_in, ev_w_out, ev_gla_w_a2, ev_gla_b_a2,
           ev_gla_norm_g, ev_na_q_g, ev_na_k_g, ev_na_rpb, od_w_in, od_w_out, od_q_g, od_k_g, od_sinks,
           moe_w_router, moe_b_router, moe_w_gate_up, moe_b_gate_up, moe_w_down, moe_b_down):
    b, s_len, d = x.shape
    n_ctx = ctx.shape[1]
    depth = mod_w.shape[0]
    assert b < MOD_ROWS and n_ctx % TOKEN_TILE == 0 and s_len % TOKEN_TILE == 0
    n_ctx_tiles = n_ctx // TOKEN_TILE
    n_rows = s_len // GRID_W
    kh = min(NA_KH, n_rows)

    xa = jnp.concatenate([ctx, x], axis=1)
    cv = jnp.zeros((MOD_ROWS, d), F32).at[:b].set(c).at[b].set(c_ctx)
    mods_all = _modulation(cv, mod_w, mod_b).reshape(depth, MOD_ROWS, 6, d)
    cos, sin = _rope_tables(n_ctx, s_len)

    for layer in range(depth):
        i = layer // 2
        mods = mods_all[layer]
        gn = norm_mix_g[layer].reshape(1, d)
        if layer % 2 == 0:
            w = ev_w_in[i]
            cuts = np.cumsum([0, GLA_QK, GLA_QK, GLA_V, GLA_V, GLA_RANK, GLA_RANK, NA_W, NA_W, NA_W])
            seg = lambda j: w[:, cuts[j]:cuts[j + 1]]
            pad = jnp.zeros((d, LANES - 2 * GLA_RANK), F32)
            w_bf = jnp.concatenate([seg(0), seg(1), seg(2), seg(3), seg(6), seg(7), seg(8), seg(4), seg(5), pad],
                                   axis=1).astype(BF16)
            a2 = jnp.zeros((LANES, 2 * GLA_QK), F32)
            a2 = a2.at[0:GLA_RANK, 0:GLA_QK].set(ev_gla_w_a2[i, 0])
            a2 = a2.at[GLA_RANK:2 * GLA_RANK, GLA_QK:].set(ev_gla_w_a2[i, 1])
            ba2 = ev_gla_b_a2[i].reshape(1, 2 * GLA_QK)
            qg = jnp.tile(ev_na_q_g[i], NA_HEADS).reshape(1, NA_W)
            kg = jnp.tile(ev_na_k_g[i], NA_HEADS).reshape(1, NA_W)
            gq, gk, gv, gate, la, nq, nk, nv = _proj_even(xa, mods, gn, w_bf, a2, ba2, qg, kg, n_ctx_tiles)
            o_fwd, o_bwd = _gla(gq, gk, gv, la, n_ctx)
            o_na = _na(nq, nk, nv, _na_bias_table(ev_na_rpb[i], kh), n_ctx)
            mixer_outs = (o_fwd, o_bwd, gate, o_na)
            extra = ev_gla_norm_g[i].reshape(1, GLA_DV)
            w_out = ev_w_out[i].astype(BF16)
        else:
            w_bf = od_w_in[i].astype(BF16)
            qg = jnp.tile(od_q_g[i], SW_HEADS).reshape(1, SW_Q)
            kg = jnp.tile(od_k_g[i], SW_KV_HEADS).reshape(1, SW_KV)
            q, k, v = _proj_odd(xa, mods, gn, w_bf, qg, kg, cos, sin, n_ctx_tiles)
            mixer_outs = (_swa(q, k, v, od_sinks[i], n_ctx),)
            extra = None
            w_out = od_w_out[i].astype(BF16)
        xn, f, idx, wts, rank, cnt, tile_base = _post(
            layer % 2 == 0, mixer_outs, extra, w_out, xa, mods, norm_ffn_g[layer].reshape(1, d),
            jnp.stack(_split_bf16(moe_w_router[layer])), moe_b_router[layer].reshape(1, N_EXPERTS), n_ctx_tiles)
        xa = _moe(xn, f, idx, wts, rank, cnt, tile_base, mods, layer, moe_w_gate_up, moe_b_gate_up, moe_w_down, moe_b_down,
                  n_ctx_tiles, latent_only=layer == depth - 1)
    return xa
```

```python
import functools

import jax
import jax.numpy as jnp
import numpy as np
from jax import lax
from jax.experimental import pallas as pl
from jax.experimental.pallas import tpu as pltpu

F32 = jnp.float32
BF16 = jnp.bfloat16
I32 = jnp.int32
HIGHEST = lax.Precision.HIGHEST

GRID_W = 64
HEAD_DIM = 64
EPS = 1e-6
GLA_HEADS = 4
GLA_DK = 64
GLA_DV = 128
GLA_RANK = 16
GLA_TAU = 16.0
GLA_CHUNK = 64
GLA_SUB = 16
NA_HEADS = 8
NA_KH = 8
NA_KW = 16
SW_HEADS = 16
SW_KV_HEADS = 2
SW_WINDOW = 128
SW_BLOCK = 128
ROPE_BASE = 10000.0
N_EXPERTS = 32
TOP_K = 4
SWIGLU_LIMIT = 7.0
SWIGLU_ALPHA = 1.702

GLA_QK = GLA_HEADS * GLA_DK
GLA_V = GLA_HEADS * GLA_DV
NA_W = NA_HEADS * HEAD_DIM
SW_Q = SW_HEADS * HEAD_DIM
SW_KV = SW_KV_HEADS * HEAD_DIM

LANES = 128
MOE_ALIGN = 16
VMEM_LIMIT_BYTES = 56 * 1024 * 1024
NEG = -1e30

TOKEN_TILE = 256
EXPERT_BLOCK = 512
MOE_CHUNK = 16
MOD_ROWS = 16


def _dot(a, b, **kw):
    return jnp.dot(a, b, preferred_element_type=F32, **kw)


def _dot_nt(a, b):
    return lax.dot_general(a, b, (((1,), (1,)), ((), ())), preferred_element_type=F32)


def _dot_tn(a, b):
    return lax.dot_general(a, b, (((0,), (0,)), ((), ())), preferred_element_type=F32)


def _split_bf16(x):
    hi = x.astype(BF16)
    lo = (x - hi.astype(F32)).astype(BF16)
    return hi, lo


def _params(sem, vmem=VMEM_LIMIT_BYTES):
    return pltpu.CompilerParams(dimension_semantics=sem, vmem_limit_bytes=vmem)


def _mod_kernel(cv_ref, w_ref, b_ref, o_ref):
    cv = cv_ref[...]
    s = cv * jax.nn.sigmoid(cv)
    o_ref[...] = _dot(s, w_ref[...], precision=HIGHEST) + b_ref[...]


def _modulation(cv, mod_w, mod_b):
    depth, d, n = mod_w.shape
    tn = 1536
    return pl.pallas_call(
        _mod_kernel,
        grid=(depth, n // tn),
        in_specs=[
            pl.BlockSpec((MOD_ROWS, d), lambda l, j: (0, 0)),
            pl.BlockSpec((None, d, tn), lambda l, j: (l, 0, j)),
            pl.BlockSpec((None, 1, tn), lambda l, j: (l, 0, j)),
        ],
        out_specs=pl.BlockSpec((None, MOD_ROWS, tn), lambda l, j: (l, 0, j)),
        out_shape=jax.ShapeDtypeStruct((depth, MOD_ROWS, n), F32),
        compiler_params=_params(("arbitrary", "arbitrary")),
        name="modulation",
    )(cv, mod_w, mod_b.reshape(depth, 1, n))


def _norm_mod(x, g_row, m_ref, shift_row, scale_row):
    ms = jnp.mean(x * x, axis=-1, keepdims=True)
    h = x * lax.rsqrt(ms + EPS) * g_row
    return h * (1.0 + m_ref[scale_row:scale_row + 1, :]) + m_ref[shift_row:shift_row + 1, :]


def _head_pair_ones():
    r = lax.broadcasted_iota(I32, (LANES, LANES), 0) // HEAD_DIM
    c = lax.broadcasted_iota(I32, (LANES, LANES), 1) // HEAD_DIM
    return jnp.where(r == c, 1.0, 0.0).astype(BF16)


def _head_rms(y, g_row, ones_blk):
    outs = []
    for j in range(y.shape[1] // LANES):
        s = y[:, j * LANES:(j + 1) * LANES]
        hi, lo = _split_bf16(s * s)
        ms = (_dot(hi, ones_blk) + _dot(lo, ones_blk)) * (1.0 / HEAD_DIM)
        outs.append(s * lax.rsqrt(ms + EPS))
    return jnp.concatenate(outs, axis=1) * g_row


def _mod_row_map(n_ctx_tiles, n_batch):
    return lambda b, t: (jnp.where(t < n_ctx_tiles, n_batch, b), 0, 0)


def _proj_even_kernel(x_ref, m_ref, gn_ref, w_ref, a2_ref, ba2_ref, qg_ref, kg_ref,
                      gq_ref, gk_ref, gv_ref, gate_ref, la_ref, nq_ref, nk_ref, nv_ref):
    h = _norm_mod(x_ref[...], gn_ref[...], m_ref, 0, 1)
    y = _dot(h.astype(BF16), w_ref[...])
    c0 = 0
    gq_ref[...] = y[:, c0:c0 + GLA_QK] * (GLA_DK ** -0.5)
    c0 += GLA_QK
    gk_ref[...] = y[:, c0:c0 + GLA_QK]
    c0 += GLA_QK
    gv_ref[...] = y[:, c0:c0 + GLA_V].astype(BF16)
    c0 += GLA_V
    gate_ref[...] = y[:, c0:c0 + GLA_V].astype(BF16)
    c0 += GLA_V
    ones_blk = _head_pair_ones()
    nq_ref[...] = (_head_rms(y[:, c0:c0 + NA_W], qg_ref[...], ones_blk) * (HEAD_DIM ** -0.5)).astype(BF16)
    c0 += NA_W
    nk_ref[...] = _head_rms(y[:, c0:c0 + NA_W], kg_ref[...], ones_blk).astype(BF16)
    c0 += NA_W
    nv_ref[...] = y[:, c0:c0 + NA_W].astype(BF16)
    c0 += NA_W
    a = y[:, c0:c0 + LANES]
    z = _dot(a, a2_ref[...], precision=HIGHEST) + ba2_ref[...]
    la_ref[...] = (jnp.minimum(z, 0.0) - jnp.log1p(jnp.exp(-jnp.abs(z)))) * (1.0 / GLA_TAU)


def _proj_even(xa, mods, gn, w_bf, a2, ba2, qg, kg, n_ctx_tiles):
    b, tt, d = xa.shape
    tm = TOKEN_TILE
    n = w_bf.shape[1]
    tok = lambda w: pl.BlockSpec((None, tm, w), lambda bi, t: (bi, t, 0))
    full = lambda shape: pl.BlockSpec(shape, lambda bi, t: (0,) * len(shape))
    widths = [(GLA_QK, F32), (GLA_QK, F32), (GLA_V, BF16), (GLA_V, BF16), (2 * GLA_QK, F32),
              (NA_W, BF16), (NA_W, BF16), (NA_W, BF16)]
    return pl.pallas_call(
        _proj_even_kernel,
        grid=(b, tt // tm),
        in_specs=[
            tok(d),
            pl.BlockSpec((None, 6, d), _mod_row_map(n_ctx_tiles, b)),
            full((1, d)), full((d, n)), full((LANES, 2 * GLA_QK)), full((1, 2 * GLA_QK)),
            full((1, NA_W)), full((1, NA_W)),
        ],
        out_specs=[tok(w) for w, _ in widths],
        out_shape=[jax.ShapeDtypeStruct((b, tt, w), dt) for w, dt in widths],
        compiler_params=_params(("parallel", "parallel")),
        name="proj_even",
    )(xa, mods, gn, w_bf, a2, ba2, qg, kg)


def _rope(y, cos, sin_signed, lo_mask):
    w = y.shape[1]
    reps = w // LANES
    cos_t = jnp.concatenate([cos] * reps, axis=1) if reps > 1 else cos
    sin_t = jnp.concatenate([sin_signed] * reps, axis=1) if reps > 1 else sin_signed
    msk = jnp.concatenate([lo_mask] * reps, axis=1) if reps > 1 else lo_mask
    quarter = HEAD_DIM // 4
    swapped = jnp.where(msk, pltpu.roll(y, w - quarter, 1), pltpu.roll(y, quarter, 1))
    return y * cos_t + swapped * sin_t


def _proj_odd_kernel(x_ref, m_ref, gn_ref, w_ref, qg_ref, kg_ref, cos_ref, sin_ref,
                     q_ref, k_ref, v_ref):
    h = _norm_mod(x_ref[...], gn_ref[...], m_ref, 0, 1)
    y = _dot(h.astype(BF16), w_ref[...])
    ones_blk = _head_pair_ones()
    lane = lax.broadcasted_iota(I32, (1, LANES), 1)
    lo_mask = (lane % (HEAD_DIM // 2)) < (HEAD_DIM // 4)
    cos = cos_ref[...]
    sin = sin_ref[...]
    q = _head_rms(y[:, 0:SW_Q], qg_ref[...], ones_blk)
    q_ref[...] = (_rope(q, cos, sin, lo_mask) * (HEAD_DIM ** -0.5)).astype(BF16)
    k = _head_rms(y[:, SW_Q:SW_Q + SW_KV], kg_ref[...], ones_blk)
    k_ref[...] = _rope(k, cos, sin, lo_mask).astype(BF16)
    v_ref[...] = y[:, SW_Q + SW_KV:SW_Q + 2 * SW_KV].astype(BF16)


def _proj_odd(xa, mods, gn, w_bf, qg, kg, cos, sin, n_ctx_tiles):
    b, tt, d = xa.shape
    tm = TOKEN_TILE
    n = w_bf.shape[1]
    tok = lambda w: pl.BlockSpec((None, tm, w), lambda bi, t: (bi, t, 0))
    full = lambda shape: pl.BlockSpec(shape, lambda bi, t: (0,) * len(shape))
    pos = pl.BlockSpec((tm, LANES), lambda bi, t: (t, 0))
    widths = [SW_Q, SW_KV, SW_KV]
    return pl.pallas_call(
        _proj_odd_kernel,
        grid=(b, tt // tm),
        in_specs=[
            tok(d),
            pl.BlockSpec((None, 6, d), _mod_row_map(n_ctx_tiles, b)),
            full((1, d)), full((d, n)), full((1, SW_Q)), full((1, SW_KV)), pos, pos,
        ],
        out_specs=[tok(w) for w in widths],
        out_shape=[jax.ShapeDtypeStruct((b, tt, w), BF16) for w in widths],
        compiler_params=_params(("parallel", "parallel")),
        name="proj_odd",
    )(xa, mods, gn, w_bf, qg, kg, cos, sin)


def _gla_direction(backward, q_ref, k_ref, v_ref, g_ref, o_ref, st_ref):
    c = GLA_CHUNK
    assert c == GLA_DK
    n_sub = c // GLA_SUB
    n_chunks = q_ref.shape[0] // c
    sgn = -1 if backward else 1
    r = lax.broadcasted_iota(I32, (2 * c, c), 0)
    j = lax.broadcasted_iota(I32, (2 * c, c), 1)
    i = r % c
    incl = jnp.where(sgn * (i - j) >= 0, 1.0, 0.0)
    blk = jnp.where(sgn * (i // GLA_SUB - j // GLA_SUB) > 0, 1.0, 0.0)
    scan_mat = jnp.where(r >= c, blk, incl).astype(BF16)
    causal = scan_mat[0:c, :].astype(F32)
    row = lax.broadcasted_iota(I32, (c, 1), 0)
    reachable = [sgn * ((a * GLA_SUB if backward else a * GLA_SUB + GLA_SUB - 1) - row) >= 0 for a in range(n_sub)]
    lane_head = lax.broadcasted_iota(I32, (GLA_HEADS * c, GLA_QK), 1) // GLA_DK
    row_idx = lax.broadcasted_iota(I32, (GLA_HEADS * c, GLA_QK), 0)
    own_chunk = jnp.where(row_idx // c == lane_head, 1.0, 0.0).astype(BF16)
    sub_shape = (GLA_HEADS * GLA_SUB, GLA_QK)
    own_sub_f32 = jnp.where(lax.broadcasted_iota(I32, sub_shape, 0) // GLA_SUB
                            == lax.broadcasted_iota(I32, sub_shape, 1) // GLA_DK, 1.0, 0.0)
    own_sub = own_sub_f32.astype(BF16)
    ones_cols = jnp.ones((c, GLA_DV), BF16)
    last = 0 if backward else c - 1

    for u in range(n_chunks):
        ci = n_chunks - 1 - u if backward else u
        rows = slice(ci * c, (ci + 1) * c)
        g_hi, g_lo = _split_bf16(g_ref[rows, :])
        cr = _dot(scan_mat, g_hi) + _dot(scan_mat, g_lo)
        cum = cr[0:c, :]
        ref_pt = cr[c:2 * c, :]
        tot = cum[last:last + 1, :]
        tot_rows = _dot_tn(g_hi, ones_cols) + _dot_tn(g_lo, ones_cols)

        q = q_ref[rows, :]
        k = k_ref[rows, :]
        q_state = (q * jnp.exp(cum)).astype(BF16)
        q_sub = (q * jnp.exp(cum - ref_pt)).astype(BF16)
        k_state = (k * jnp.exp(tot - cum)).astype(BF16)

        placed = []
        for a in range(n_sub):
            lo = a * GLA_SUB
            k_sub = (k * jnp.exp(jnp.where(reachable[a], ref_pt[lo:lo + 1, :] - cum, 0.0))).astype(BF16)
            q4 = jnp.concatenate([q_sub[lo:lo + GLA_SUB, :]] * GLA_HEADS, axis=0) * own_sub
            a4 = _dot_nt(q4, k_sub) * jnp.concatenate([causal[lo:lo + GLA_SUB, :]] * GLA_HEADS, axis=0)
            placed.append(jnp.concatenate([a4] * GLA_HEADS, axis=1) * own_sub_f32)
        attn = jnp.concatenate([placed[a][h * GLA_SUB:(h + 1) * GLA_SUB, :]
                                for h in range(GLA_HEADS) for a in range(n_sub)], axis=0).astype(BF16)
        v = v_ref[rows, :]
        v_rows = jnp.concatenate([v[:, h * GLA_DV:(h + 1) * GLA_DV] for h in range(GLA_HEADS)], axis=0)
        q4s = jnp.concatenate([q_state] * GLA_HEADS, axis=0) * own_chunk
        st = st_ref[...]
        o = _dot(attn, v_rows) + _dot(q4s, st.astype(BF16))
        for h in range(GLA_HEADS):
            o_ref[rows, h * GLA_DV:(h + 1) * GLA_DV] = o[h * c:(h + 1) * c, :]
        kv = _dot_tn(k_state, v)
        kv_own = jnp.concatenate([kv[h * GLA_DK:(h + 1) * GLA_DK, h * GLA_DV:(h + 1) * GLA_DV]
                                  for h in range(GLA_HEADS)], axis=0)
        st_ref[...] = st * jnp.exp(tot_rows) + kv_own


def _gla_kernel(qf_ref, kf_ref, vf_ref, gf_ref, qb_ref, kb_ref, vb_ref, gb_ref, of_ref, ob_ref, stf_ref, stb_ref):
    @pl.when(pl.program_id(1) == 0)
    def _():
        stf_ref[...] = jnp.zeros_like(stf_ref)
        stb_ref[...] = jnp.zeros_like(stb_ref)

    _gla_direction(False, qf_ref, kf_ref, vf_ref, gf_ref, of_ref, stf_ref)
    _gla_direction(True, qb_ref, kb_ref, vb_ref, gb_ref, ob_ref, stb_ref)


def _gla(gq, gk, gv, la, n_ctx):
    b, tt, _ = gq.shape
    rows = TOKEN_TILE
    nb = tt // rows
    n_ctx_blk = n_ctx // rows

    def bwd(s):
        return jnp.where(s < n_ctx_blk, n_ctx_blk - 1 - s, nb - 1 + n_ctx_blk - s)

    def tok(w, col, order):
        return pl.BlockSpec((None, rows, w), lambda bi, s: (bi, order(s), col))

    fwd = lambda s: s
    return pl.pallas_call(
        _gla_kernel,
        grid=(b, nb),
        in_specs=[tok(GLA_QK, 0, fwd), tok(GLA_QK, 0, fwd), tok(GLA_V, 0, fwd), tok(GLA_QK, 0, fwd),
                  tok(GLA_QK, 0, bwd), tok(GLA_QK, 0, bwd), tok(GLA_V, 0, bwd), tok(GLA_QK, 1, bwd)],
        out_specs=[tok(GLA_V, 0, fwd), tok(GLA_V, 0, bwd)],
        out_shape=[jax.ShapeDtypeStruct((b, tt, GLA_V), F32)] * 2,
        scratch_shapes=[pltpu.VMEM((GLA_HEADS * GLA_DK, GLA_DV), F32)] * 2,
        compiler_params=_params(("arbitrary", "arbitrary")),
        name="gla_scan",
    )(gq, gk, gv, la, gq, gk, gv, la)


def _na_bias_table(rpb, kh):
    n_heads = rpb.shape[0]
    qc = np.arange(GRID_W)[:, None]
    kc = np.arange(GRID_W)[None, :]
    win0 = np.clip(qc - NA_KW // 2, 0, GRID_W - NA_KW)
    valid = (kc >= win0) & (kc < win0 + NA_KW)
    side = GRID_W - NA_KW
    rp = jnp.pad(rpb, ((0, 0), (0, 0), (side, side)))
    base = jnp.stack([rp[:, :, GRID_W - 1 - c:2 * GRID_W - 1 - c] for c in range(GRID_W)], axis=2)
    base = jnp.where(valid[None, None], base, NEG)
    tab = jnp.stack([base[:, NA_KH - 1 - a:NA_KH - 1 - a + kh] for a in range(kh)], axis=1)
    tab = tab.reshape(n_heads // 2, 2, kh, kh, GRID_W, GRID_W).transpose(0, 2, 1, 4, 3, 5)
    return tab.reshape(n_heads // 2, kh, 2 * GRID_W, kh * GRID_W).astype(F32)


def _na_kernel(q_ref, k_ref, v_ref, bias_ref, o_ref, *, n_ctx, n_ctx_blk, rows_per_blk, n_rows, kh):
    t = pl.program_id(2)
    lane = lax.broadcasted_iota(I32, (1, LANES), 1)
    first = lane < HEAD_DIM
    kc = k_ref[0:n_ctx, :]
    vc = v_ref[0:n_ctx, :]

    def stack_heads(q):
        z = jnp.zeros_like(q)
        return jnp.concatenate([jnp.where(first, q, z), jnp.where(first, z, q)], axis=0)

    def unstack_heads(o):
        n = o.shape[0] // 2
        return jnp.where(first, o[0:n], o[n:2 * n])

    @pl.when(t < n_ctx_blk)
    def _():
        s = _dot_nt(stack_heads(q_ref[...]), kc)
        m = jnp.max(s, axis=-1, keepdims=True)
        p = jnp.exp(s - m)
        den = jnp.sum(p, axis=-1, keepdims=True)
        o_ref[...] = unstack_heads(_dot(p.astype(BF16), vc) / den).astype(o_ref.dtype)

    @pl.when(t >= n_ctx_blk)
    def _():
        qs_all = jnp.concatenate([stack_heads(q_ref[i * GRID_W:(i + 1) * GRID_W, :]) for i in range(rows_per_blk)],
                                 axis=0)
        sc_all = _dot_nt(qs_all, kc)
        nq = 2 * GRID_W
        parts, ctx_probs, dens = [], [], []
        for i in range(rows_per_blk):
            r = (t - n_ctx_blk) * rows_per_blk + i
            row0 = jnp.clip(r - kh // 2, 0, n_rows - kh)
            cls = r - row0
            start = pl.multiple_of(n_ctx + row0 * GRID_W, GRID_W)
            kb = k_ref[pl.ds(start, kh * GRID_W), :]
            vb = v_ref[pl.ds(start, kh * GRID_W), :]
            s = _dot_nt(qs_all[i * nq:(i + 1) * nq, :], kb) + bias_ref[cls]
            sc = sc_all[i * nq:(i + 1) * nq, :]
            m = jnp.maximum(jnp.max(s, axis=-1, keepdims=True), jnp.max(sc, axis=-1, keepdims=True))
            p = jnp.exp(s - m)
            pc = jnp.exp(sc - m)
            dens.append(jnp.sum(p, axis=-1, keepdims=True) + jnp.sum(pc, axis=-1, keepdims=True))
            ctx_probs.append(pc.astype(BF16))
            parts.append(_dot(p.astype(BF16), vb))
        o_ctx = _dot(jnp.concatenate(ctx_probs, axis=0), vc)
        for i in range(rows_per_blk):
            o = (parts[i] + o_ctx[i * nq:(i + 1) * nq, :]) / dens[i]
            o_ref[i * GRID_W:(i + 1) * GRID_W, :] = unstack_heads(o).astype(o_ref.dtype)


def _na(nq, nk, nv, bias_tab, n_ctx):
    b, tt, _ = nq.shape
    tq = TOKEN_TILE
    n_rows = (tt - n_ctx) // GRID_W
    kh = bias_tab.shape[1]
    kern = functools.partial(_na_kernel, n_ctx=n_ctx, n_ctx_blk=n_ctx // tq, rows_per_blk=tq // GRID_W,
                             n_rows=n_rows, kh=kh)
    return pl.pallas_call(
        kern,
        grid=(b, NA_HEADS // 2, tt // tq),
        in_specs=[
            pl.BlockSpec((None, tq, LANES), lambda bi, p, t: (bi, t, p)),
            pl.BlockSpec((None, tt, LANES), lambda bi, p, t: (bi, 0, p)),
            pl.BlockSpec((None, tt, LANES), lambda bi, p, t: (bi, 0, p)),
            pl.BlockSpec((None, kh, 2 * GRID_W, kh * GRID_W), lambda bi, p, t: (p, 0, 0, 0)),
        ],
        out_specs=pl.BlockSpec((None, tq, LANES), lambda bi, p, t: (bi, t, p)),
        out_shape=jax.ShapeDtypeStruct((b, tt, NA_W), BF16),
        compiler_params=_params(("parallel", "parallel", "arbitrary")),
        name="na_attn",
    )(nq, nk, nv, bias_tab)


def _swa_kernel(sink_ref, q_ref, k_ref, v_ref, o_ref, *, n_ctx, n_lat_blk):
    blk = SW_BLOCK
    n = pl.program_id(1)
    n_ctx_blk = n_ctx // blk
    jb = n - n_ctx_blk
    sb = jnp.clip(jb - 1, 0, n_lat_blk - 3)
    start = pl.multiple_of(n_ctx + sb * blk, blk)
    keys = jnp.concatenate([k_ref[pl.ds(start, 3 * blk), :], k_ref[0:n_ctx, :]], axis=0)
    vals = jnp.concatenate([v_ref[pl.ds(start, 3 * blk), :], v_ref[0:n_ctx, :]], axis=0)
    nk = 3 * blk + n_ctx
    iq = lax.broadcasted_iota(I32, (blk, nk), 0)
    ik = lax.broadcasted_iota(I32, (blk, nk), 1)
    offset = jnp.where(n < n_ctx_blk, 4 * SW_WINDOW, (jb - sb) * blk)
    dist = jnp.where(ik >= 3 * blk, 0, iq - ik + offset)
    bias = jnp.where(jnp.abs(dist) <= SW_WINDOW, 0.0, NEG)
    q = q_ref[...]
    group = SW_HEADS // SW_KV_HEADS
    bias_g = jnp.concatenate([bias] * group, axis=0)
    lane = lax.broadcasted_iota(I32, (1, LANES), 1)
    assert SW_KV_HEADS * HEAD_DIM == LANES
    for g in range(SW_KV_HEADS):
        gs = slice(g * HEAD_DIM, (g + 1) * HEAD_DIM)
        in_g = (lane // HEAD_DIM) == g
        heads = range(g * group, (g + 1) * group)
        blocks = []
        for h in heads:
            two = q[:, (h // 2) * LANES:(h // 2 + 1) * LANES]
            if h % 2 != g:
                two = jnp.concatenate([two[:, HEAD_DIM:], two[:, :HEAD_DIM]], axis=1)
            blocks.append(jnp.where(in_g, two, jnp.zeros_like(two)))
        qg = jnp.concatenate(blocks, axis=0)
        sink = jnp.concatenate([jnp.full((blk, 1), sink_ref[h], F32) for h in heads], axis=0)
        s = _dot_nt(qg, keys) + bias_g
        m = jnp.maximum(jnp.max(s, axis=-1, keepdims=True), sink)
        p = jnp.exp((s - m).astype(BF16))
        o2 = _dot(p, jnp.where(in_g, vals, jnp.ones_like(vals)))
        other = (1 - g) * HEAD_DIM
        den = o2[:, other:other + 1] + jnp.exp(sink - m)
        o = o2[:, gs] / den
        for pr in range(group // 2):
            pair = [o[(2 * pr + e) * blk:(2 * pr + e + 1) * blk, :] for e in range(2)]
            col = (g * group // 2 + pr) * LANES
            o_ref[:, col:col + LANES] = jnp.concatenate(pair, axis=1).astype(o_ref.dtype)


def _swa(q, k, v, sinks, n_ctx):
    b, tt, _ = q.shape
    blk = SW_BLOCK
    kern = functools.partial(_swa_kernel, n_ctx=n_ctx, n_lat_blk=(tt - n_ctx) // blk)
    return pl.pallas_call(
        kern,
        grid=(b, tt // blk),
        in_specs=[
            pl.BlockSpec(memory_space=pltpu.SMEM),
            pl.BlockSpec((None, blk, SW_Q), lambda bi, n: (bi, n, 0)),
            pl.BlockSpec((None, tt, SW_KV), lambda bi, n: (bi, 0, 0)),
            pl.BlockSpec((None, tt, SW_KV), lambda bi, n: (bi, 0, 0)),
        ],
        out_specs=pl.BlockSpec((None, blk, SW_Q), lambda bi, n: (bi, n, 0)),
        out_shape=jax.ShapeDtypeStruct((b, tt, SW_Q), BF16),
        compiler_params=_params(("parallel", "arbitrary")),
        name="swa_attn",
    )(sinks, q, k, v)


def _route(f, wr_ref, br_ref, before_ref, run_ref, idx_ref, wt_ref, rank_ref, cnt_ref, base_ref):
    tm = f.shape[0]
    f_hi, f_lo = _split_bf16(f)
    logits = _dot(f_hi, wr_ref[0]) + (_dot(f_lo, wr_ref[0]) + _dot(f_hi, wr_ref[1])) + br_ref[...]
    e_iota = lax.broadcasted_iota(I32, (tm, N_EXPERTS), 1).astype(F32)
    lane8 = lax.broadcasted_iota(I32, (tm, 8), 1)
    vals, hots = [], []
    idx_out = jnp.zeros((tm, 8), I32)
    for k in range(TOP_K):
        mx = jnp.max(logits, axis=-1, keepdims=True)
        am = jnp.min(jnp.where(logits == mx, e_iota, float(N_EXPERTS)), axis=-1, keepdims=True)
        hot = e_iota == am
        vals.append(mx)
        hots.append(hot)
        idx_out = jnp.where(lane8 == k, am.astype(I32), idx_out)
        logits = jnp.where(hot, -jnp.inf, logits)
    exps = [jnp.exp(v - vals[0]) for v in vals]
    den = exps[0] + exps[1] + exps[2] + exps[3]
    multihot = jnp.zeros((tm, N_EXPERTS), F32)
    wt_out = jnp.zeros((tm, 8), F32)
    for k in range(TOP_K):
        multihot = multihot + jnp.where(hots[k], 1.0, 0.0)
        wt_out = jnp.where(lane8 == k, exps[k] / den, wt_out)
    base = _dot(before_ref[...], multihot.astype(BF16)) + run_ref[...]
    rank_out = jnp.zeros((tm, 8), I32)
    for k in range(TOP_K):
        rk = jnp.sum(jnp.where(hots[k], base, 0.0), axis=-1, keepdims=True)
        rank_out = jnp.where(lane8 == k, rk.astype(I32), rank_out)
    base_ref[...] = run_ref[...]
    run_ref[...] = run_ref[...] + jnp.sum(multihot, axis=0, keepdims=True)
    idx_ref[...] = idx_out
    wt_ref[...] = wt_out
    rank_ref[...] = rank_out
    cnt_ref[...] = run_ref[...]


def _post_common(mix, x_ref, m_ref, fg_ref, wr_ref, br_ref, before_ref, xn_ref, f_ref, idx_ref, wt_ref, rank_ref,
                 cnt_ref, base_ref, run_ref):
    @pl.when((pl.program_id(0) == 0) & (pl.program_id(1) == 0))
    def _():
        run_ref[...] = jnp.zeros_like(run_ref)

    xn = x_ref[...] + m_ref[2:3, :] * mix
    xn_ref[...] = xn
    f = _norm_mod(xn, fg_ref[...], m_ref, 3, 4)
    f_ref[...] = f.astype(f_ref.dtype)
    _route(f, wr_ref, br_ref, before_ref, run_ref, idx_ref, wt_ref, rank_ref, cnt_ref, base_ref)


def _post_even_kernel(of_ref, ob_ref, gate_ref, na_ref, gng_ref, w_ref, x_ref, m_ref, fg_ref, wr_ref, br_ref,
                      before_ref, xn_ref, f_ref, idx_ref, wt_ref, rank_ref, cnt_ref, base_ref, run_ref):
    o = of_ref[...] + ob_ref[...]
    gate = gate_ref[...].astype(F32)
    gn = gng_ref[...]
    parts = []
    for h in range(GLA_HEADS):
        oh = o[:, h * GLA_DV:(h + 1) * GLA_DV]
        ms = jnp.mean(oh * oh, axis=-1, keepdims=True)
        parts.append(oh * lax.rsqrt(ms + EPS) * gn)
    gla = jnp.concatenate(parts, axis=1) * (gate * jax.nn.sigmoid(gate))
    mix = _dot(gla.astype(BF16), w_ref[0:GLA_V, :]) + _dot(na_ref[...], w_ref[GLA_V:GLA_V + NA_W, :])
    _post_common(mix, x_ref, m_ref, fg_ref, wr_ref, br_ref, before_ref, xn_ref, f_ref, idx_ref, wt_ref, rank_ref,
                 cnt_ref, base_ref, run_ref)


def _post_odd_kernel(o_ref, w_ref, x_ref, m_ref, fg_ref, wr_ref, br_ref, before_ref,
                     xn_ref, f_ref, idx_ref, wt_ref, rank_ref, cnt_ref, base_ref, run_ref):
    mix = _dot(o_ref[...], w_ref[...])
    _post_common(mix, x_ref, m_ref, fg_ref, wr_ref, br_ref, before_ref, xn_ref, f_ref, idx_ref, wt_ref, rank_ref,
                 cnt_ref, base_ref, run_ref)


def _post(even, mixer_outs, extra, w_bf, xa, mods, fg, wr, br, n_ctx_tiles):
    b, tt, d = xa.shape
    tm = TOKEN_TILE
    tok = lambda w: pl.BlockSpec((None, tm, w), lambda bi, t: (bi, t, 0))
    full = lambda shape: pl.BlockSpec(shape, lambda bi, t: (0,) * len(shape))
    if even:
        o_fwd, o_bwd, gate, na = mixer_outs
        lead_specs = [tok(GLA_V), tok(GLA_V), tok(GLA_V), tok(NA_W), full((1, GLA_DV))]
        lead_args = [o_fwd, o_bwd, gate, na, extra]
        kern = _post_even_kernel
    else:
        (o,) = mixer_outs
        lead_specs = [tok(SW_Q)]
        lead_args = [o]
        kern = _post_odd_kernel
    nt = tt // tm
    small = lambda: pl.BlockSpec((None, tm, 8), lambda bi, t: (bi, t, 0))
    before = jnp.asarray(np.tril(np.ones((tm, tm), np.float32), -1), BF16)
    return pl.pallas_call(
        kern,
        grid=(b, tt // tm),
        in_specs=lead_specs + [
            full(w_bf.shape), tok(d),
            pl.BlockSpec((None, 6, d), _mod_row_map(n_ctx_tiles, b)),
            full((1, d)), full((2, d, N_EXPERTS)), full((1, N_EXPERTS)), full((tm, tm)),
        ],
        out_specs=[tok(d), tok(d), small(), small(), small(), full((1, N_EXPERTS)),
                   pl.BlockSpec((None, 1, N_EXPERTS), lambda bi, t: (bi * nt + t, 0, 0))],
        out_shape=[
            jax.ShapeDtypeStruct((b, tt, d), F32), jax.ShapeDtypeStruct((b, tt, d), BF16),
            jax.ShapeDtypeStruct((b, tt, 8), I32), jax.ShapeDtypeStruct((b, tt, 8), F32),
            jax.ShapeDtypeStruct((b, tt, 8), I32), jax.ShapeDtypeStruct((1, N_EXPERTS), F32),
            jax.ShapeDtypeStruct((b * nt, 1, N_EXPERTS), F32),
        ],
        scratch_shapes=[pltpu.VMEM((1, N_EXPERTS), F32)],
        compiler_params=_params(("arbitrary", "arbitrary")),
        name="post_even" if even else "post_odd",
    )(*lead_args, w_bf, xa, mods, fg, wr, br, before)


def _stage_rows(tm):
    return tm * TOP_K + N_EXPERTS * (MOE_CHUNK + MOE_ALIGN)


WAIT_BATCH = 4


def _for_tile_chunks(rows_ref, tot_ref, g, max_chunks, fn):
    def per_chunk(j, c):
        fn(pl.multiple_of(rows_ref[g * max_chunks + j], MOE_ALIGN), pl.multiple_of(j * MOE_CHUNK, MOE_CHUNK))
        return c

    lax.fori_loop(0, tot_ref[g], per_chunk, 0)


STAGE_GROUP = 2 * LANES
STAGE_MIN_GROUPS = 6


def _for_used_groups(used_rows, max_groups, body):
    for n in range(STAGE_MIN_GROUPS, max_groups + 1):
        lo = 0 if n == STAGE_MIN_GROUPS else (n - 1) * STAGE_GROUP
        hi = n * STAGE_GROUP
        take = (used_rows > lo) & (used_rows <= hi) if lo else used_rows <= hi

        @pl.when(take)
        def _(n=n):
            body(n)


def _wait_chunks(n, make_copy):
    def batch(i, c):
        make_copy(WAIT_BATCH * MOE_CHUNK).wait()
        return c

    def single(i, c):
        make_copy(MOE_CHUNK).wait()
        return c

    lax.fori_loop(0, n // WAIT_BATCH, batch, 0)
    lax.fori_loop(0, n % WAIT_BATCH, single, 0)


def _dispatch_kernel(rows_ref, tot_ref, nch_ref, off_ref, lead_ref, cnt_ref, lo_ref, hi_ref, colt_ref, f_ref, xs_ref,
                     stage, carry, zeros_ref, sems, zsem):
    g = pl.program_id(0)
    n_tiles = pl.num_programs(0)
    ch = MOE_CHUNK
    buf = g % 2
    max_chunks = stage.shape[1] // ch

    def chunk_copy(which, slot_row, stage_row, rows=ch):
        return pltpu.make_async_copy(stage.at[which, pl.ds(stage_row, rows)], xs_ref.at[pl.ds(slot_row, rows)],
                                     sems.at[which])

    def wait_tile(tile, which):
        _wait_chunks(tot_ref[tile], lambda rows: chunk_copy(which, 0, 0, rows))

    @pl.when(g == 0)
    def _():
        carry[...] = jnp.zeros_like(carry)
        zeros_ref[...] = jnp.zeros_like(zeros_ref)
        stage[...] = jnp.zeros_like(stage)

        def pad_pass(do):
            def per_expert(e, c):
                lo = lo_ref[e]
                hi = hi_ref[e]
                n_full = (hi - lo) // ch

                def full(j, c2):
                    do(pltpu.make_async_copy(
                        zeros_ref, xs_ref.at[pl.ds(pl.multiple_of(hi - (j + 1) * ch, MOE_ALIGN), ch)], zsem))
                    return c2

                lax.fori_loop(0, n_full, full, 0)

                @pl.when((hi - lo) - n_full * ch > 0)
                def _():
                    do(pltpu.make_async_copy(zeros_ref.at[pl.ds(0, MOE_ALIGN)],
                                             xs_ref.at[pl.ds(pl.multiple_of(lo, MOE_ALIGN), MOE_ALIGN)], zsem))
                return c

            lax.fori_loop(0, lo_ref.shape[0], per_expert, 0)

        pad_pass(lambda cp: cp.start())
        pad_pass(lambda cp: cp.wait())

    colt = colt_ref[...]
    group = STAGE_GROUP

    def fill_stage(n_groups):
        for j in range(n_groups):
            r_iota = lax.broadcasted_iota(I32, (group, colt.shape[1]), 0) + j * group
            hit = jnp.zeros(r_iota.shape, F32)
            for k in range(TOP_K):
                hit = jnp.where(r_iota == colt[k:k + 1, :], 1.0, hit)
            stage[buf, j * group:(j + 1) * group, :] = _dot(hit.astype(BF16), f_ref[...]).astype(stage.dtype)

    _for_used_groups(tot_ref[g] * ch, stage.shape[1] // group, fill_stage)

    for e in range(N_EXPERTS):
        i = g * N_EXPERTS + e
        active = nch_ref[i] > 0
        ptr = off_ref[i]
        head = pl.ds(pl.multiple_of(ptr, MOE_ALIGN), MOE_ALIGN)
        old = carry[e]
        stage[buf, head, :] = stage[buf, head, :] + jnp.where(active, old, jnp.zeros_like(old))
        end = lead_ref[i] + cnt_ref[i]
        new_lead = end % MOE_ALIGN
        tail = pl.ds(pl.multiple_of(ptr + end - new_lead, MOE_ALIGN), MOE_ALIGN)
        kept = jnp.where(new_lead > 0, stage[buf, tail, :], jnp.zeros_like(old))
        carry[e] = jnp.where(active, kept, old)

    @pl.when(g > 0)
    def _():
        wait_tile(g - 1, 1 - buf)

    _for_tile_chunks(rows_ref, tot_ref, g, max_chunks,
                     lambda slot_row, stage_row: chunk_copy(buf, slot_row, stage_row).start())

    @pl.when(g == n_tiles - 1)
    def _():
        wait_tile(g, buf)


def _dispatch(rows, tot, nch, off, lead, cnt, pad_lo, pad_hi, colt, f2, n_slots):
    t, d = f2.shape
    tm = TOKEN_TILE
    rs = _stage_rows(tm)
    return pl.pallas_call(
        _dispatch_kernel,
        grid_spec=pltpu.PrefetchScalarGridSpec(
            num_scalar_prefetch=8,
            grid=(t // tm,),
            in_specs=[
                pl.BlockSpec((None, 8, tm), lambda g, *_: (g, 0, 0)),
                pl.BlockSpec((tm, d), lambda g, *_: (g, 0)),
            ],
            out_specs=pl.BlockSpec(memory_space=pl.ANY),
            scratch_shapes=[pltpu.VMEM((2, rs, d), BF16), pltpu.VMEM((N_EXPERTS, MOE_ALIGN, d), BF16),
                            pltpu.VMEM((MOE_CHUNK, d), BF16),
                            pltpu.SemaphoreType.DMA((2,)), pltpu.SemaphoreType.DMA(())],
        ),
        out_shape=jax.ShapeDtypeStruct((n_slots, d), BF16),
        compiler_params=_params(("arbitrary",)),
        name="moe_dispatch",
    )(rows, tot, nch, off, lead, cnt, pad_lo, pad_hi, colt, f2)


def _expert_kernel(be_ref, nu_ref, xs_ref, wgu_ref, bgu_ref, wd_ref, bd_ref, ys_ref, wgu_bf, wd_bf):
    i = pl.program_id(0)
    prev = be_ref[jnp.maximum(i - 1, 0)]
    fresh = (i == 0) | (be_ref[i] != prev)

    @pl.when(fresh & (i < nu_ref[0]))
    def _():
        wgu_bf[...] = wgu_ref[...].astype(BF16)
        wd_bf[...] = wd_ref[...].astype(BF16)

    @pl.when(i < nu_ref[0])
    def _():
        x = xs_ref[...]
        de = wd_bf.shape[0]
        half = de // 2
        acc = None
        for c in range(2):
            glu = _dot(x, wgu_bf[:, c * half:(c + 1) * half]) + bgu_ref[:, c * half:(c + 1) * half]
            lin = _dot(x, wgu_bf[:, de + c * half:de + (c + 1) * half]) + bgu_ref[:, de + c * half:de + (c + 1) * half]
            glu = jnp.minimum(glu, SWIGLU_LIMIT)
            lin = jnp.clip(lin, -SWIGLU_LIMIT, SWIGLU_LIMIT)
            act = (glu * jax.nn.sigmoid(SWIGLU_ALPHA * glu) * (lin + 1.0)).astype(BF16)
            part = _dot(act, wd_bf[c * half:(c + 1) * half, :])
            acc = part if acc is None else acc + part
        ys_ref[...] = (acc + bd_ref[...]).astype(ys_ref.dtype)

    @pl.when(i >= nu_ref[0])
    def _():
        ys_ref[...] = jnp.zeros_like(ys_ref)


def _experts(block_e, n_used, xs, layer, wgu, bgu, wd, bd):
    n_slots, d = xs.shape
    bm = EXPERT_BLOCK
    depth, ne, _, de2 = wgu.shape
    de = wd.shape[2]
    row_blk = lambda i, be, nu: (jnp.minimum(i, nu[0] - 1), 0)
    exp_blk = lambda i, be, nu: (layer, be[i], 0, 0)
    return pl.pallas_call(
        _expert_kernel,
        grid_spec=pltpu.PrefetchScalarGridSpec(
            num_scalar_prefetch=2,
            grid=(n_slots // bm,),
            in_specs=[
                pl.BlockSpec((bm, d), row_blk),
                pl.BlockSpec((None, None, d, de2), exp_blk),
                pl.BlockSpec((None, None, 1, de2), exp_blk),
                pl.BlockSpec((None, None, de, d), exp_blk),
                pl.BlockSpec((None, None, 1, d), exp_blk),
            ],
            out_specs=pl.BlockSpec((bm, d), lambda i, be, nu: (i, 0)),
            scratch_shapes=[pltpu.VMEM((d, de2), BF16), pltpu.VMEM((de, d), BF16)],
        ),
        out_shape=jax.ShapeDtypeStruct((n_slots, d), BF16),
        compiler_params=_params(("arbitrary",)),
        name="moe_experts",
    )(block_e, n_used, xs, wgu, bgu.reshape(depth, ne, 1, de2), wd, bd.reshape(depth, ne, 1, d))


def _combine_kernel(rows_ref, tot_ref, col_ref, wt_ref, x_ref, m_ref, ys_ref, o_ref, stage, sems):
    g = pl.program_id(0)
    n_tiles = pl.num_programs(0)
    ch = MOE_CHUNK
    buf = g % 2
    max_chunks = stage.shape[1] // ch

    def chunk_copy(which, slot_row, stage_row, rows=ch):
        return pltpu.make_async_copy(ys_ref.at[pl.ds(slot_row, rows)], stage.at[which, pl.ds(stage_row, rows)],
                                     sems.at[which])

    def fetch_tile(tile, which):
        _for_tile_chunks(rows_ref, tot_ref, tile, max_chunks,
                         lambda slot_row, stage_row: chunk_copy(which, slot_row, stage_row).start())

    @pl.when(g == 0)
    def _():
        stage[...] = jnp.zeros_like(stage)
        fetch_tile(0, 0)

    @pl.when(g + 1 < n_tiles)
    def _():
        fetch_tile(g + 1, 1 - buf)

    _wait_chunks(tot_ref[g], lambda rows: chunk_copy(buf, 0, 0, rows))

    col = col_ref[...]
    wt = wt_ref[...]
    tm = col.shape[0]
    group = STAGE_GROUP

    def weighted_sum(n_groups):
        y = jnp.zeros(x_ref.shape, F32)
        for j in range(n_groups):
            c_iota = lax.broadcasted_iota(I32, (tm, group), 1) + j * group
            w = jnp.zeros((tm, group), F32)
            for k in range(TOP_K):
                w = jnp.where(c_iota == col[:, k:k + 1], wt[:, k:k + 1], w)
            y = y + _dot(w.astype(BF16), stage[buf, j * group:(j + 1) * group, :])
        o_ref[...] = x_ref[...] + m_ref[5:6, :] * y

    _for_used_groups(tot_ref[g] * ch, stage.shape[1] // group, weighted_sum)


def _combine(rows, tot, col, wts, xn, mods, ys, n_ctx_tiles, latent_only):
    b, tt, d = xn.shape
    tm = TOKEN_TILE
    nt = tt // tm
    rs = _stage_rows(tm)
    assert rs % (2 * LANES) == 0
    if latent_only:
        n_out = nt - n_ctx_tiles
        out_rows = b * n_out * tm
        out_map = lambda g, *_: ((g // nt) * n_out + jnp.maximum(g % nt - n_ctx_tiles, 0), 0)
    else:
        out_rows = b * tt
        out_map = lambda g, *_: (g, 0)
    out = pl.pallas_call(
        _combine_kernel,
        grid_spec=pltpu.PrefetchScalarGridSpec(
            num_scalar_prefetch=2,
            grid=(b * nt,),
            in_specs=[
                pl.BlockSpec((tm, 8), lambda g, *_: (g, 0)),
                pl.BlockSpec((tm, 8), lambda g, *_: (g, 0)),
                pl.BlockSpec((tm, d), lambda g, *_: (g, 0)),
                pl.BlockSpec((None, 6, d), lambda g, *_: (jnp.where(g % nt < n_ctx_tiles, b, g // nt), 0, 0)),
                pl.BlockSpec(memory_space=pl.ANY),
            ],
            out_specs=pl.BlockSpec((tm, d), out_map),
            scratch_shapes=[pltpu.VMEM((2, rs, d), BF16), pltpu.SemaphoreType.DMA((2,))],
        ),
        out_shape=jax.ShapeDtypeStruct((out_rows, d), F32),
        compiler_params=_params(("arbitrary",)),
        name="moe_combine",
    )(rows, tot, col, wts.reshape(b * tt, 8), xn.reshape(b * tt, d), mods, ys)
    return out.reshape(b, out_rows // b, d)


def _moe(xn, f, idx, wts, rank, cnt, tile_base, mods, layer, wgu, bgu, wd, bd, n_ctx_tiles, latent_only):
    b, tt, d = xn.shape
    t = b * tt
    bm = EXPERT_BLOCK
    tm = TOKEN_TILE
    ch = MOE_CHUNK
    n_tiles = t // tm
    n_blocks = (t * TOP_K + N_EXPERTS * ch + bm - 1) // bm + N_EXPERTS
    e_ids = np.arange(N_EXPERTS)
    lower = e_ids[None, :] < e_ids[:, None]
    counts = cnt[0].astype(I32)
    padded = (counts + ch + bm - 1) // bm * bm
    pstart = jnp.sum(jnp.where(lower, padded[None, :], 0), axis=1)
    pend = pstart + padded
    base = tile_base.reshape(n_tiles, N_EXPERTS).astype(I32)
    tile_cnt = jnp.concatenate([base[1:], counts[None, :]], axis=0) - base
    first = pstart[None, :] + base
    seg = first // MOE_ALIGN * MOE_ALIGN
    lead = first - seg
    nch = jnp.where(tile_cnt > 0, (lead + tile_cnt + ch - 1) // ch, 0)
    off = jnp.sum(jnp.where(lower[None], nch[:, None, :], 0), axis=2) * ch
    idx4 = idx[..., :TOP_K].reshape(n_tiles, tm, TOP_K)
    rank4 = rank[..., :TOP_K].reshape(n_tiles, tm, TOP_K)
    delta = (off + lead - base)[:, None, None, :]
    col = rank4 + jnp.sum(jnp.where(idx4[..., None] == e_ids, delta, 0), axis=-1)
    col = jnp.concatenate([col, jnp.full((n_tiles, tm, 8 - TOP_K), -1, I32)], axis=-1)
    colt = col.transpose(0, 2, 1)
    n_used = (pend[N_EXPERTS - 1] // bm).reshape(1).astype(I32)
    starts = np.arange(n_blocks, dtype=np.int32)[:, None] * bm
    block_e = jnp.minimum(jnp.sum((pend[None, :] <= starts).astype(I32), axis=1), N_EXPERTS - 1)
    pad_lo = jnp.concatenate([(pstart + counts + MOE_ALIGN - 1) // MOE_ALIGN * MOE_ALIGN, pend[N_EXPERTS - 1:]])
    pad_hi = jnp.concatenate([pend, jnp.full((1,), n_blocks * bm, I32)])
    max_chunks = _stage_rows(tm) // ch
    cum = off // ch + nch
    j = np.arange(max_chunks, dtype=np.int32)
    e_of_j = jnp.sum((cum[:, None, :] <= j[None, :, None]).astype(I32), axis=-1)
    hot = e_of_j[..., None] == e_ids
    seg_j = jnp.sum(jnp.where(hot, seg[:, None, :], 0), axis=-1)
    first_j = jnp.sum(jnp.where(hot, (off // ch)[:, None, :], 0), axis=-1)
    rows = (seg_j + (j[None, :] - first_j) * ch).reshape(-1)
    tot = cum[:, N_EXPERTS - 1]
    xs = _dispatch(rows, tot, nch.reshape(-1), off.reshape(-1), lead.reshape(-1), tile_cnt.reshape(-1),
                   pad_lo, pad_hi, colt, f.reshape(t, d), n_blocks * bm)
    ys = _experts(block_e, n_used, xs, layer, wgu, bgu, wd, bd)
    return _combine(rows, tot, col.reshape(t, 8), wts, xn, mods, ys, n_ctx_tiles, latent_only)


def _rope_tables(n_ctx, s_len):
    quarter = HEAD_DIM // 4
    t = np.arange(s_len)
    pos = np.stack([t // GRID_W, t % GRID_W], axis=0).astype(np.float32)
    inv_freq = ROPE_BASE ** (-jnp.arange(quarter, dtype=F32) / quarter)
    ang = jnp.asarray(pos)[:, :, None] * inv_freq
    cos, sin = jnp.cos(ang), jnp.sin(ang)
    cos_h = jnp.concatenate([cos[0], cos[0], cos[1], cos[1]], axis=-1)
    sin_h = jnp.concatenate([-sin[0], sin[0], -sin[1], sin[1]], axis=-1)
    reps = LANES // HEAD_DIM
    cos_l = jnp.tile(cos_h, (1, reps))
    sin_l = jnp.tile(sin_h, (1, reps))
    cos_all = jnp.concatenate([jnp.ones((n_ctx, LANES), F32), cos_l], axis=0)
    sin_all = jnp.concatenate([jnp.zeros((n_ctx, LANES), F32), sin_l], axis=0)
    return cos_all, sin_all


def kernel(x, c, ctx, c_ctx, mod_w, mod_b, norm_mix_g, norm_ffn_g, ev_w_in, ev_w_out, ev_gla_w_a2, ev_gla_b_a2,
           ev_gla_norm_g, ev_na_q_g, ev_na_k_g, ev_na_rpb, od_w_in, od_w_out, od_q_g, od_k_g, od_sinks,
           moe_w_router, moe_b_router, moe_w_gate_up, moe_b_gate_up, moe_w_down, moe_b_down):
    b, s_len, d = x.shape
    n_ctx = ctx.shape[1]
    depth = mod_w.shape[0]
    assert b < MOD_ROWS and n_ctx % TOKEN_TILE == 0 and s_len % TOKEN_TILE == 0
    n_ctx_tiles = n_ctx // TOKEN_TILE
    n_rows = s_len // GRID_W
    kh = min(NA_KH, n_rows)

    xa = jnp.concatenate([ctx, x], axis=1)
    cv = jnp.zeros((MOD_ROWS, d), F32).at[:b].set(c).at[b].set(c_ctx)
    mods_all = _modulation(cv, mod_w, mod_b).reshape(depth, MOD_ROWS, 6, d)
    cos, sin = _rope_tables(n_ctx, s_len)

    for layer in range(depth):
        i = layer // 2
        mods = mods_all[layer]
        gn = norm_mix_g[layer].reshape(1, d)
        if layer % 2 == 0:
            w = ev_w_in[i]
            cuts = np.cumsum([0, GLA_QK, GLA_QK, GLA_V, GLA_V, GLA_RANK, GLA_RANK, NA_W, NA_W, NA_W])
            seg = lambda j: w[:, cuts[j]:cuts[j + 1]]
            pad = jnp.zeros((d, LANES - 2 * GLA_RANK), F32)
            w_bf = jnp.concatenate([seg(0), seg(1), seg(2), seg(3), seg(6), seg(7), seg(8), seg(4), seg(5), pad],
                                   axis=1).astype(BF16)
            a2 = jnp.zeros((LANES, 2 * GLA_QK), F32)
            a2 = a2.at[0:GLA_RANK, 0:GLA_QK].set(ev_gla_w_a2[i, 0])
            a2 = a2.at[GLA_RANK:2 * GLA_RANK, GLA_QK:].set(ev_gla_w_a2[i, 1])
            ba2 = ev_gla_b_a2[i].reshape(1, 2 * GLA_QK)
            qg = jnp.tile(ev_na_q_g[i], NA_HEADS).reshape(1, NA_W)
            kg = jnp.tile(ev_na_k_g[i], NA_HEADS).reshape(1, NA_W)
            gq, gk, gv, gate, la, nq, nk, nv = _proj_even(xa, mods, gn, w_bf, a2, ba2, qg, kg, n_ctx_tiles)
            o_fwd, o_bwd = _gla(gq, gk, gv, la, n_ctx)
            o_na = _na(nq, nk, nv, _na_bias_table(ev_na_rpb[i], kh), n_ctx)
            mixer_outs = (o_fwd, o_bwd, gate, o_na)
            extra = ev_gla_norm_g[i].reshape(1, GLA_DV)
            w_out = ev_w_out[i].astype(BF16)
        else:
            w_bf = od_w_in[i].astype(BF16)
            qg = jnp.tile(od_q_g[i], SW_HEADS).reshape(1, SW_Q)
            kg = jnp.tile(od_k_g[i], SW_KV_HEADS).reshape(1, SW_KV)
            q, k, v = _proj_odd(xa, mods, gn, w_bf, qg, kg, cos, sin, n_ctx_tiles)
            mixer_outs = (_swa(q, k, v, od_sinks[i], n_ctx),)
            extra = None
            w_out = od_w_out[i].astype(BF16)
        xn, f, idx, wts, rank, cnt, tile_base = _post(
            layer % 2 == 0, mixer_outs, extra, w_out, xa, mods, norm_ffn_g[layer].reshape(1, d),
            jnp.stack(_split_bf16(moe_w_router[layer])), moe_b_router[layer].reshape(1, N_EXPERTS), n_ctx_tiles)
        xa = _moe(xn, f, idx, wts, rank, cnt, tile_base, mods, layer, moe_w_gate_up, moe_b_gate_up, moe_w_down, moe_b_down,
                  n_ctx_tiles, latent_only=layer == depth - 1)
    return xa
```
